```python
import jax, jax.numpy as jnp
from jax import lax
import numpy as np

D_MODEL = 1024
BATCH = 1
SEQ = 16384
DEPTH = 4
DEC_BATCH = 32
DEC_SEQ = 32
PAST_LEN = 1024

CHUNK = 64
Q_BLOCK = 128
ROPE_THETA = 10000.0
EPS = 1e-6
N_EVEN = (DEPTH + 1) // 2
N_ODD = DEPTH // 2
A_HEADS = 4
A_DK = 64
A_DV = 128
B_HEADS = 8
B_DH = 64
C_HEADS = 8
C_KV_HEADS = 2
C_GROUP = C_HEADS // C_KV_HEADS
C_DH = 64
IDX_HEADS = 8
IDX_DIM = 64
IDX_SCALE = (IDX_HEADS * IDX_DIM) ** -0.5
TOPK_MAX = 256
D_HEADS = 4
D_DK = 64
D_DV = 128
D_GATE_RANK = 16
D_GATE_NORM = 16.0
D_FF = 4 * D_MODEL

EVEN_SPLITS = (A_HEADS * A_DK, A_HEADS * A_DK, A_HEADS * A_DV, A_HEADS * A_DV,
               B_HEADS * B_DH, B_HEADS * B_DH, B_HEADS * B_DH, B_HEADS)
EVEN_IN = sum(EVEN_SPLITS)
EVEN_MIX = A_HEADS * A_DV + B_HEADS * B_DH
ODD_SPLITS = (C_HEADS * C_DH, C_KV_HEADS * C_DH, C_KV_HEADS * C_DH,
              IDX_HEADS * IDX_DIM, IDX_DIM, IDX_HEADS,
              D_HEADS * D_DK, D_HEADS * D_DK, D_HEADS * D_DV, D_HEADS * D_DV, D_GATE_RANK)
ODD_IN = sum(ODD_SPLITS)
ODD_MIX = C_HEADS * C_DH + D_HEADS * D_DV

kernel_name = 'hybrid_stream_ret_fox_dsa_gla_step'

F32 = jnp.float32


def split_cols(h, sizes):
    offs = [int(o) for o in np.cumsum(sizes)[:-1]]
    return jnp.split(h, offs, axis=-1)


def rmsnorm(x, g):
    xf = x.astype(F32)
    y = xf * lax.rsqrt(jnp.mean(xf * xf, axis=-1, keepdims=True) + EPS) * g.astype(F32)
    return y.astype(x.dtype)


def rms_only(x):
    xf = x.astype(F32)
    return (xf * lax.rsqrt(jnp.mean(xf * xf, axis=-1, keepdims=True) + EPS)).astype(x.dtype)


def rope(x, pos):
    half = x.shape[-1] // 2
    inv = jnp.power(ROPE_THETA, -jnp.arange(half, dtype=F32) / half)
    ang = pos.astype(F32)[:, None] * inv[None, :]
    cos = jnp.cos(ang)[None, :, None, :]
    sin = jnp.sin(ang)[None, :, None, :]
    xf = x.astype(F32)
    x1, x2 = xf[..., :half], xf[..., half:]
    return jnp.concatenate([x1 * cos - x2 * sin, x2 * cos + x1 * sin], axis=-1).astype(x.dtype)


def retention_log_gamma():
    return jnp.log1p(-jnp.exp2(-5.0 - jnp.arange(A_HEADS, dtype=F32)))


def to_chunks(a, n, chunk):
    b, t, h, d = a.shape
    return a.reshape(b, n, chunk, h, d).transpose(1, 0, 3, 2, 4)


def from_chunks(o):
    n, b, h, c, d = o.shape
    return o.transpose(1, 0, 3, 2, 4).reshape(b, n * c, h, d)


def to_blocks(a, nb):
    return a.reshape((a.shape[0], nb, Q_BLOCK) + a.shape[2:]).swapaxes(0, 1)


def from_blocks(o):
    return o.swapaxes(0, 1).reshape((o.shape[1], -1) + o.shape[3:])


def retention_scan(q, k, v, log_gamma, s0, chunk):
    b, t, h, dk = q.shape
    n = t // chunk
    qc, kc, vc = (to_chunks(a.astype(F32), n, chunk) for a in (q, k, v))
    idx = jnp.arange(chunk, dtype=F32)
    lg = log_gamma[:, None, None]
    diff = idx[:, None] - idx[None, :]
    decay = jnp.where(diff >= 0, jnp.exp(jnp.maximum(diff, 0.0)[None] * lg), 0.0)
    q_dec = jnp.exp((idx + 1.0)[None, :, None] * lg)
    k_dec = jnp.exp((chunk - 1.0 - idx)[None, :, None] * lg)
    c_dec = jnp.exp(chunk * lg)

    def step(s, inp):
        qi, ki, vi = inp
        inter = jnp.einsum('bhcd,bhde->bhce', qi * q_dec, s)
        att = jnp.einsum('bhid,bhjd->bhij', qi, ki) * decay
        intra = jnp.einsum('bhij,bhje->bhie', att, vi)
        s = c_dec * s + jnp.einsum('bhjd,bhje->bhde', ki * k_dec, vi)
        return s, inter + intra

    s, o = lax.scan(step, s0.astype(F32), (qc, kc, vc))
    return from_chunks(o).astype(v.dtype), s


def gla_scan(q, k, v, log_a, s0, chunk):
    b, t, h, dk = q.shape
    n = t // chunk
    qc, kc, vc = (to_chunks(a.astype(F32), n, chunk) for a in (q, k, v))
    ac = to_chunks(log_a.astype(F32), n, chunk)
    causal = jnp.tril(jnp.ones((chunk, chunk), dtype=bool))[:, :, None]

    def step(s, inp):
        qi, ki, vi, ai = inp
        cum = jnp.cumsum(ai, axis=2)
        inter = jnp.einsum('bhcd,bhde->bhce', qi * jnp.exp(cum), s)
        diff = cum[:, :, :, None, :] - cum[:, :, None, :, :]
        dec = jnp.exp(jnp.where(causal, diff, -jnp.inf))
        att = jnp.einsum('bhid,bhjd,bhijd->bhij', qi, ki, dec)
        intra = jnp.einsum('bhij,bhje->bhie', att, vi)
        last = cum[:, :, -1:, :]
        s = jnp.exp(last[:, :, 0, :])[..., None] * s + jnp.einsum('bhjd,bhje->bhde', ki * jnp.exp(last - cum), vi)
        return s, inter + intra

    s, o = lax.scan(step, s0.astype(F32), (qc, kc, vc, ac))
    return from_chunks(o).astype(v.dtype), s


def fox_attend(q, k, v, cq, ck, qpos, kpos):
    s = jnp.einsum('bthd,bshd->bhts', q, k).astype(F32) * (q.shape[-1] ** -0.5)
    s = s + cq.transpose(0, 2, 1)[:, :, :, None] - ck.transpose(0, 2, 1)[:, :, None, :]
    s = jnp.where((kpos[None, :] <= qpos[:, None])[None, None], s, -jnp.inf)
    p = jax.nn.softmax(s, axis=-1)
    return jnp.einsum('bhts,bshd->bthd', p.astype(v.dtype), v)


def fox_prompt(q, k, v, c):
    t = q.shape[1]
    nb = t // Q_BLOCK
    kpos = jnp.arange(t, dtype=jnp.int32)
    o = lax.map(lambda a: fox_attend(a[0], k, v, a[1], c, a[2], kpos),
                (to_blocks(q, nb), to_blocks(c, nb), kpos.reshape(nb, Q_BLOCK)))
    return from_blocks(o)


def dsa_attend(q, iq, iw, qpos, k, v, ik, kpos, k_sel):
    b, tq, hq, dh = q.shape
    lim = (qpos // CHUNK + 1) * CHUNK
    score = jax.nn.relu(jnp.einsum('bthd,bsd->bths', iq, ik).astype(F32) * IDX_SCALE)
    score = jnp.einsum('bths,bth->bts', score, iw.astype(F32))
    score = jnp.where((kpos[None, :] < lim[:, None])[None], score, -jnp.inf)
    _, sel = lax.top_k(score, k_sel)
    valid = sel < lim[None, :, None]
    gather = jax.vmap(lambda rows, ids: rows[ids])
    kg = gather(k, sel)
    vg = gather(v, sel)
    qg = q.reshape(b, tq, C_KV_HEADS, C_GROUP, dh)
    s = jnp.einsum('bthgd,btnhd->bthgn', qg, kg).astype(F32) * (dh ** -0.5)
    s = jnp.where(valid[:, :, None, None, :], s, -jnp.inf)
    p = jax.nn.softmax(s, axis=-1)
    o = jnp.einsum('bthgn,btnhd->bthgd', p.astype(v.dtype), vg)
    return o.reshape(b, tq, hq, dh)


def dsa_prompt(q, iq, iw, k, v, ik, k_sel):
    t = q.shape[1]
    nb = t // Q_BLOCK
    kpos = jnp.arange(t, dtype=jnp.int32)
    o = lax.map(lambda a: dsa_attend(a[0], a[1], a[2], a[3], k, v, ik, kpos, k_sel),
                (to_blocks(q, nb), to_blocks(iq, nb), to_blocks(iw, nb), kpos.reshape(nb, Q_BLOCK)))
    return from_blocks(o)


def even_mix(xn, pos, w_in, b_f, q_g, k_g, w_out, s_a, past_k, past_v, past_logf):
    bsz, t, _ = xn.shape
    qa, ka, va, ga, qb, kb, vb, fb = split_cols(xn @ w_in, EVEN_SPLITS)
    chunk = min(CHUNK, t)
    qa = rope(qa.reshape(bsz, t, A_HEADS, A_DK), pos)
    ka = rope(ka.reshape(bsz, t, A_HEADS, A_DK), pos) * (A_DK ** -0.5)
    va = va.reshape(bsz, t, A_HEADS, A_DV)
    oa, s_a_new = retention_scan(qa, ka, va, retention_log_gamma(), s_a, chunk)
    oa = (rms_only(oa) * jax.nn.silu(ga.reshape(bsz, t, A_HEADS, A_DV))).reshape(bsz, t, A_HEADS * A_DV)
    qb = rmsnorm(qb.reshape(bsz, t, B_HEADS, B_DH), q_g)
    kb = rmsnorm(kb.reshape(bsz, t, B_HEADS, B_DH), k_g)
    vb = vb.reshape(bsz, t, B_HEADS, B_DH)
    logf = jax.nn.log_sigmoid(fb.astype(F32) + b_f.astype(F32))
    if past_k is None:
        ob = fox_prompt(qb, kb, vb, jnp.cumsum(logf, axis=1))
    else:
        k_all = jnp.concatenate([past_k.astype(kb.dtype), kb], axis=1)
        v_all = jnp.concatenate([past_v.astype(vb.dtype), vb], axis=1)
        c_all = jnp.cumsum(jnp.concatenate([past_logf.astype(F32), logf], axis=1), axis=1)
        n_keys = k_all.shape[1]
        ob = fox_attend(qb, k_all, v_all, c_all[:, n_keys - t:], c_all, pos,
                        jnp.arange(n_keys, dtype=jnp.int32))
    mixed = jnp.concatenate([oa.astype(xn.dtype), ob.reshape(bsz, t, B_HEADS * B_DH).astype(xn.dtype)], axis=-1)
    return (mixed @ w_out).astype(xn.dtype), s_a_new, kb, vb, logf


def odd_mix(xn, pos, w_in, w_gate_up, b_gate, q_g, k_g, o_g, w_out, s_d, past_k, past_v, past_idx):
    bsz, t, _ = xn.shape
    qc, kc, vc, iq, ik, iw, qd, kd, vd, gd, gr = split_cols(xn @ w_in, ODD_SPLITS)
    qc = rope(rmsnorm(qc.reshape(bsz, t, C_HEADS, C_DH), q_g), pos)
    kc = rope(rmsnorm(kc.reshape(bsz, t, C_KV_HEADS, C_DH), k_g), pos)
    vc = vc.reshape(bsz, t, C_KV_HEADS, C_DH)
    iq = rope(iq.reshape(bsz, t, IDX_HEADS, IDX_DIM), pos)
    ik = rope(ik.reshape(bsz, t, 1, IDX_DIM), pos)[:, :, 0]
    if past_k is None:
        oc = dsa_prompt(qc, iq, iw, kc, vc, ik, min(TOPK_MAX, t // 4))
    else:
        k_all = jnp.concatenate([past_k.astype(kc.dtype), kc], axis=1)
        v_all = jnp.concatenate([past_v.astype(vc.dtype), vc], axis=1)
        ik_all = jnp.concatenate([past_idx.astype(ik.dtype), ik], axis=1)
        n_keys = k_all.shape[1]
        oc = dsa_attend(qc, iq, iw, pos, k_all, v_all, ik_all,
                        jnp.arange(n_keys, dtype=jnp.int32), min(TOPK_MAX, n_keys // 4))
    chunk = min(CHUNK, t)
    qd = qd.reshape(bsz, t, D_HEADS, D_DK) * (D_DK ** -0.5)
    kd = kd.reshape(bsz, t, D_HEADS, D_DK)
    vd = vd.reshape(bsz, t, D_HEADS, D_DV)
    log_a = jax.nn.log_sigmoid((gr @ w_gate_up + b_gate).astype(F32)).reshape(bsz, t, D_HEADS, D_DK) / D_GATE_NORM
    od, s_d_new = gla_scan(qd, kd, vd, log_a, s_d, chunk)
    od = (rmsnorm(od, o_g) * jax.nn.silu(gd.reshape(bsz, t, D_HEADS, D_DV))).reshape(bsz, t, D_HEADS * D_DV)
    mixed = jnp.concatenate([oc.reshape(bsz, t, C_HEADS * C_DH).astype(xn.dtype), od.astype(xn.dtype)], axis=-1)
    return (mixed @ w_out).astype(xn.dtype), s_d_new, kc, vc, ik


def sq_relu_mlp(x, g, w1, w2):
    h = jax.nn.relu(rmsnorm(x, g) @ w1)
    return (h * h) @ w2


def setup_inputs(seed: int = 0) -> dict:
    key = jax.random.key(seed)
    ks = jax.random.split(key, 32)

    def nrm(k, shape, scale):
        return jax.random.normal(k, shape, F32) * scale

    return {
        'x_prompt': nrm(ks[0], (BATCH, SEQ, D_MODEL), 1.0),
        'x_sample': nrm(ks[1], (DEC_BATCH, DEC_SEQ, D_MODEL), 1.0),
        'state_a': nrm(ks[2], (N_EVEN, DEC_BATCH, A_HEADS, A_DK, A_DV), 0.5),
        'cache_b_k': nrm(ks[3], (N_EVEN, DEC_BATCH, PAST_LEN, B_HEADS, B_DH), 1.0),
        'cache_b_v': nrm(ks[4], (N_EVEN, DEC_BATCH, PAST_LEN, B_HEADS, B_DH), 1.0),
        'cache_b_logf': jax.nn.log_sigmoid(3.0 + nrm(ks[5], (N_EVEN, DEC_BATCH, PAST_LEN, B_HEADS), 1.0)),
        'cache_c_k': nrm(ks[6], (N_ODD, DEC_BATCH, PAST_LEN, C_KV_HEADS, C_DH), 1.0),
        'cache_c_v': nrm(ks[7], (N_ODD, DEC_BATCH, PAST_LEN, C_KV_HEADS, C_DH), 1.0),
        'cache_c_idx': nrm(ks[8], (N_ODD, DEC_BATCH, PAST_LEN, IDX_DIM), 1.0),
        'state_d': nrm(ks[9], (N_ODD, DEC_BATCH, D_HEADS, D_DK, D_DV), 0.5),
        'norm_mix': 1.0 + nrm(ks[10], (DEPTH, D_MODEL), 0.02),
        'norm_ffn': 1.0 + nrm(ks[11], (DEPTH, D_MODEL), 0.02),
        'even_w_in': nrm(ks[12], (N_EVEN, D_MODEL, EVEN_IN), D_MODEL ** -0.5),
        'even_b_f': 3.0 + nrm(ks[13], (N_EVEN, B_HEADS), 0.1),
        'even_q_norm': 1.0 + nrm(ks[14], (N_EVEN, B_DH), 0.02),
        'even_k_norm': 1.0 + nrm(ks[15], (N_EVEN, B_DH), 0.02),
        'even_w_out': nrm(ks[16], (N_EVEN, EVEN_MIX, D_MODEL), EVEN_MIX ** -0.5),
        'odd_w_in': nrm(ks[17], (N_ODD, D_MODEL, ODD_IN), D_MODEL ** -0.5),
        'odd_w_gate_up': nrm(ks[18], (N_ODD, D_GATE_RANK, D_HEADS * D_DK), D_GATE_RANK ** -0.5),
        'odd_b_gate': nrm(ks[19], (N_ODD, D_HEADS * D_DK), 0.1),
        'odd_q_norm': 1.0 + nrm(ks[20], (N_ODD, C_DH), 0.02),
        'odd_k_norm': 1.0 + nrm(ks[21], (N_ODD, C_DH), 0.02),
        'odd_o_norm': 1.0 + nrm(ks[22], (N_ODD, D_DV), 0.02),
        'odd_w_out': nrm(ks[23], (N_ODD, ODD_MIX, D_MODEL), ODD_MIX ** -0.5),
        'ffn_w1': nrm(ks[24], (DEPTH, D_MODEL, D_FF), D_MODEL ** -0.5),
        'ffn_w2': nrm(ks[25], (DEPTH, D_FF, D_MODEL), D_FF ** -0.5),
    }


def reference(x_prompt, x_sample, state_a, cache_b_k, cache_b_v, cache_b_logf, cache_c_k, cache_c_v,
              cache_c_idx, state_d, norm_mix, norm_ffn, even_w_in, even_b_f, even_q_norm, even_k_norm,
              even_w_out, odd_w_in, odd_w_gate_up, odd_b_gate, odd_q_norm, odd_k_norm, odd_o_norm,
              odd_w_out, ffn_w1, ffn_w2):
    bp, tp = x_prompt.shape[:2]
    ts = x_sample.shape[1]
    past = cache_b_k.shape[2]
    pos_p = jnp.arange(tp, dtype=jnp.int32)
    pos_s = past + jnp.arange(ts, dtype=jnp.int32)
    yp, ys = x_prompt, x_sample
    a_p, a_s, bk_p, bk_s, bv_p, bv_s, bf_p, bf_s = [], [], [], [], [], [], [], []
    ck_p, ck_s, cv_p, cv_s, ci_p, ci_s, d_p, d_s = [], [], [], [], [], [], [], []
    for l in range(DEPTH):
        i = l // 2
        if l % 2 == 0:
            o, sa, kb, vb, lf = even_mix(rmsnorm(yp, norm_mix[l]), pos_p, even_w_in[i], even_b_f[i],
                                         even_q_norm[i], even_k_norm[i], even_w_out[i],
                                         jnp.zeros((bp, A_HEADS, A_DK, A_DV), F32), None, None, None)
            yp = yp + o
            a_p.append(sa); bk_p.append(kb); bv_p.append(vb); bf_p.append(lf)
            o, sa, kb, vb, lf = even_mix(rmsnorm(ys, norm_mix[l]), pos_s, even_w_in[i], even_b_f[i],
                                         even_q_norm[i], even_k_norm[i], even_w_out[i],
                                         state_a[i], cache_b_k[i], cache_b_v[i], cache_b_logf[i])
            ys = ys + o
            a_s.append(sa); bk_s.append(kb); bv_s.append(vb); bf_s.append(lf)
        else:
            o, sd, kc, vc, ik = odd_mix(rmsnorm(yp, norm_mix[l]), pos_p, odd_w_in[i], odd_w_gate_up[i],
                                        odd_b_gate[i], odd_q_norm[i], odd_k_norm[i], odd_o_norm[i],
                                        odd_w_out[i], jnp.zeros((bp, D_HEADS, D_DK, D_DV), F32),
                                        None, None, None)
            yp = yp + o
            d_p.append(sd); ck_p.append(kc); cv_p.append(vc); ci_p.append(ik)
            o, sd, kc, vc, ik = odd_mix(rmsnorm(ys, norm_mix[l]), pos_s, odd_w_in[i], odd_w_gate_up[i],
                                        odd_b_gate[i], odd_q_norm[i], odd_k_norm[i], odd_o_norm[i],
                                        odd_w_out[i], state_d[i], cache_c_k[i], cache_c_v[i], cache_c_idx[i])
            ys = ys + o
            d_s.append(sd); ck_s.append(kc); cv_s.append(vc); ci_s.append(ik)
        yp = yp + sq_relu_mlp(yp, norm_ffn[l], ffn_w1[l], ffn_w2[l])
        ys = ys + sq_relu_mlp(ys, norm_ffn[l], ffn_w1[l], ffn_w2[l])
    return (yp, ys,
            jnp.stack(a_p), jnp.stack(a_s),
            jnp.stack(bk_p), jnp.stack(bk_s),
            jnp.stack(bv_p), jnp.stack(bv_s),
            jnp.stack(bf_p), jnp.stack(bf_s),
            jnp.stack(ck_p), jnp.stack(ck_s),
            jnp.stack(cv_p), jnp.stack(cv_s),
            jnp.stack(ci_p), jnp.stack(ci_s),
            jnp.stack(d_p), jnp.stack(d_s))
```

```python
import functools

import numpy as np
import jax
import jax.numpy as jnp
from jax import lax
from jax.experimental import pallas as pl
from jax.experimental.pallas import tpu as pltpu

F32 = jnp.float32
BF16 = jnp.bfloat16
I32 = jnp.int32

D_MODEL = 1024
CHUNK = 64
ROPE_THETA = 10000.0
EPS = 1e-6
A_HEADS, A_DK, A_DV = 4, 64, 128
B_HEADS, B_DH = 8, 64
C_HEADS, C_KV_HEADS, C_DH = 8, 2, 64
IDX_HEADS, IDX_DIM = 8, 64
IDX_SCALE = (IDX_HEADS * IDX_DIM) ** -0.5
TOPK_MAX = 256
D_HEADS, D_DK, D_DV = 4, 64, 128
D_GATE_RANK = 16
D_GATE_NORM = 16.0
D_FF = 4 * D_MODEL

LANES = 128
SUBLANES = 8
VMEM_LIMIT_BYTES = 56 * 1024 * 1024

HEAD = 64
SUB = 16
NEG = -1e30
INT_MIN = -(2 ** 31)

EV_QK, EV_VA, EV_GA, EV_QB, EV_KB, EV_VB, EV_FB = 0, 512, 1024, 1536, 2048, 2560, 3072
EV_WIDTH = 3200
OD_QC, OD_IQ, OD_VD, OD_GD, OD_QKD, OD_KVC, OD_MISC = 0, 512, 1024, 1536, 2048, 2560, 2816
OD_WIDTH = 2944
MISC_IK, MISC_IW, MISC_GR = 0, 64, 72


def _cparams(sem):
    return pltpu.CompilerParams(dimension_semantics=sem, vmem_limit_bytes=VMEM_LIMIT_BYTES)


def _dot(a, b):
    return jnp.dot(a, b, preferred_element_type=F32)


def _dot_nt(a, b):
    return lax.dot_general(a, b, (((1,), (1,)), ((), ())), preferred_element_type=F32)


def _dot_tn(a, b):
    return lax.dot_general(a, b, (((0,), (0,)), ((), ())), preferred_element_type=F32)


def _split3(x):
    a1 = x.astype(BF16)
    r1 = x - a1.astype(F32)
    a2 = r1.astype(BF16)
    a3 = (r1 - a2.astype(F32)).astype(BF16)
    return a1, a2, a3


def _half_masks():
    lane = lax.broadcasted_iota(I32, (1, LANES), 1)
    lo = (lane < HEAD).astype(F32)
    return lo, 1.0 - lo


def _log_sigmoid(x):
    return jnp.minimum(x, 0.0) - jnp.log1p(jnp.exp(-jnp.abs(x)))


def _silu(x):
    return x / (1.0 + jnp.exp(-x))


def _norm_mm_kernel(x_ref, g_ref, w_ref, o_ref, *, relu2):
    x = x_ref[...]
    ms = jnp.mean(x * x, axis=-1, keepdims=True)
    xn = (x * lax.rsqrt(ms + EPS) * g_ref[...]).astype(BF16)
    y = _dot(xn, w_ref[...])
    if relu2:
        y = jnp.maximum(y, 0.0)
        y = y * y
    o_ref[...] = y.astype(o_ref.dtype)


def norm_matmul(x, g, w, *, tm, tn, relu2=False, out_dtype=F32, name):
    rows, k = x.shape
    n = w.shape[1]
    return pl.pallas_call(
        functools.partial(_norm_mm_kernel, relu2=relu2),
        grid=(rows // tm, n // tn),
        in_specs=[pl.BlockSpec((tm, k), lambda i, j: (i, 0)),
                  pl.BlockSpec((1, k), lambda i, j: (0, 0)),
                  pl.BlockSpec((k, tn), lambda i, j: (0, j))],
        out_specs=pl.BlockSpec((tm, tn), lambda i, j: (i, j)),
        out_shape=jax.ShapeDtypeStruct((rows, n), out_dtype),
        compiler_params=_cparams(("parallel", "parallel")),
        name=name,
    )(x, g.reshape(1, k), w)


def _mm_res_kernel(*refs, n_in):
    res_ref = refs[0]
    o_ref = refs[-1]
    acc = res_ref[...]
    for a_ref, w_ref in zip(refs[1:1 + n_in], refs[1 + n_in:1 + 2 * n_in]):
        acc = acc + _dot(a_ref[...].astype(BF16), w_ref[...])
    o_ref[...] = acc


def matmul_residual(res, a_list, w_list, *, tm, name):
    rows, n = res.shape
    n_in = len(a_list)
    in_specs = [pl.BlockSpec((tm, n), lambda i: (i, 0))]
    in_specs += [pl.BlockSpec((tm, a.shape[1]), lambda i: (i, 0)) for a in a_list]
    in_specs += [pl.BlockSpec(w.shape, lambda i: (0, 0)) for w in w_list]
    return pl.pallas_call(
        functools.partial(_mm_res_kernel, n_in=n_in),
        grid=(rows // tm,),
        in_specs=in_specs,
        out_specs=pl.BlockSpec((tm, n), lambda i: (i, 0)),
        out_shape=jax.ShapeDtypeStruct((rows, n), F32),
        compiler_params=_cparams(("parallel",)),
        name=name,
    )(res, *a_list, *w_list)


def _tile_lanes(t, width):
    n = width // LANES
    return t if n == 1 else jnp.concatenate([t] * n, axis=1)


def _rope(x, cos, sin_lo, sin_hi):
    w = x.shape[1]
    up = pltpu.roll(x, w - HEAD // 2, 1)
    dn = pltpu.roll(x, HEAD // 2, 1)
    return x * _tile_lanes(cos, w) + up * _tile_lanes(sin_lo, w) + dn * _tile_lanes(sin_hi, w)


def _head_rms(x, gmat):
    s = x * x
    hi = s.astype(BF16)
    lo = (s - hi.astype(F32)).astype(BF16)
    ms = (_dot(hi, gmat) + _dot(lo, gmat)) * (1.0 / HEAD)
    return x * lax.rsqrt(ms + EPS)


def _prep_even_kernel(qk_ref, qb_ref, kb_ref, fb_ref, cos_ref, slo_ref, shi_ref,
                      qg_ref, kg_ref, bf_ref, gmat_ref,
                      qka_ref, qbn_ref, kbn_ref, lf_ref):
    cos, slo, shi = cos_ref[...], slo_ref[...], shi_ref[...]
    lane = lax.broadcasted_iota(I32, (1, 4 * LANES), 1)
    kscale = jnp.where(lane < 2 * LANES, 1.0, A_DK ** -0.5)
    qka_ref[...] = _rope(qk_ref[...], cos, slo, shi) * kscale
    gmat = gmat_ref[...]
    qbn_ref[...] = _head_rms(qb_ref[...], gmat) * (qg_ref[...] * (B_DH ** -0.5))
    kbn_ref[...] = _head_rms(kb_ref[...], gmat) * kg_ref[...]
    lf_ref[...] = _log_sigmoid(fb_ref[...] + bf_ref[...])


def prep_even(h, cos, slo, shi, qg, kg, bf, gmat, *, tm):
    rows = h.shape[0]
    blk = lambda w, c: pl.BlockSpec((tm, w), lambda i, c=c: (i, c))
    row = lambda w: pl.BlockSpec((1, w), lambda i: (0, 0))
    return pl.pallas_call(
        _prep_even_kernel,
        grid=(rows // tm,),
        in_specs=[blk(512, EV_QK // 512), blk(512, EV_QB // 512), blk(512, EV_KB // 512),
                  blk(LANES, EV_FB // LANES), blk(LANES, 0), blk(LANES, 0), blk(LANES, 0),
                  row(512), row(512), row(LANES),
                  pl.BlockSpec((512, 512), lambda i: (0, 0))],
        out_specs=[blk(512, 0), blk(512, 0), blk(512, 0), blk(LANES, 0)],
        out_shape=[jax.ShapeDtypeStruct((rows, 512), F32)] * 3
        + [jax.ShapeDtypeStruct((rows, LANES), F32)],
        compiler_params=_cparams(("parallel",)),
        name="prep_even",
    )(h, h, h, h, cos, slo, shi, qg, kg, bf, gmat)


def _prep_odd_kernel(qc_ref, iq_ref, qkd_ref, kvc_ref, misc_ref, cos_ref, slo_ref, shi_ref,
                     qg_ref, kg_ref, wg_ref, bg_ref, gmat_ref,
                     qcr_ref, iqr_ref, qkdo_ref, kcr_ref, ikr_ref, kcb_ref, vcb_ref, ikb_ref,
                     la_ref):
    cos, slo, shi = cos_ref[...], slo_ref[...], shi_ref[...]
    gmat = gmat_ref[...]
    qc = _head_rms(qc_ref[...], gmat) * (qg_ref[...] * (C_DH ** -0.5))
    qcr_ref[...] = _rope(qc, cos, slo, shi)
    iqr_ref[...] = _rope(iq_ref[...], cos, slo, shi)
    lane = lax.broadcasted_iota(I32, (1, 4 * LANES), 1)
    qkdo_ref[...] = qkd_ref[...] * jnp.where(lane < 2 * LANES, D_DK ** -0.5, 1.0)
    kvc = kvc_ref[...]
    kc = _head_rms(kvc[:, :LANES], gmat[:LANES, :LANES]) * kg_ref[...]
    kcr = _rope(kc, cos, slo, shi)
    kcr_ref[...] = kcr
    kcb_ref[...] = kcr.astype(BF16)
    vcb_ref[...] = kvc[:, LANES:].astype(BF16)
    misc = misc_ref[...]
    ikr = _rope(misc, cos, slo, shi)
    lane1 = lax.broadcasted_iota(I32, (1, LANES), 1)
    ik2 = jnp.where(lane1 < HEAD, ikr, pltpu.roll(ikr, HEAD, 1))
    ikr_ref[...] = ik2
    ikb_ref[...] = ik2.astype(BF16)
    z = _dot(misc.astype(BF16), wg_ref[...]) + bg_ref[...]
    la_ref[...] = _log_sigmoid(z) * (1.0 / D_GATE_NORM)


def prep_odd(h, cos, slo, shi, qg, kg, wg, bg, gmat, *, tm):
    rows = h.shape[0]
    blk = lambda w, c: pl.BlockSpec((tm, w), lambda i, c=c: (i, c))
    row = lambda w: pl.BlockSpec((1, w), lambda i: (0, 0))
    return pl.pallas_call(
        _prep_odd_kernel,
        grid=(rows // tm,),
        in_specs=[blk(512, OD_QC // 512), blk(512, OD_IQ // 512), blk(512, OD_QKD // 512),
                  blk(256, OD_KVC // 256), blk(LANES, OD_MISC // LANES),
                  blk(LANES, 0), blk(LANES, 0), blk(LANES, 0),
                  row(512), row(LANES),
                  pl.BlockSpec((LANES, 256), lambda i: (0, 0)), row(256),
                  pl.BlockSpec((512, 512), lambda i: (0, 0))],
        out_specs=[blk(512, 0), blk(512, 0), blk(512, 0), blk(LANES, 0), blk(LANES, 0),
                   blk(LANES, 0), blk(LANES, 0), blk(LANES, 0), blk(256, 0)],
        out_shape=[jax.ShapeDtypeStruct((rows, 512), F32)] * 3
        + [jax.ShapeDtypeStruct((rows, LANES), F32)] * 2
        + [jax.ShapeDtypeStruct((rows, LANES), BF16)] * 3
        + [jax.ShapeDtypeStruct((rows, 256), F32)],
        compiler_params=_cparams(("parallel",)),
        name="prep_odd",
    )(h, h, h, h, h, cos, slo, shi, qg, kg, wg, bg, gmat)


def _scan_kernel(q_ref, k_ref, v_ref, g_ref, la_ref, gain_ref, s0_ref, tri_ref, bmask_ref,
                 o_ref, sfin_ref, st_ref, *, ts, nsteps):
    step = pl.program_id(1)

    @pl.when(step == 0)
    def _():
        st_ref[...] = s0_ref[0]

    q, k, v = q_ref[...], k_ref[...], v_ref[...]
    tri = tri_ref[...]
    a1, a2, a3 = _split3(la_ref[...])
    cum = _dot(tri, a1) + _dot(tri, a2) + _dot(tri, a3)
    qt = q * jnp.exp(cum)
    kt = k * jnp.exp(-cum)
    hm = _half_masks()
    bmask = bmask_ref[...]
    ri = lax.broadcasted_iota(I32, (SUB, SUB), 0)
    ci = lax.broadcasted_iota(I32, (SUB, SUB), 1)
    causal = ci <= ri
    o_rows = []
    for u in range(ts // SUB):
        r0 = u * SUB
        last = cum[r0 + SUB - 1:r0 + SUB, :]
        k2 = k[r0:r0 + SUB, :] * jnp.exp(last - cum[r0:r0 + SUB, :])
        elast = jnp.exp(last)
        o_pairs = []
        for p in range(2):
            l0 = p * LANES
            qp = qt[r0:r0 + SUB, l0:l0 + LANES]
            kp = kt[r0:r0 + SUB, l0:l0 + LANES].astype(BF16)
            st = st_ref[p]
            inter = _dot_nt(qp.astype(BF16), st.astype(BF16))
            intra = []
            for hh in range(2):
                att = _dot_nt((qp * hm[hh]).astype(BF16), kp)
                att = jnp.where(causal, att, 0.0)
                c0 = (2 * p + hh) * LANES
                intra.append(_dot(att.astype(BF16), v[r0:r0 + SUB, c0:c0 + LANES].astype(BF16)))
            o_pairs.append(inter + jnp.concatenate(intra, axis=1))
            upd = _dot_tn(v[r0:r0 + SUB, 2 * l0:2 * l0 + 2 * LANES].astype(BF16),
                          k2[:, l0:l0 + LANES].astype(BF16))
            st_ref[p] = st * elast[:, l0:l0 + LANES] + upd * bmask
        o_rows.append(jnp.concatenate(o_pairs, axis=1))
    o = jnp.concatenate(o_rows, axis=0)
    g = g_ref[...]
    gain = gain_ref[...]
    outs = []
    for h in range(4):
        oh = o[:, h * LANES:(h + 1) * LANES]
        ms = jnp.mean(oh * oh, axis=-1, keepdims=True)
        outs.append(oh * lax.rsqrt(ms + EPS) * gain[:, h * LANES:(h + 1) * LANES]
                    * _silu(g[:, h * LANES:(h + 1) * LANES]))
    o_ref[...] = jnp.concatenate(outs, axis=1)

    @pl.when(step == nsteps - 1)
    def _():
        sfin_ref[0] = st_ref[...]


def _scan_consts(ts):
    r = np.arange(ts)
    tri = ((r[:, None] // SUB == r[None, :] // SUB) & (r[None, :] <= r[:, None])).astype(np.float32)
    row = np.arange(2 * LANES)[:, None] // LANES
    col = np.arange(LANES)[None, :] // HEAD
    bmask = (row == col).astype(np.float32)
    return jnp.asarray(tri, BF16), jnp.asarray(bmask, F32)


def linear_scan(q_arr, q_col, k_arr, k_col, v_arr, v_col, g_arr, g_col, la, gain, s0,
                *, row0, nb, t, ts, name):
    nsteps = t // ts
    blk0 = row0 // ts
    tri, bmask = _scan_consts(ts)
    src = lambda w, c: pl.BlockSpec((ts, w), lambda b, s, c=c: (blk0 + b * nsteps + s, c))
    loc = lambda w: pl.BlockSpec((ts, w), lambda b, s: (b * nsteps + s, 0))
    return pl.pallas_call(
        functools.partial(_scan_kernel, ts=ts, nsteps=nsteps),
        grid=(nb, nsteps),
        in_specs=[src(256, q_col), src(256, k_col), src(512, v_col), src(512, g_col), loc(256),
                  pl.BlockSpec((1, 512), lambda b, s: (0, 0)),
                  pl.BlockSpec((1, 2, 256, LANES), lambda b, s: (b, 0, 0, 0)),
                  pl.BlockSpec((ts, ts), lambda b, s: (0, 0)),
                  pl.BlockSpec((256, LANES), lambda b, s: (0, 0))],
        out_specs=[loc(512), pl.BlockSpec((1, 2, 256, LANES), lambda b, s: (b, 0, 0, 0))],
        out_shape=[jax.ShapeDtypeStruct((nb * t, 512), F32),
                   jax.ShapeDtypeStruct((nb, 2, 256, LANES), F32)],
        scratch_shapes=[pltpu.VMEM((2, 256, LANES), F32)],
        compiler_params=_cparams(("parallel", "arbitrary")),
        name=name,
    )(q_arr, k_arr, v_arr, g_arr, la, gain, s0, tri, bmask)


def _state_to_pairs(s):
    nb = s.shape[0]
    st = jnp.swapaxes(s, -1, -2).reshape(nb, 2, 2, LANES, HEAD)
    z = jnp.zeros_like(st[:, :, 0])
    top = jnp.concatenate([st[:, :, 0], z], axis=-1)
    bot = jnp.concatenate([z, st[:, :, 1]], axis=-1)
    return jnp.concatenate([top, bot], axis=-2)


def _pairs_to_state(sp):
    nb = sp.shape[0]
    h0 = sp[:, :, :LANES, :HEAD]
    h1 = sp[:, :, LANES:, HEAD:]
    st = jnp.stack([h0, h1], axis=2).reshape(nb, 4, LANES, HEAD)
    return jnp.swapaxes(st, -1, -2)


def _cumsum_kernel(x_ref, u_ref, o_ref, carry_ref):
    @pl.when(pl.program_id(1) == 0)
    def _():
        carry_ref[...] = jnp.zeros_like(carry_ref)

    u = u_ref[...]
    a1, a2, a3 = _split3(x_ref[0])
    c = _dot(a1, u) + _dot(a2, u) + _dot(a3, u) + carry_ref[:, 0:1]
    o_ref[0] = c
    carry_ref[...] = jnp.broadcast_to(c[:, -1:], carry_ref.shape)


def lane_cumsum(x, *, tb):
    nb, h, n = x.shape
    r = np.arange(tb)
    u = jnp.asarray((r[:, None] <= r[None, :]).astype(np.float32), BF16)
    return pl.pallas_call(
        _cumsum_kernel,
        grid=(nb, n // tb),
        in_specs=[pl.BlockSpec((1, h, tb), lambda b, j: (b, 0, j)),
                  pl.BlockSpec((tb, tb), lambda b, j: (0, 0))],
        out_specs=pl.BlockSpec((1, h, tb), lambda b, j: (b, 0, j)),
        out_shape=jax.ShapeDtypeStruct((nb, h, n), F32),
        scratch_shapes=[pltpu.VMEM((h, LANES), F32)],
        compiler_params=_cparams(("parallel", "arbitrary")),
        name="logf_cumsum",
    )(x, u)


def _softmax_update(s, v, m_ref, l_ref, acc_ref, idx):
    m_prev = m_ref[idx]
    m_new = jnp.maximum(m_prev, jnp.max(s, axis=1, keepdims=True))
    alpha = jnp.exp(m_prev - m_new)
    p = jnp.exp(s - m_new)
    l_ref[idx] = alpha * l_ref[idx] + jnp.sum(p, axis=1, keepdims=True)
    acc_ref[idx] = alpha * acc_ref[idx] + _dot(p.astype(BF16), v)
    m_ref[idx] = m_new


def _fox_kernel(qi_ref, kj_ref, q_ref, k_ref, v_ref, c_ref, o_ref, m_ref, l_ref, acc_ref,
                *, tq, tk, q_off):
    s_id = pl.program_id(2)
    qi = qi_ref[s_id]
    kj = kj_ref[s_id]
    j_last = (q_off + (qi + 1) * tq - 1) // tk

    @pl.when(kj == 0)
    def _():
        m_ref[...] = jnp.full_like(m_ref, NEG)
        l_ref[...] = jnp.zeros_like(l_ref)
        acc_ref[...] = jnp.zeros_like(acc_ref)

    q = q_ref[...]
    k = k_ref[...].astype(BF16)
    v = v_ref[...].astype(BF16)
    qpos = q_off + qi * tq + lax.broadcasted_iota(I32, (tq, tk), 0)
    kpos = kj * tk + lax.broadcasted_iota(I32, (tq, tk), 1)
    visible = kpos <= qpos
    hm = _half_masks()
    for hh in range(2):
        s = _dot_nt((q * hm[hh]).astype(BF16), k) - c_ref[0, 0, hh:hh + 1, :]
        s = jnp.where(visible, s, NEG)
        _softmax_update(s, v, m_ref, l_ref, acc_ref, hh)

    @pl.when(kj == j_last)
    def _():
        o_ref[...] = (acc_ref[0] / l_ref[0]) * hm[0] + (acc_ref[1] / l_ref[1]) * hm[1]


def fox_attention(q_arr, k_arr, v_arr, v_col0, c_t, *, q_row0, nb, tq_total, tk_total, q_off,
                  tq, tk, name):
    nq = tq_total // tq
    steps = [(i, j) for i in range(nq) for j in range((q_off + (i + 1) * tq - 1) // tk + 1)]
    qi = jnp.asarray([s[0] for s in steps], I32)
    kj = jnp.asarray([s[1] for s in steps], I32)
    qb0 = q_row0 // tq
    nkb = tk_total // tk
    grid_spec = pltpu.PrefetchScalarGridSpec(
        num_scalar_prefetch=2,
        grid=(nb, 4, len(steps)),
        in_specs=[
            pl.BlockSpec((tq, LANES), lambda b, p, s, qi, kj: (qb0 + b * nq + qi[s], p)),
            pl.BlockSpec((tk, LANES), lambda b, p, s, qi, kj: (b * nkb + kj[s], p)),
            pl.BlockSpec((tk, LANES), lambda b, p, s, qi, kj: (b * nkb + kj[s], v_col0 + p)),
            pl.BlockSpec((1, 1, 2, tk), lambda b, p, s, qi, kj: (b, p, 0, kj[s])),
        ],
        out_specs=pl.BlockSpec((tq, LANES), lambda b, p, s, qi, kj: (b * nq + qi[s], p)),
        scratch_shapes=[pltpu.VMEM((2, tq, 1), F32), pltpu.VMEM((2, tq, 1), F32),
                        pltpu.VMEM((2, tq, LANES), F32)],
    )
    return pl.pallas_call(
        functools.partial(_fox_kernel, tq=tq, tk=tk, q_off=q_off),
        grid_spec=grid_spec,
        out_shape=jax.ShapeDtypeStruct((nb * tq_total, 4 * LANES), F32),
        compiler_params=_cparams(("parallel", "parallel", "arbitrary")),
        name=name,
    )(qi, kj, q_arr, k_arr, v_arr, c_t)


def _sortable_key(x):
    b = lax.bitcast_convert_type(x + 0.0, I32)
    return jnp.where(b < 0, b ^ jnp.int32(0x7FFFFFFF), b)


def _dsa_kernel(iq_ref, misc_ref, qc_ref, kc_ref, vc_ref, ik_ref, o_ref,
                keys_ref, iqm_ref, m_ref, l_ref, acc_ref, thr_ref, jcut_ref,
                *, tq, tk, n_keys, q_off, ksel):
    i = pl.program_id(1)
    hm = _half_masks()
    row = lax.broadcasted_iota(I32, (tq, 1), 0)
    lim = jnp.minimum(((q_off + i * tq + row) // CHUNK + 1) * CHUNK, n_keys)
    lim_max = jnp.minimum(((q_off + (i + 1) * tq - 1) // CHUNK + 1) * CHUNK, n_keys)
    nkb = (lim_max + tk - 1) // tk
    ncol = tk // LANES

    iq = iq_ref[...]
    for hd in range(IDX_HEADS):
        c0 = (hd // 2) * LANES
        iqm_ref[hd] = (iq[:, c0:c0 + LANES] * hm[hd % 2]).astype(BF16)
    iw = misc_ref[:, MISC_IW:MISC_IW + IDX_HEADS] * IDX_SCALE

    def score_body(kb, carry):
        k0 = pl.multiple_of(kb * tk, tk)
        ikb = ik_ref[pl.ds(k0, tk), :]
        sc = jnp.zeros((tq, tk), F32)
        for hd in range(IDX_HEADS):
            a = _dot_nt(iqm_ref[hd], ikb)
            sc = sc + jnp.maximum(a, 0.0) * iw[:, hd:hd + 1]
        kpos = k0 + lax.broadcasted_iota(I32, (tq, tk), 1)
        keys_ref[kb] = jnp.where(kpos < lim, _sortable_key(sc), INT_MIN)
        return carry

    lax.fori_loop(0, nkb, score_body, 0)

    def count(pred):
        def body(kb, acc):
            for c in range(ncol):
                blk = keys_ref[kb, :, c * LANES:(c + 1) * LANES]
                kpos = kb * tk + c * LANES + lax.broadcasted_iota(I32, (tq, LANES), 1)
                acc = acc + jnp.where(pred(blk, kpos), 1, 0)
            return acc
        acc = lax.fori_loop(0, nkb, body, jnp.zeros((tq, LANES), I32))
        return jnp.sum(acc, axis=1, keepdims=True)

    thr_ref[...] = jnp.full((tq, 1), INT_MIN + 1, I32)
    jcut_ref[...] = jnp.full((tq, 1), 2 ** 30, I32)

    @pl.when(lim_max > ksel)
    def _():
        c0 = count(lambda blk, kpos: blk >= 0)
        res0 = jnp.where(c0 >= ksel, 0, INT_MIN).astype(I32)

        def bit_body(t, res):
            cand = res + jnp.left_shift(jnp.int32(1), 30 - t)
            cnt = count(lambda blk, kpos: blk >= cand)
            return jnp.where(cnt >= ksel, cand, res)

        res = lax.fori_loop(0, 31, bit_body, res0)
        thr = jnp.maximum(res, INT_MIN + 1)
        thr_ref[...] = thr
        need = ksel - count(lambda blk, kpos: blk > thr)
        n_eq = count(lambda blk, kpos: blk == thr)
        split = n_eq > need

        @pl.when(jnp.max(jnp.where(split, 1, 0)) > 0)
        def _():
            def idx_body(t, lo):
                cand = lo + jnp.left_shift(jnp.int32(1), 14 - t)
                cnt = count(lambda blk, kpos: (blk == thr) & (kpos < cand))
                return jnp.where(cnt < need, cand, lo)

            lo = lax.fori_loop(0, 15, idx_body, jnp.zeros((tq, 1), I32))
            jcut_ref[...] = jnp.where(split, lo, 2 ** 30)

    m_ref[...] = jnp.full_like(m_ref, NEG)
    l_ref[...] = jnp.zeros_like(l_ref)
    acc_ref[...] = jnp.zeros_like(acc_ref)
    qc = qc_ref[...]
    for hd in range(C_HEADS):
        c0 = (hd // 2) * LANES
        iqm_ref[hd] = (qc[:, c0:c0 + LANES] * hm[hd % 2]).astype(BF16)
    thr = thr_ref[...]
    jcut = jcut_ref[...]

    def attn_body(kb, carry):
        k0 = pl.multiple_of(kb * tk, tk)
        kblk = kc_ref[pl.ds(k0, tk), :]
        vblk = vc_ref[pl.ds(k0, tk), :]
        keys = keys_ref[kb]
        kpos = k0 + lax.broadcasted_iota(I32, (tq, tk), 1)
        sel = (keys > thr) | ((keys == thr) & (kpos <= jcut))
        for hd in range(C_HEADS):
            s = jnp.where(sel, _dot_nt(iqm_ref[hd], kblk), NEG)
            _softmax_update(s, vblk, m_ref, l_ref, acc_ref, hd)
        return carry

    lax.fori_loop(0, nkb, attn_body, 0)
    outs = []
    for c in range(4):
        outs.append((acc_ref[2 * c] / l_ref[2 * c]) * hm[0]
                    + (acc_ref[2 * c + 1] / l_ref[2 * c + 1]) * hm[1])
    o_ref[...] = jnp.concatenate(outs, axis=1)


def dsa_attention(iq_arr, misc_arr, misc_col, qc_arr, kc, vc, ik, *, q_row0, nb, tq_total,
                  n_keys, n_keys_pad, q_off, tq, tk, name):
    nq = tq_total // tq
    qb0 = q_row0 // tq
    ksel = min(TOPK_MAX, n_keys // 4)
    qspec = lambda w, c: pl.BlockSpec((tq, w), lambda b, i, c=c: (qb0 + b * nq + i, c))
    kspec = pl.BlockSpec((n_keys_pad, LANES), lambda b, i: (b, 0))
    return pl.pallas_call(
        functools.partial(_dsa_kernel, tq=tq, tk=tk, n_keys=n_keys, q_off=q_off, ksel=ksel),
        grid=(nb, nq),
        in_specs=[qspec(512, 0), qspec(LANES, misc_col), qspec(512, 0), kspec, kspec, kspec],
        out_specs=pl.BlockSpec((tq, 512), lambda b, i: (b * nq + i, 0)),
        out_shape=jax.ShapeDtypeStruct((nb * tq_total, 512), F32),
        scratch_shapes=[pltpu.VMEM((n_keys_pad // tk, tq, tk), I32),
                        pltpu.VMEM((8, tq, LANES), BF16),
                        pltpu.VMEM((8, tq, 1), F32), pltpu.VMEM((8, tq, 1), F32),
                        pltpu.VMEM((8, tq, LANES), F32),
                        pltpu.VMEM((tq, 1), I32), pltpu.VMEM((tq, 1), I32)],
        compiler_params=_cparams(("parallel", "arbitrary")),
        name=name,
    )(iq_arr, misc_arr, qc_arr, kc, vc, ik)


def _rope_tables(pos):
    half = HEAD // 2
    inv = jnp.power(ROPE_THETA, -jnp.arange(half, dtype=F32) / half)
    ang = pos.astype(F32)[:, None] * inv[None, :]
    cos, sin = jnp.cos(ang), jnp.sin(ang)
    z = jnp.zeros_like(sin)
    return (jnp.tile(cos, (1, 4)), jnp.tile(jnp.concatenate([-sin, z], 1), (1, 2)),
            jnp.tile(jnp.concatenate([z, sin], 1), (1, 2)))


def _head_mean_matrix():
    r = np.arange(512)
    return jnp.asarray((r[:, None] // HEAD == r[None, :] // HEAD).astype(np.float32), BF16)


def _pick_tile(n, prefs):
    for t in prefs:
        if n % t == 0:
            return t
    raise ValueError(f"no tile in {prefs} divides {n}")


def _pad_keys(x, n_pad):
    nb, n, w = x.shape
    if n_pad > n:
        x = jnp.concatenate([x, jnp.zeros((nb, n_pad - n, w), x.dtype)], axis=1)
    return x.reshape(nb * n_pad, w)


def kernel(x_prompt, x_sample, state_a, cache_b_k, cache_b_v, cache_b_logf, cache_c_k, cache_c_v,
           cache_c_idx, state_d, norm_mix, norm_ffn, even_w_in, even_b_f, even_q_norm, even_k_norm,
           even_w_out, odd_w_in, odd_w_gate_up, odd_b_gate, odd_q_norm, odd_k_norm, odd_o_norm,
           odd_w_out, ffn_w1, ffn_w2):
    bp, tp = x_prompt.shape[:2]
    nb, ts = x_sample.shape[:2]
    past = cache_b_k.shape[2]
    depth = norm_mix.shape[0]
    rp, rs = bp * tp, nb * ts
    rows = rp + rs
    assert bp == 1 and tp % 128 == 0 and ts % SUB == 0 and rp % ts == 0 and past % CHUNK == 0

    y = jnp.concatenate([x_prompt.reshape(rp, D_MODEL), x_sample.reshape(rs, D_MODEL)], axis=0)
    pos = jnp.concatenate([jnp.arange(tp, dtype=I32),
                           jnp.tile(past + jnp.arange(ts, dtype=I32), nb)])
    cos, slo, shi = _rope_tables(pos)
    gmat = _head_mean_matrix()

    tm = _pick_tile(rows, (512, 256, 128, 64, 32))
    tm_prep = _pick_tile(rows, (256, 128, 64, 32))
    ts_p = _pick_tile(tp, (128, 64, 32, 16))
    tq_fox = _pick_tile(tp, (512, 256, 128))
    tq_dsa = 128
    tk_dsa = _pick_tile(tp, (512, 256, 128))
    nk_s = past + ts
    nk_s_pad = -(-nk_s // 384) * 384
    tb_p = _pick_tile(tp, (512, 256, 128))

    lg = jnp.log1p(-jnp.exp2(-5.0 - jnp.arange(A_HEADS, dtype=F32)))
    la_ret = jnp.repeat(lg, A_DK)[None, :]
    ones_gain = jnp.ones((1, 512), F32)
    zero_state = jnp.zeros((bp, 2, 256, LANES), F32)

    a_p, a_s, bk_p, bk_s, bv_p, bv_s, bf_p, bf_s = [], [], [], [], [], [], [], []
    ck_p, ck_s, cv_p, cv_s, ci_p, ci_s, d_p, d_s = [], [], [], [], [], [], [], []

    for l in range(depth):
        i = l // 2
        if l % 2 == 0:
            w_in = jnp.concatenate(
                [even_w_in[i], jnp.zeros((D_MODEL, EV_WIDTH - even_w_in.shape[2]), F32)], 1).astype(BF16)
            h = norm_matmul(y, norm_mix[l], w_in, tm=tm, tn=EV_WIDTH, name="even_in_proj")
            bf = jnp.concatenate([even_b_f[i], jnp.zeros((LANES - B_HEADS,), F32)])[None, :]
            qka, qbn, kbn, lf = prep_even(
                h, cos, slo, shi, jnp.tile(even_q_norm[i], B_HEADS)[None, :],
                jnp.tile(even_k_norm[i], B_HEADS)[None, :], bf, gmat, tm=tm_prep)
            oa_p, sa_p = linear_scan(qka, 0, qka, 1, h, EV_VA // 512, h, EV_GA // 512,
                                     jnp.broadcast_to(la_ret, (rp, 256)), ones_gain, zero_state,
                                     row0=0, nb=bp, t=tp, ts=ts_p, name="retention_prompt")
            oa_s, sa_s = linear_scan(qka, 0, qka, 1, h, EV_VA // 512, h, EV_GA // 512,
                                     jnp.broadcast_to(la_ret, (rs, 256)), ones_gain,
                                     _state_to_pairs(state_a[i]),
                                     row0=rp, nb=nb, t=ts, ts=ts, name="retention_decode")
            lf8 = lf[:, :B_HEADS]
            c_p = lane_cumsum(lf8[:rp].T.reshape(bp, B_HEADS, tp), tb=tb_p)
            ob_p = fox_attention(qbn, kbn, h, EV_VB // LANES, c_p.reshape(bp, 4, 2, tp),
                                 q_row0=0, nb=bp, tq_total=tp, tk_total=tp, q_off=0,
                                 tq=tq_fox, tk=tq_fox, name="fox_prompt")
            kb_new = kbn[rp:].reshape(nb, ts, 512)
            vb_new = h[rp:, EV_VB:EV_VB + 512].reshape(nb, ts, 512)
            lf_all = jnp.concatenate([cache_b_logf[i], lf8[rp:].reshape(nb, ts, B_HEADS),
                                      jnp.zeros((nb, nk_s_pad - nk_s, B_HEADS), F32)], axis=1)
            c_s = lane_cumsum(jnp.swapaxes(lf_all, 1, 2), tb=nk_s_pad // 3)
            k_all = _pad_keys(jnp.concatenate([cache_b_k[i].reshape(nb, past, 512), kb_new], 1), nk_s_pad)
            v_all = _pad_keys(jnp.concatenate([cache_b_v[i].reshape(nb, past, 512), vb_new], 1), nk_s_pad)
            ob_s = fox_attention(qbn, k_all, v_all, 0, c_s.reshape(nb, 4, 2, nk_s_pad),
                                 q_row0=rp, nb=nb, tq_total=ts, tk_total=nk_s_pad, q_off=past,
                                 tq=ts, tk=nk_s_pad, name="fox_decode")
            w_out = even_w_out[i].astype(BF16)
            y = matmul_residual(y, [jnp.concatenate([oa_p, oa_s], 0), jnp.concatenate([ob_p, ob_s], 0)],
                                [w_out[:512], w_out[512:]], tm=tm, name="even_out_proj")
            a_p.append(_pairs_to_state(sa_p))
            a_s.append(_pairs_to_state(sa_s))
            bk_p.append(kbn[:rp].reshape(bp, tp, B_HEADS, B_DH))
            bk_s.append(kb_new.reshape(nb, ts, B_HEADS, B_DH))
            bv_p.append(h[:rp, EV_VB:EV_VB + 512].reshape(bp, tp, B_HEADS, B_DH))
            bv_s.append(vb_new.reshape(nb, ts, B_HEADS, B_DH))
            bf_p.append(lf8[:rp].reshape(bp, tp, B_HEADS))
            bf_s.append(lf8[rp:].reshape(nb, ts, B_HEADS))
        else:
            w = odd_w_in[i]
            offs = np.cumsum([0, 512, 128, 128, 512, 64, 8, 256, 256, 512, 512, 16])
            qc_w, kc_w, vc_w, iq_w, ik_w, iw_w, qd_w, kd_w, vd_w, gd_w, gr_w = [
                w[:, int(a):int(b)] for a, b in zip(offs[:-1], offs[1:])]
            qc_w = qc_w.reshape(D_MODEL, 2, 4, C_DH).transpose(0, 2, 1, 3).reshape(D_MODEL, 512)
            w_in = jnp.concatenate(
                [qc_w, iq_w, vd_w, gd_w, qd_w, kd_w, kc_w, vc_w, ik_w, iw_w, gr_w,
                 jnp.zeros((D_MODEL, OD_WIDTH - OD_MISC - 88), F32)], 1).astype(BF16)
            h = norm_matmul(y, norm_mix[l], w_in, tm=tm, tn=OD_WIDTH, name="odd_in_proj")
            wg = jnp.zeros((LANES, 256), F32).at[MISC_GR:MISC_GR + D_GATE_RANK].set(
                odd_w_gate_up[i]).astype(BF16)
            qcr, iqr, qkd, kcr, ikr, kcb, vcb, ikb, la = prep_odd(
                h, cos, slo, shi, jnp.tile(odd_q_norm[i], C_HEADS)[None, :],
                jnp.tile(odd_k_norm[i], C_KV_HEADS)[None, :], wg, odd_b_gate[i][None, :], gmat,
                tm=tm_prep)
            oc_p = dsa_attention(iqr, h, OD_MISC // LANES, qcr, kcb[:rp], vcb[:rp], ikb[:rp],
                                 q_row0=0, nb=bp, tq_total=tp, n_keys=tp, n_keys_pad=tp, q_off=0,
                                 tq=tq_dsa, tk=tk_dsa, name="dsa_prompt")

            def with_past(cache, new):
                return _pad_keys(jnp.concatenate(
                    [cache.reshape(nb, past, -1).astype(BF16), new[rp:].reshape(nb, ts, -1)], 1), nk_s_pad)

            ik_past = jnp.concatenate([cache_c_idx[i], cache_c_idx[i]], axis=-1)
            oc_s = dsa_attention(iqr, h, OD_MISC // LANES, qcr,
                                 with_past(cache_c_k[i], kcb), with_past(cache_c_v[i], vcb),
                                 with_past(ik_past, ikb),
                                 q_row0=rp, nb=nb, tq_total=ts, n_keys=nk_s, n_keys_pad=nk_s_pad,
                                 q_off=past, tq=ts, tk=384, name="dsa_decode")
            gain = jnp.tile(odd_o_norm[i], D_HEADS)[None, :]
            od_p, sd_p = linear_scan(qkd, 0, qkd, 1, h, OD_VD // 512, h, OD_GD // 512, la[:rp], gain,
                                     zero_state, row0=0, nb=bp, t=tp, ts=ts_p, name="gla_prompt")
            od_s, sd_s = linear_scan(qkd, 0, qkd, 1, h, OD_VD // 512, h, OD_GD // 512, la[rp:], gain,
                                     _state_to_pairs(state_d[i]), row0=rp, nb=nb, t=ts, ts=ts,
                                     name="gla_decode")
            w_out = odd_w_out[i]
            w_oc = w_out[:512].reshape(2, 4, C_DH, D_MODEL).transpose(1, 0, 2, 3).reshape(512, D_MODEL)
            y = matmul_residual(y, [jnp.concatenate([oc_p, oc_s], 0), jnp.concatenate([od_p, od_s], 0)],
                                [w_oc.astype(BF16), w_out[512:].astype(BF16)], tm=tm,
                                name="odd_out_proj")
            d_p.append(_pairs_to_state(sd_p))
            d_s.append(_pairs_to_state(sd_s))
            ck_p.append(kcr[:rp].reshape(bp, tp, C_KV_HEADS, C_DH))
            ck_s.append(kcr[rp:].reshape(nb, ts, C_KV_HEADS, C_DH))
            cv_p.append(h[:rp, OD_KVC + LANES:OD_KVC + 2 * LANES].reshape(bp, tp, C_KV_HEADS, C_DH))
            cv_s.append(h[rp:, OD_KVC + LANES:OD_KVC + 2 * LANES].reshape(nb, ts, C_KV_HEADS, C_DH))
            ci_p.append(ikr[:rp, :IDX_DIM].reshape(bp, tp, IDX_DIM))
            ci_s.append(ikr[rp:, :IDX_DIM].reshape(nb, ts, IDX_DIM))
        hid = norm_matmul(y, norm_ffn[l], ffn_w1[l].astype(BF16), tm=tm, tn=2048, relu2=True,
                          out_dtype=BF16, name="mlp_up")
        y = matmul_residual(y, [hid], [ffn_w2[l].astype(BF16)], tm=tm, name="mlp_down")

    return (y[:rp].reshape(bp, tp, D_MODEL), y[rp:].reshape(nb, ts, D_MODEL),
            jnp.stack(a_p), jnp.stack(a_s), jnp.stack(bk_p), jnp.stack(bk_s),
            jnp.stack(bv_p), jnp.stack(bv_s), jnp.stack(bf_p), jnp.stack(bf_s),
            jnp.stack(ck_p), jnp.stack(ck_s), jnp.stack(cv_p), jnp.stack(cv_s),
            jnp.stack(ci_p), jnp.stack(ci_s), jnp.stack(d_p), jnp.stack(d_s))
```

```python
import functools

import numpy as np
import jax
import jax.numpy as jnp
from jax import lax
from jax.experimental import pallas as pl
from jax.experimental.pallas import tpu as pltpu

F32 = jnp.float32
BF16 = jnp.bfloat16
I32 = jnp.int32

D_MODEL = 1024
CHUNK = 64
ROPE_THETA = 10000.0
EPS = 1e-6
A_HEADS, A_DK, A_DV = 4, 64, 128
B_HEADS, B_DH = 8, 64
C_HEADS, C_KV_HEADS, C_DH = 8, 2, 64
IDX_HEADS, IDX_DIM = 8, 64
IDX_SCALE = (IDX_HEADS * IDX_DIM) ** -0.5
TOPK_MAX = 256
D_HEADS, D_DK, D_DV = 4, 64, 128
D_GATE_RANK = 16
D_GATE_NORM = 16.0
D_FF = 4 * D_MODEL

LANES = 128
SUBLANES = 8
VMEM_LIMIT_BYTES = 56 * 1024 * 1024

HEAD = 64
SUB = 16
NEG = -1e30
LOG2E = 1.4426950408889634
INT_MIN = -(2 ** 31)

EV_QK, EV_VA, EV_GA, EV_QB, EV_KB, EV_VB, EV_FB = 0, 512, 1024, 1536, 2048, 2560, 3072
EV_WIDTH = 3200
OD_QC, OD_IQ, OD_VD, OD_GD, OD_QKD, OD_KVC, OD_MISC = 0, 512, 1024, 1536, 2048, 2560, 2816
OD_WIDTH = 2944
MISC_IK, MISC_IW, MISC_GR = 0, 64, 72


def _cparams(sem):
    return pltpu.CompilerParams(dimension_semantics=sem, vmem_limit_bytes=VMEM_LIMIT_BYTES)


def _dot(a, b):
    return jnp.dot(a, b, preferred_element_type=F32)


def _dot_nt(a, b):
    return lax.dot_general(a, b, (((1,), (1,)), ((), ())), preferred_element_type=F32)


def _dot_tn(a, b):
    return lax.dot_general(a, b, (((0,), (0,)), ((), ())), preferred_element_type=F32)


def _split3(x):
    a1 = x.astype(BF16)
    r1 = x - a1.astype(F32)
    a2 = r1.astype(BF16)
    a3 = (r1 - a2.astype(F32)).astype(BF16)
    return a1, a2, a3


def _half_masks():
    lane = lax.broadcasted_iota(I32, (1, LANES), 1)
    lo = (lane < HEAD).astype(F32)
    return lo, 1.0 - lo


def _log_sigmoid(x):
    return jnp.minimum(x, 0.0) - jnp.log1p(jnp.exp(-jnp.abs(x)))


def _silu(x):
    return x / (1.0 + jnp.exp(-x))


def _norm_mm_kernel(x_ref, g_ref, w_ref, o_ref, *, relu2):
    x = x_ref[...]
    ms = jnp.mean(x * x, axis=-1, keepdims=True)
    xn = (x * lax.rsqrt(ms + EPS) * g_ref[...]).astype(BF16)
    y = _dot(xn, w_ref[...])
    if relu2:
        y = jnp.maximum(y, 0.0)
        y = y * y
    o_ref[...] = y.astype(o_ref.dtype)


def norm_matmul(x, g, w, *, tm, tn, relu2=False, out_dtype=F32, name):
    rows, k = x.shape
    n = w.shape[1]
    return pl.pallas_call(
        functools.partial(_norm_mm_kernel, relu2=relu2),
        grid=(rows // tm, n // tn),
        in_specs=[pl.BlockSpec((tm, k), lambda i, j: (i, 0)),
                  pl.BlockSpec((1, k), lambda i, j: (0, 0)),
                  pl.BlockSpec((k, tn), lambda i, j: (0, j))],
        out_specs=pl.BlockSpec((tm, tn), lambda i, j: (i, j)),
        out_shape=jax.ShapeDtypeStruct((rows, n), out_dtype),
        compiler_params=_cparams(("parallel", "parallel")),
        name=name,
    )(x, g.reshape(1, k), w)


def _mm_res_kernel(*refs, n_in):
    res_ref = refs[0]
    o_ref = refs[-1]
    acc = res_ref[...]
    for a_ref, w_ref in zip(refs[1:1 + n_in], refs[1 + n_in:1 + 2 * n_in]):
        acc = acc + _dot(a_ref[...].astype(BF16), w_ref[...])
    o_ref[...] = acc


def matmul_residual(res, a_list, w_list, *, tm, name):
    rows, n = res.shape
    n_in = len(a_list)
    in_specs = [pl.BlockSpec((tm, n), lambda i: (i, 0))]
    in_specs += [pl.BlockSpec((tm, a.shape[1]), lambda i: (i, 0)) for a in a_list]
    in_specs += [pl.BlockSpec(w.shape, lambda i: (0, 0)) for w in w_list]
    return pl.pallas_call(
        functools.partial(_mm_res_kernel, n_in=n_in),
        grid=(rows // tm,),
        in_specs=in_specs,
        out_specs=pl.BlockSpec((tm, n), lambda i: (i, 0)),
        out_shape=jax.ShapeDtypeStruct((rows, n), F32),
        compiler_params=_cparams(("parallel",)),
        name=name,
    )(res, *a_list, *w_list)


def _tile_lanes(t, width):
    n = width // LANES
    return t if n == 1 else jnp.concatenate([t] * n, axis=1)


def _rope(x, cos, sin_lo, sin_hi):
    w = x.shape[1]
    up = pltpu.roll(x, w - HEAD // 2, 1)
    dn = pltpu.roll(x, HEAD // 2, 1)
    return x * _tile_lanes(cos, w) + up * _tile_lanes(sin_lo, w) + dn * _tile_lanes(sin_hi, w)


def _head_rms(x, gmat):
    s = x * x
    hi = s.astype(BF16)
    lo = (s - hi.astype(F32)).astype(BF16)
    ms = (_dot(hi, gmat) + _dot(lo, gmat)) * (1.0 / HEAD)
    return x * lax.rsqrt(ms + EPS)


def _prep_even_kernel(qk_ref, qb_ref, kb_ref, fb_ref, cos_ref, slo_ref, shi_ref,
                      qg_ref, kg_ref, bf_ref, gmat_ref,
                      qka_ref, qbn_ref, kbn_ref, lf_ref):
    cos, slo, shi = cos_ref[...], slo_ref[...], shi_ref[...]
    lane = lax.broadcasted_iota(I32, (1, 4 * LANES), 1)
    kscale = jnp.where(lane < 2 * LANES, 1.0, A_DK ** -0.5)
    qka_ref[...] = _rope(qk_ref[...], cos, slo, shi) * kscale
    gmat = gmat_ref[...]
    qbn_ref[...] = _head_rms(qb_ref[...], gmat) * (qg_ref[...] * (B_DH ** -0.5 * LOG2E))
    kbn_ref[...] = _head_rms(kb_ref[...], gmat) * kg_ref[...]
    lf_ref[...] = _log_sigmoid(fb_ref[...] + bf_ref[...])


def prep_even(h, cos, slo, shi, qg, kg, bf, gmat, *, tm):
    rows = h.shape[0]
    blk = lambda w, c: pl.BlockSpec((tm, w), lambda i, c=c: (i, c))
    row = lambda w: pl.BlockSpec((1, w), lambda i: (0, 0))
    return pl.pallas_call(
        _prep_even_kernel,
        grid=(rows // tm,),
        in_specs=[blk(512, EV_QK // 512), blk(512, EV_QB // 512), blk(512, EV_KB // 512),
                  blk(LANES, EV_FB // LANES), blk(LANES, 0), blk(LANES, 0), blk(LANES, 0),
                  row(512), row(512), row(LANES),
                  pl.BlockSpec((512, 512), lambda i: (0, 0))],
        out_specs=[blk(512, 0), blk(512, 0), blk(512, 0), blk(LANES, 0)],
        out_shape=[jax.ShapeDtypeStruct((rows, 512), F32)] * 3
        + [jax.ShapeDtypeStruct((rows, LANES), F32)],
        compiler_params=_cparams(("parallel",)),
        name="prep_even",
    )(h, h, h, h, cos, slo, shi, qg, kg, bf, gmat)


def _prep_odd_kernel(qc_ref, iq_ref, qkd_ref, kvc_ref, misc_ref, cos_ref, slo_ref, shi_ref,
                     qg_ref, kg_ref, wg_ref, bg_ref, gmat_ref,
                     qcr_ref, iqr_ref, qkdo_ref, kcr_ref, ikr_ref, kcb_ref, vcb_ref, ikb_ref,
                     la_ref):
    cos, slo, shi = cos_ref[...], slo_ref[...], shi_ref[...]
    gmat = gmat_ref[...]
    qc = _head_rms(qc_ref[...], gmat) * (qg_ref[...] * (C_DH ** -0.5 * LOG2E))
    qcr_ref[...] = _rope(qc, cos, slo, shi)
    iqr_ref[...] = _rope(iq_ref[...], cos, slo, shi)
    lane = lax.broadcasted_iota(I32, (1, 4 * LANES), 1)
    qkdo_ref[...] = qkd_ref[...] * jnp.where(lane < 2 * LANES, D_DK ** -0.5, 1.0)
    kvc = kvc_ref[...]
    kc = _head_rms(kvc[:, :LANES], gmat[:LANES, :LANES]) * kg_ref[...]
    kcr = _rope(kc, cos, slo, shi)
    kcr_ref[...] = kcr
    kcb_ref[...] = kcr.astype(BF16)
    vcb_ref[...] = kvc[:, LANES:].astype(BF16)
    misc = misc_ref[...]
    ikr = _rope(misc, cos, slo, shi)
    lane1 = lax.broadcasted_iota(I32, (1, LANES), 1)
    ik2 = jnp.where(lane1 < HEAD, ikr, pltpu.roll(ikr, HEAD, 1))
    ikr_ref[...] = ik2
    ikb_ref[...] = ik2.astype(BF16)
    z = _dot(misc.astype(BF16), wg_ref[...]) + bg_ref[...]
    la_ref[...] = _log_sigmoid(z) * (1.0 / D_GATE_NORM)


def prep_odd(h, cos, slo, shi, qg, kg, wg, bg, gmat, *, tm):
    rows = h.shape[0]
    blk = lambda w, c: pl.BlockSpec((tm, w), lambda i, c=c: (i, c))
    row = lambda w: pl.BlockSpec((1, w), lambda i: (0, 0))
    return pl.pallas_call(
        _prep_odd_kernel,
        grid=(rows // tm,),
        in_specs=[blk(512, OD_QC // 512), blk(512, OD_IQ // 512), blk(512, OD_QKD // 512),
                  blk(256, OD_KVC // 256), blk(LANES, OD_MISC // LANES),
                  blk(LANES, 0), blk(LANES, 0), blk(LANES, 0),
                  row(512), row(LANES),
                  pl.BlockSpec((LANES, 256), lambda i: (0, 0)), row(256),
                  pl.BlockSpec((512, 512), lambda i: (0, 0))],
        out_specs=[blk(512, 0), blk(512, 0), blk(512, 0), blk(LANES, 0), blk(LANES, 0),
                   blk(LANES, 0), blk(LANES, 0), blk(LANES, 0), blk(256, 0)],
        out_shape=[jax.ShapeDtypeStruct((rows, 512), F32)] * 3
        + [jax.ShapeDtypeStruct((rows, LANES), F32)] * 2
        + [jax.ShapeDtypeStruct((rows, LANES), BF16)] * 3
        + [jax.ShapeDtypeStruct((rows, 256), F32)],
        compiler_params=_cparams(("parallel",)),
        name="prep_odd",
    )(h, h, h, h, h, cos, slo, shi, qg, kg, wg, bg, gmat)


def _scan_kernel(q_ref, k_ref, v_ref, g_ref, la_ref, gain_ref, s0_ref, tri_ref, bmask_ref,
                 o_ref, sfin_ref, st_ref, *, ts, nsteps):
    step = pl.program_id(1)

    @pl.when(step == 0)
    def _():
        st_ref[...] = s0_ref[0]

    q, k, v = q_ref[...], k_ref[...], v_ref[...]
    tri = tri_ref[...]
    a1, a2, a3 = _split3(la_ref[...])
    cum = _dot(tri, a1) + _dot(tri, a2) + _dot(tri, a3)
    qt = q * jnp.exp(cum)
    kt = k * jnp.exp(-cum)
    hm = _half_masks()
    bmask = bmask_ref[...]
    ri = lax.broadcasted_iota(I32, (SUB, SUB), 0)
    ci = lax.broadcasted_iota(I32, (SUB, SUB), 1)
    causal = ci <= ri
    o_rows = []
    for u in range(ts // SUB):
        r0 = u * SUB
        last = cum[r0 + SUB - 1:r0 + SUB, :]
        k2 = k[r0:r0 + SUB, :] * jnp.exp(last - cum[r0:r0 + SUB, :])
        elast = jnp.exp(last)
        o_pairs = []
        for p in range(2):
            l0 = p * LANES
            qp = qt[r0:r0 + SUB, l0:l0 + LANES]
            kp = kt[r0:r0 + SUB, l0:l0 + LANES].astype(BF16)
            st = st_ref[p]
            inter = _dot_nt(qp.astype(BF16), st.astype(BF16))
            intra = []
            for hh in range(2):
                att = _dot_nt((qp * hm[hh]).astype(BF16), kp)
                att = jnp.where(causal, att, 0.0)
                c0 = (2 * p + hh) * LANES
                intra.append(_dot(att.astype(BF16), v[r0:r0 + SUB, c0:c0 + LANES].astype(BF16)))
            o_pairs.append(inter + jnp.concatenate(intra, axis=1))
            upd = _dot_tn(v[r0:r0 + SUB, 2 * l0:2 * l0 + 2 * LANES].astype(BF16),
                          k2[:, l0:l0 + LANES].astype(BF16))
            st_ref[p] = st * elast[:, l0:l0 + LANES] + upd * bmask
        o_rows.append(jnp.concatenate(o_pairs, axis=1))
    o = jnp.concatenate(o_rows, axis=0)
    g = g_ref[...]
    gain = gain_ref[...]
    outs = []
    for h in range(4):
        oh = o[:, h * LANES:(h + 1) * LANES]
        ms = jnp.mean(oh * oh, axis=-1, keepdims=True)
        outs.append(oh * lax.rsqrt(ms + EPS) * gain[:, h * LANES:(h + 1) * LANES]
                    * _silu(g[:, h * LANES:(h + 1) * LANES]))
    o_ref[...] = jnp.concatenate(outs, axis=1)

    @pl.when(step == nsteps - 1)
    def _():
        sfin_ref[0] = st_ref[...]


def _scan_consts(ts):
    r = np.arange(ts)
    tri = ((r[:, None] // SUB == r[None, :] // SUB) & (r[None, :] <= r[:, None])).astype(np.float32)
    row = np.arange(2 * LANES)[:, None] // LANES
    col = np.arange(LANES)[None, :] // HEAD
    bmask = (row == col).astype(np.float32)
    return jnp.asarray(tri, BF16), jnp.asarray(bmask, F32)


def linear_scan(q_arr, q_col, k_arr, k_col, v_arr, v_col, g_arr, g_col, la, gain, s0,
                *, row0, nb, t, ts, name):
    nsteps = t // ts
    blk0 = row0 // ts
    tri, bmask = _scan_consts(ts)
    src = lambda w, c: pl.BlockSpec((ts, w), lambda b, s, c=c: (blk0 + b * nsteps + s, c))
    loc = lambda w: pl.BlockSpec((ts, w), lambda b, s: (b * nsteps + s, 0))
    return pl.pallas_call(
        functools.partial(_scan_kernel, ts=ts, nsteps=nsteps),
        grid=(nb, nsteps),
        in_specs=[src(256, q_col), src(256, k_col), src(512, v_col), src(512, g_col), loc(256),
                  pl.BlockSpec((1, 512), lambda b, s: (0, 0)),
                  pl.BlockSpec((1, 2, 256, LANES), lambda b, s: (b, 0, 0, 0)),
                  pl.BlockSpec((ts, ts), lambda b, s: (0, 0)),
                  pl.BlockSpec((256, LANES), lambda b, s: (0, 0))],
        out_specs=[loc(512), pl.BlockSpec((1, 2, 256, LANES), lambda b, s: (b, 0, 0, 0))],
        out_shape=[jax.ShapeDtypeStruct((nb * t, 512), F32),
                   jax.ShapeDtypeStruct((nb, 2, 256, LANES), F32)],
        scratch_shapes=[pltpu.VMEM((2, 256, LANES), F32)],
        compiler_params=_cparams(("parallel", "arbitrary")),
        name=name,
    )(q_arr, k_arr, v_arr, g_arr, la, gain, s0, tri, bmask)


def _state_to_pairs(s):
    nb = s.shape[0]
    st = jnp.swapaxes(s, -1, -2).reshape(nb, 2, 2, LANES, HEAD)
    z = jnp.zeros_like(st[:, :, 0])
    top = jnp.concatenate([st[:, :, 0], z], axis=-1)
    bot = jnp.concatenate([z, st[:, :, 1]], axis=-1)
    return jnp.concatenate([top, bot], axis=-2)


def _pairs_to_state(sp):
    nb = sp.shape[0]
    h0 = sp[:, :, :LANES, :HEAD]
    h1 = sp[:, :, LANES:, HEAD:]
    st = jnp.stack([h0, h1], axis=2).reshape(nb, 4, LANES, HEAD)
    return jnp.swapaxes(st, -1, -2)


def _cumsum_kernel(x_ref, tri_ref, o_ref, carry_ref):
    @pl.when(pl.program_id(1) == 0)
    def _():
        carry_ref[...] = jnp.zeros_like(carry_ref)

    tri = tri_ref[...]
    a1, a2, a3 = _split3(x_ref[...])
    c = _dot(tri, a1) + _dot(tri, a2) + _dot(tri, a3) + carry_ref[0:1, :]
    o_ref[...] = c * LOG2E
    carry_ref[...] = jnp.broadcast_to(c[-1:, :], carry_ref.shape)


def row_cumsum(x, *, row0, nb, n, tb):
    r = np.arange(tb)
    tri = jnp.asarray((r[None, :] <= r[:, None]).astype(np.float32), BF16)
    nblk = n // tb
    blk0 = row0 // tb
    return pl.pallas_call(
        _cumsum_kernel,
        grid=(nb, nblk),
        in_specs=[pl.BlockSpec((tb, LANES), lambda b, j: (blk0 + b * nblk + j, 0)),
                  pl.BlockSpec((tb, tb), lambda b, j: (0, 0))],
        out_specs=pl.BlockSpec((tb, LANES), lambda b, j: (b * nblk + j, 0)),
        out_shape=jax.ShapeDtypeStruct((nb * n, LANES), F32),
        scratch_shapes=[pltpu.VMEM((SUBLANES, LANES), F32)],
        compiler_params=_cparams(("parallel", "arbitrary")),
        name="logf_cumsum",
    )(x, tri)


def _attn_update(s, vt, m_ref, l_ref, acc_ref, idx):
    m_prev = m_ref[idx]
    m_new = jnp.maximum(m_prev, jnp.max(s, axis=0, keepdims=True))
    alpha = jnp.exp2(m_prev - m_new)
    p = jnp.exp2(s - m_new)
    l_ref[idx] = alpha * l_ref[idx] + jnp.sum(p, axis=0, keepdims=True)
    acc_ref[idx] = alpha * acc_ref[idx] + _dot(vt, p.astype(BF16))
    m_ref[idx] = m_new


def _eye(n):
    r = lax.broadcasted_iota(I32, (n, n), 0)
    c = lax.broadcasted_iota(I32, (n, n), 1)
    return jnp.where(r == c, 1.0, 0.0).astype(BF16)


def _transpose_exact(x, eye):
    a1, a2, a3 = _split3(x)
    return _dot_nt(eye, a1) + _dot_nt(eye, a2) + _dot_nt(eye, a3)


def _write_heads(o_ref, l_ref, acc_ref, tq):
    eye = _eye(tq)
    for c in range(4):
        ot = jnp.concatenate([acc_ref[2 * c] / l_ref[2 * c], acc_ref[2 * c + 1] / l_ref[2 * c + 1]], axis=0)
        o_ref[:, c * LANES:(c + 1) * LANES] = _transpose_exact(ot, eye)


def _fox_kernel(qi_ref, kj_ref, q_ref, k_ref, v_ref, c_ref, o_ref, m_ref, l_ref, acc_ref,
                *, tq, tk, q_off):
    s_id = pl.program_id(1)
    qi = qi_ref[s_id]
    kj = kj_ref[s_id]
    j_last = (q_off + (qi + 1) * tq - 1) // tk

    @pl.when(kj == 0)
    def _():
        m_ref[...] = jnp.full_like(m_ref, NEG)
        l_ref[...] = jnp.zeros_like(l_ref)
        acc_ref[...] = jnp.zeros_like(acc_ref)

    def block(masked):
        hm = _half_masks()
        q, k, v, c = q_ref[...], k_ref[...], v_ref[...], c_ref[...]
        if masked:
            kpos = kj * tk + lax.broadcasted_iota(I32, (tk, tq), 0)
            qpos = q_off + qi * tq + lax.broadcasted_iota(I32, (tk, tq), 1)
            visible = kpos <= qpos
        scores = []
        for h in range(B_HEADS):
            l0 = (h // 2) * LANES
            qm = (q[:, l0:l0 + LANES] * hm[h % 2]).astype(BF16)
            s = _dot_nt(k[:, l0:l0 + LANES].astype(BF16), qm) - c[:, h:h + 1]
            scores.append(jnp.where(visible, s, NEG) if masked else s)
        for p in range(4):
            l0 = p * LANES
            vt = v[:, l0:l0 + LANES].T.astype(BF16)
            for hh in range(2):
                h = 2 * p + hh
                _attn_update(scores[h], vt[hh * HEAD:(hh + 1) * HEAD], m_ref, l_ref, acc_ref, h)

    @pl.when(kj < j_last)
    def _():
        block(False)

    @pl.when(kj == j_last)
    def _():
        block(True)
        _write_heads(o_ref, l_ref, acc_ref, tq)


def fox_attention(q_arr, k_arr, v_arr, v_col, c_arr, *, q_row0, nb, tq_total, tk_total, q_off,
                  tq, tk, name):
    nq = tq_total // tq
    steps = [(i, j) for i in range(nq) for j in range((q_off + (i + 1) * tq - 1) // tk + 1)]
    qi = jnp.asarray([s[0] for s in steps], I32)
    kj = jnp.asarray([s[1] for s in steps], I32)
    qb0 = q_row0 // tq
    nkb = tk_total // tk
    grid_spec = pltpu.PrefetchScalarGridSpec(
        num_scalar_prefetch=2,
        grid=(nb, len(steps)),
        in_specs=[
            pl.BlockSpec((tq, 512), lambda b, s, qi, kj: (qb0 + b * nq + qi[s], 0)),
            pl.BlockSpec((tk, 512), lambda b, s, qi, kj: (b * nkb + kj[s], 0)),
            pl.BlockSpec((tk, 512), lambda b, s, qi, kj: (b * nkb + kj[s], v_col)),
            pl.BlockSpec((tk, LANES), lambda b, s, qi, kj: (b * nkb + kj[s], 0)),
        ],
        out_specs=pl.BlockSpec((tq, 512), lambda b, s, qi, kj: (b * nq + qi[s], 0)),
        scratch_shapes=[pltpu.VMEM((8, 1, tq), F32), pltpu.VMEM((8, 1, tq), F32),
                        pltpu.VMEM((8, HEAD, tq), F32)],
    )
    return pl.pallas_call(
        functools.partial(_fox_kernel, tq=tq, tk=tk, q_off=q_off),
        grid_spec=grid_spec,
        out_shape=jax.ShapeDtypeStruct((nb * tq_total, 4 * LANES), F32),
        compiler_params=_cparams(("parallel", "arbitrary")),
        name=name,
    )(qi, kj, q_arr, k_arr, v_arr, c_arr)


def _sortable_key(x):
    b = lax.bitcast_convert_type(x + 0.0, I32)
    return jnp.where(b < 0, b ^ jnp.int32(0x7FFFFFFF), b)


def _dsa_kernel(iq_ref, misc_ref, qc_ref, kc_ref, vc_ref, ik_ref, o_ref,
                keys_ref, qm_ref, m_ref, l_ref, acc_ref, thr_ref, jcut_ref,
                *, tq, tk, n_keys, q_off, ksel):
    i = pl.program_id(1)
    hm = _half_masks()
    qcol = lax.broadcasted_iota(I32, (1, tq), 1)
    lim = jnp.minimum(((q_off + i * tq + qcol) // CHUNK + 1) * CHUNK, n_keys)
    lim_max = jnp.minimum(((q_off + (i + 1) * tq - 1) // CHUNK + 1) * CHUNK, n_keys)
    nkb = (lim_max + tk - 1) // tk
    eye = _eye(LANES)

    def key_pos(kb):
        return kb * tk + lax.broadcasted_iota(I32, (tk, tq), 0)

    iq = iq_ref[...]
    for hd in range(IDX_HEADS):
        c0 = (hd // 2) * LANES
        qm_ref[hd] = (iq[:, c0:c0 + LANES] * hm[hd % 2]).astype(BF16)
    iwt = _transpose_exact(misc_ref[...], eye) * IDX_SCALE

    def score_body(kb, carry):
        k0 = pl.multiple_of(kb * tk, tk)
        ikb = ik_ref[pl.ds(k0, tk), :]
        sc = jnp.zeros((tk, tq), F32)
        for hd in range(IDX_HEADS):
            a = _dot_nt(ikb, qm_ref[hd])
            sc = sc + jnp.maximum(a, 0.0) * iwt[MISC_IW + hd:MISC_IW + hd + 1, :]
        keys_ref[kb] = jnp.where(key_pos(kb) < lim, _sortable_key(sc), INT_MIN)
        return carry

    lax.fori_loop(0, nkb, score_body, 0)

    def count(pred):
        def body(kb, acc):
            hit = jnp.where(pred(keys_ref[kb], key_pos(kb)), 1.0, 0.0)
            return acc + jnp.sum(hit.reshape(tk // SUBLANES, SUBLANES, tq), axis=0)
        acc = lax.fori_loop(0, nkb, body, jnp.zeros((SUBLANES, tq), F32))
        return jnp.sum(acc, axis=0, keepdims=True)

    thr_ref[...] = jnp.full((1, tq), INT_MIN + 1, I32)
    jcut_ref[...] = jnp.full((1, tq), 2 ** 30, I32)
    kf = float(ksel)

    @pl.when(lim_max > ksel)
    def _():
        c0 = count(lambda blk, kpos: blk >= 0)
        res0 = jnp.where(c0 >= kf, 0, INT_MIN).astype(I32)

        def bit_body(t, res):
            cand = res + jnp.left_shift(jnp.int32(1), 30 - t)
            cnt = count(lambda blk, kpos: blk >= cand)
            return jnp.where(cnt >= kf, cand, res)

        res = lax.fori_loop(0, 31, bit_body, res0)
        thr = jnp.maximum(res, INT_MIN + 1)
        thr_ref[...] = thr
        need = kf - count(lambda blk, kpos: blk > thr)
        n_eq = count(lambda blk, kpos: blk == thr)
        split = n_eq > need

        @pl.when(jnp.max(jnp.where(split, 1.0, 0.0)) > 0.0)
        def _():
            def idx_body(t, lo):
                cand = lo + jnp.left_shift(jnp.int32(1), 14 - t)
                cnt = count(lambda blk, kpos: (blk == thr) & (kpos < cand))
                return jnp.where(cnt < need, cand, lo)

            lo = lax.fori_loop(0, 15, idx_body, jnp.zeros((1, tq), I32))
            jcut_ref[...] = jnp.where(split, lo, 2 ** 30)

    m_ref[...] = jnp.full_like(m_ref, NEG)
    l_ref[...] = jnp.zeros_like(l_ref)
    acc_ref[...] = jnp.zeros_like(acc_ref)
    qc = qc_ref[...]
    for hd in range(C_HEADS):
        c0 = (hd // 2) * LANES
        qm_ref[hd] = (qc[:, c0:c0 + LANES] * hm[hd % 2]).astype(BF16)
    thr = thr_ref[...]
    jcut = jcut_ref[...]

    def attn_body(kb, carry):
        k0 = pl.multiple_of(kb * tk, tk)
        kblk = kc_ref[pl.ds(k0, tk), :]
        vt = _dot_nt(eye, vc_ref[pl.ds(k0, tk), :]).astype(BF16)
        keys = keys_ref[kb]
        sel = (keys > thr) | ((keys == thr) & (key_pos(kb) <= jcut))
        scores = [jnp.where(sel, _dot_nt(kblk, qm_ref[hd]), NEG) for hd in range(C_HEADS)]
        for hd in range(C_HEADS):
            hh = hd % 2
            _attn_update(scores[hd], vt[hh * HEAD:(hh + 1) * HEAD], m_ref, l_ref, acc_ref, hd)
        return carry

    lax.fori_loop(0, nkb, attn_body, 0)
    _write_heads(o_ref, l_ref, acc_ref, tq)


def dsa_attention(iq_arr, misc_arr, misc_col, qc_arr, kc, vc, ik, *, q_row0, nb, tq_total,
                  n_keys, n_keys_pad, q_off, tq, tk, name):
    nq = tq_total // tq
    qb0 = q_row0 // tq
    ksel = min(TOPK_MAX, n_keys // 4)
    qspec = lambda w, c: pl.BlockSpec((tq, w), lambda b, i, c=c: (qb0 + b * nq + i, c))
    kspec = pl.BlockSpec((n_keys_pad, LANES), lambda b, i: (b, 0))
    return pl.pallas_call(
        functools.partial(_dsa_kernel, tq=tq, tk=tk, n_keys=n_keys, q_off=q_off, ksel=ksel),
        grid=(nb, nq),
        in_specs=[qspec(512, 0), qspec(LANES, misc_col), qspec(512, 0), kspec, kspec, kspec],
        out_specs=pl.BlockSpec((tq, 512), lambda b, i: (b * nq + i, 0)),
        out_shape=jax.ShapeDtypeStruct((nb * tq_total, 512), F32),
        scratch_shapes=[pltpu.VMEM((n_keys_pad // tk, tk, tq), I32),
                        pltpu.VMEM((8, tq, LANES), BF16),
                        pltpu.VMEM((8, 1, tq), F32), pltpu.VMEM((8, 1, tq), F32),
                        pltpu.VMEM((8, HEAD, tq), F32),
                        pltpu.VMEM((1, tq), I32), pltpu.VMEM((1, tq), I32)],
        compiler_params=_cparams(("parallel", "arbitrary")),
        name=name,
    )(iq_arr, misc_arr, qc_arr, kc, vc, ik)


def _rope_tables(pos):
    half = HEAD // 2
    inv = jnp.power(ROPE_THETA, -jnp.arange(half, dtype=F32) / half)
    ang = pos.astype(F32)[:, None] * inv[None, :]
    cos, sin = jnp.cos(ang), jnp.sin(ang)
    z = jnp.zeros_like(sin)
    return (jnp.tile(cos, (1, 4)), jnp.tile(jnp.concatenate([-sin, z], 1), (1, 2)),
            jnp.tile(jnp.concatenate([z, sin], 1), (1, 2)))


def _head_mean_matrix():
    r = np.arange(512)
    return jnp.asarray((r[:, None] // HEAD == r[None, :] // HEAD).astype(np.float32), BF16)


def _pick_tile(n, prefs):
    for t in prefs:
        if n % t == 0:
            return t
    raise ValueError(f"no tile in {prefs} divides {n}")


def _pad_keys(x, n_pad):
    nb, n, w = x.shape
    if n_pad > n:
        x = jnp.concatenate([x, jnp.zeros((nb, n_pad - n, w), x.dtype)], axis=1)
    return x.reshape(nb * n_pad, w)


def kernel(x_prompt, x_sample, state_a, cache_b_k, cache_b_v, cache_b_logf, cache_c_k, cache_c_v,
           cache_c_idx, state_d, norm_mix, norm_ffn, even_w_in, even_b_f, even_q_norm, even_k_norm,
           even_w_out, odd_w_in, odd_w_gate_up, odd_b_gate, odd_q_norm, odd_k_norm, odd_o_norm,
           odd_w_out, ffn_w1, ffn_w2):
    bp, tp = x_prompt.shape[:2]
    nb, ts = x_sample.shape[:2]
    past = cache_b_k.shape[2]
    depth = norm_mix.shape[0]
    rp, rs = bp * tp, nb * ts
    rows = rp + rs
    assert bp == 1 and tp % 128 == 0 and ts % SUB == 0 and rp % ts == 0 and past % CHUNK == 0

    y = jnp.concatenate([x_prompt.reshape(rp, D_MODEL), x_sample.reshape(rs, D_MODEL)], axis=0)
    pos = jnp.concatenate([jnp.arange(tp, dtype=I32),
                           jnp.tile(past + jnp.arange(ts, dtype=I32), nb)])
    cos, slo, shi = _rope_tables(pos)
    gmat = _head_mean_matrix()

    tm = _pick_tile(rows, (512, 256, 128, 64, 32))
    tm_prep = _pick_tile(rows, (256, 128, 64, 32))
    ts_p = _pick_tile(tp, (128, 64, 32, 16))
    tq_fox = _pick_tile(tp, (256, 128))
    tk_fox = _pick_tile(tp, (512, 256, 128))
    tq_dsa = _pick_tile(tp, (256, 128))
    tk_dsa = _pick_tile(tp, (512, 256, 128))
    nk_s = past + ts
    nk_s_pad = -(-nk_s // 384) * 384
    tb_p = _pick_tile(tp, (512, 256, 128))

    lg = jnp.log1p(-jnp.exp2(-5.0 - jnp.arange(A_HEADS, dtype=F32)))
    la_ret = jnp.repeat(lg, A_DK)[None, :]
    ones_gain = jnp.ones((1, 512), F32)
    zero_state = jnp.zeros((bp, 2, 256, LANES), F32)

    a_p, a_s, bk_p, bk_s, bv_p, bv_s, bf_p, bf_s = [], [], [], [], [], [], [], []
    ck_p, ck_s, cv_p, cv_s, ci_p, ci_s, d_p, d_s = [], [], [], [], [], [], [], []

    for l in range(depth):
        i = l // 2
        if l % 2 == 0:
            w_in = jnp.concatenate(
                [even_w_in[i], jnp.zeros((D_MODEL, EV_WIDTH - even_w_in.shape[2]), F32)], 1).astype(BF16)
            h = norm_matmul(y, norm_mix[l], w_in, tm=tm, tn=EV_WIDTH, name="even_in_proj")
            bf = jnp.concatenate([even_b_f[i], jnp.zeros((LANES - B_HEADS,), F32)])[None, :]
            qka, qbn, kbn, lf = prep_even(
                h, cos, slo, shi, jnp.tile(even_q_norm[i], B_HEADS)[None, :],
                jnp.tile(even_k_norm[i], B_HEADS)[None, :], bf, gmat, tm=tm_prep)
            oa_p, sa_p = linear_scan(qka, 0, qka, 1, h, EV_VA // 512, h, EV_GA // 512,
                                     jnp.broadcast_to(la_ret, (rp, 256)), ones_gain, zero_state,
                                     row0=0, nb=bp, t=tp, ts=ts_p, name="retention_prompt")
            oa_s, sa_s = linear_scan(qka, 0, qka, 1, h, EV_VA // 512, h, EV_GA // 512,
                                     jnp.broadcast_to(la_ret, (rs, 256)), ones_gain,
                                     _state_to_pairs(state_a[i]),
                                     row0=rp, nb=nb, t=ts, ts=ts, name="retention_decode")
            lf8 = lf[:, :B_HEADS]
            c_p = row_cumsum(lf, row0=0, nb=bp, n=tp, tb=tb_p)
            ob_p = fox_attention(qbn, kbn, h, EV_VB // 512, c_p,
                                 q_row0=0, nb=bp, tq_total=tp, tk_total=tp, q_off=0,
                                 tq=tq_fox, tk=tk_fox, name="fox_prompt")
            kb_new = kbn[rp:].reshape(nb, ts, 512)
            vb_new = h[rp:, EV_VB:EV_VB + 512].reshape(nb, ts, 512)
            lf_past = jnp.pad(cache_b_logf[i], ((0, 0), (0, 0), (0, LANES - B_HEADS)))
            lf_all = _pad_keys(jnp.concatenate([lf_past, lf[rp:].reshape(nb, ts, LANES)], 1), nk_s_pad)
            c_s = row_cumsum(lf_all, row0=0, nb=nb, n=nk_s_pad, tb=nk_s_pad // 3)
            k_all = _pad_keys(jnp.concatenate([cache_b_k[i].reshape(nb, past, 512), kb_new], 1), nk_s_pad)
            v_all = _pad_keys(jnp.concatenate([cache_b_v[i].reshape(nb, past, 512), vb_new], 1), nk_s_pad)
            ob_s = fox_attention(qbn, k_all, v_all, 0, c_s,
                                 q_row0=rp, nb=nb, tq_total=ts, tk_total=nk_s_pad, q_off=past,
                                 tq=ts, tk=nk_s_pad, name="fox_decode")
            w_out = even_w_out[i].astype(BF16)
            y = matmul_residual(y, [jnp.concatenate([oa_p, oa_s], 0), jnp.concatenate([ob_p, ob_s], 0)],
                                [w_out[:512], w_out[512:]], tm=tm, name="even_out_proj")
            a_p.append(_pairs_to_state(sa_p))
            a_s.append(_pairs_to_state(sa_s))
            bk_p.append(kbn[:rp].reshape(bp, tp, B_HEADS, B_DH))
            bk_s.append(kb_new.reshape(nb, ts, B_HEADS, B_DH))
            bv_p.append(h[:rp, EV_VB:EV_VB + 512].reshape(bp, tp, B_HEADS, B_DH))
            bv_s.append(vb_new.reshape(nb, ts, B_HEADS, B_DH))
            bf_p.append(lf8[:rp].reshape(bp, tp, B_HEADS))
            bf_s.append(lf8[rp:].reshape(nb, ts, B_HEADS))
        else:
            w = odd_w_in[i]
            offs = np.cumsum([0, 512, 128, 128, 512, 64, 8, 256, 256, 512, 512, 16])
            qc_w, kc_w, vc_w, iq_w, ik_w, iw_w, qd_w, kd_w, vd_w, gd_w, gr_w = [
                w[:, int(a):int(b)] for a, b in zip(offs[:-1], offs[1:])]
            qc_w = qc_w.reshape(D_MODEL, 2, 4, C_DH).transpose(0, 2, 1, 3).reshape(D_MODEL, 512)
            w_in = jnp.concatenate(
                [qc_w, iq_w, vd_w, gd_w, qd_w, kd_w, kc_w, vc_w, ik_w, iw_w, gr_w,
                 jnp.zeros((D_MODEL, OD_WIDTH - OD_MISC - 88), F32)], 1).astype(BF16)
            h = norm_matmul(y, norm_mix[l], w_in, tm=tm, tn=OD_WIDTH, name="odd_in_proj")
            wg = jnp.zeros((LANES, 256), F32).at[MISC_GR:MISC_GR + D_GATE_RANK].set(
                odd_w_gate_up[i]).astype(BF16)
            qcr, iqr, qkd, kcr, ikr, kcb, vcb, ikb, la = prep_odd(
                h, cos, slo, shi, jnp.tile(odd_q_norm[i], C_HEADS)[None, :],
                jnp.tile(odd_k_norm[i], C_KV_HEADS)[None, :], wg, odd_b_gate[i][None, :], gmat,
                tm=tm_prep)
            oc_p = dsa_attention(iqr, h, OD_MISC // LANES, qcr, kcb[:rp], vcb[:rp], ikb[:rp],
                                 q_row0=0, nb=bp, tq_total=tp, n_keys=tp, n_keys_pad=tp, q_off=0,
                                 tq=tq_dsa, tk=tk_dsa, name="dsa_prompt")

            def with_past(cache, new):
                return _pad_keys(jnp.concatenate(
                    [cache.reshape(nb, past, -1).astype(BF16), new[rp:].reshape(nb, ts, -1)], 1), nk_s_pad)

            ik_past = jnp.concatenate([cache_c_idx[i], cache_c_idx[i]], axis=-1)
            oc_s = dsa_attention(iqr, h, OD_MISC // LANES, qcr,
                                 with_past(cache_c_k[i], kcb), with_past(cache_c_v[i], vcb),
                                 with_past(ik_past, ikb),
                                 q_row0=rp, nb=nb, tq_total=ts, n_keys=nk_s, n_keys_pad=nk_s_pad,
                                 q_off=past, tq=ts, tk=384, name="dsa_decode")
            gain = jnp.tile(odd_o_norm[i], D_HEADS)[None, :]
            od_p, sd_p = linear_scan(qkd, 0, qkd, 1, h, OD_VD // 512, h, OD_GD // 512, la[:rp], gain,
                                     zero_state, row0=0, nb=bp, t=tp, ts=ts_p, name="gla_prompt")
            od_s, sd_s = linear_scan(qkd, 0, qkd, 1, h, OD_VD // 512, h, OD_GD // 512, la[rp:], gain,
                                     _state_to_pairs(state_d[i]), row0=rp, nb=nb, t=ts, ts=ts,
                                     name="gla_decode")
            w_out = odd_w_out[i]
            w_oc = w_out[:512].reshape(2, 4, C_DH, D_MODEL).transpose(1, 0, 2, 3).reshape(512, D_MODEL)
            y = matmul_residual(y, [jnp.concatenate([oc_p, oc_s], 0), jnp.concatenate([od_p, od_s], 0)],
                                [w_oc.astype(BF16), w_out[512:].astype(BF16)], tm=tm,
                                name="odd_out_proj")
            d_p.append(_pairs_to_state(sd_p))
            d_s.append(_pairs_to_state(sd_s))
            ck_p.append(kcr[:rp].reshape(bp, tp, C_KV_HEADS, C_DH))
            ck_s.append(kcr[rp:].reshape(nb, ts, C_KV_HEADS, C_DH))
            cv_p.append(h[:rp, OD_KVC + LANES:OD_KVC + 2 * LANES].reshape(bp, tp, C_KV_HEADS, C_DH))
            cv_s.append(h[rp:, OD_KVC + LANES:OD_KVC + 2 * LANES].reshape(nb, ts, C_KV_HEADS, C_DH))
            ci_p.append(ikr[:rp, :IDX_DIM].reshape(bp, tp, IDX_DIM))
            ci_s.append(ikr[rp:, :IDX_DIM].reshape(nb, ts, IDX_DIM))
        hid = norm_matmul(y, norm_ffn[l], ffn_w1[l].astype(BF16), tm=tm, tn=2048, relu2=True,
                          out_dtype=BF16, name="mlp_up")
        y = matmul_residual(y, [hid], [ffn_w2[l].astype(BF16)], tm=tm, name="mlp_down")

    return (y[:rp].reshape(bp, tp, D_MODEL), y[rp:].reshape(nb, ts, D_MODEL),
            jnp.stack(a_p), jnp.stack(a_s), jnp.stack(bk_p), jnp.stack(bk_s),
            jnp.stack(bv_p), jnp.stack(bv_s), jnp.stack(bf_p), jnp.stack(bf_s),
            jnp.stack(ck_p), jnp.stack(ck_s), jnp.stack(cv_p), jnp.stack(cv_s),
            jnp.stack(ci_p), jnp.stack(ci_s), jnp.stack(d_p), jnp.stack(d_s))
```

```python
import functools

import numpy as np
import jax
import jax.numpy as jnp
from jax import lax
from jax.experimental import pallas as pl
from jax.experimental.pallas import tpu as pltpu

F32 = jnp.float32
BF16 = jnp.bfloat16
I32 = jnp.int32

D_MODEL = 1024
CHUNK = 64
ROPE_THETA = 10000.0
EPS = 1e-6
A_HEADS, A_DK, A_DV = 4, 64, 128
B_HEADS, B_DH = 8, 64
C_HEADS, C_KV_HEADS, C_DH = 8, 2, 64
IDX_HEADS, IDX_DIM = 8, 64
IDX_SCALE = (IDX_HEADS * IDX_DIM) ** -0.5
TOPK_MAX = 256
D_HEADS, D_DK, D_DV = 4, 64, 128
D_GATE_RANK = 16
D_GATE_NORM = 16.0
D_FF = 4 * D_MODEL

LANES = 128
SUBLANES = 8
VMEM_LIMIT_BYTES = 56 * 1024 * 1024

HEAD = 64
SUB = 16
FOLD = 64
COUNT_FOLD = 32
ONES_ROWS = 16
NEG = -1e30
LOG2E = 1.4426950408889634
INT_MIN = -(2 ** 31)

EV_QK, EV_VA, EV_GA, EV_QB, EV_KB, EV_VB, EV_FB = 0, 512, 1024, 1536, 2048, 2560, 3072
EV_WIDTH = 3200
OD_QC, OD_IQ, OD_VD, OD_GD, OD_QKD, OD_KVC, OD_MISC = 0, 512, 1024, 1536, 2048, 2560, 2816
OD_WIDTH = 2944
MISC_IK, MISC_IW, MISC_GR = 0, 64, 72


def _cparams(sem):
    return pltpu.CompilerParams(dimension_semantics=sem, vmem_limit_bytes=VMEM_LIMIT_BYTES)


def _dot(a, b):
    return jnp.dot(a, b, preferred_element_type=F32)


def _dot_nt(a, b):
    return lax.dot_general(a, b, (((1,), (1,)), ((), ())), preferred_element_type=F32)


def _dot_tn(a, b):
    return lax.dot_general(a, b, (((0,), (0,)), ((), ())), preferred_element_type=F32)


def _split3(x):
    a1 = x.astype(BF16)
    r1 = x - a1.astype(F32)
    a2 = r1.astype(BF16)
    a3 = (r1 - a2.astype(F32)).astype(BF16)
    return a1, a2, a3


def _half_masks():
    lane = lax.broadcasted_iota(I32, (1, LANES), 1)
    lo = (lane < HEAD).astype(F32)
    return lo, 1.0 - lo


def _log_sigmoid(x):
    return jnp.minimum(x, 0.0) - jnp.log1p(jnp.exp(-jnp.abs(x)))


def _silu(x):
    return x / (1.0 + jnp.exp(-x))


def _norm_mm_kernel(x_ref, g_ref, w_ref, o_ref, *, relu2):
    x = x_ref[...]
    ms = jnp.mean(x * x, axis=-1, keepdims=True)
    xn = (x * lax.rsqrt(ms + EPS) * g_ref[...]).astype(BF16)
    y = _dot(xn, w_ref[...])
    if relu2:
        y = jnp.maximum(y, 0.0)
        y = y * y
    o_ref[...] = y.astype(o_ref.dtype)


def norm_matmul(x, g, w, *, tm, tn, relu2=False, out_dtype=F32, name):
    rows, k = x.shape
    n = w.shape[1]
    return pl.pallas_call(
        functools.partial(_norm_mm_kernel, relu2=relu2),
        grid=(rows // tm, n // tn),
        in_specs=[pl.BlockSpec((tm, k), lambda i, j: (i, 0)),
                  pl.BlockSpec((1, k), lambda i, j: (0, 0)),
                  pl.BlockSpec((k, tn), lambda i, j: (0, j))],
        out_specs=pl.BlockSpec((tm, tn), lambda i, j: (i, j)),
        out_shape=jax.ShapeDtypeStruct((rows, n), out_dtype),
        compiler_params=_cparams(("parallel", "parallel")),
        name=name,
    )(x, g.reshape(1, k), w)


def _mm_res_kernel(*refs, n_in):
    res_ref = refs[0]
    o_ref = refs[-1]
    acc = res_ref[...]
    for a_ref, w_ref in zip(refs[1:1 + n_in], refs[1 + n_in:1 + 2 * n_in]):
        acc = acc + _dot(a_ref[...].astype(BF16), w_ref[...])
    o_ref[...] = acc


def matmul_residual(res, a_list, w_list, *, tm, name):
    rows, n = res.shape
    n_in = len(a_list)
    in_specs = [pl.BlockSpec((tm, n), lambda i: (i, 0))]
    in_specs += [pl.BlockSpec((tm, a.shape[1]), lambda i: (i, 0)) for a in a_list]
    in_specs += [pl.BlockSpec(w.shape, lambda i: (0, 0)) for w in w_list]
    return pl.pallas_call(
        functools.partial(_mm_res_kernel, n_in=n_in),
        grid=(rows // tm,),
        in_specs=in_specs,
        out_specs=pl.BlockSpec((tm, n), lambda i: (i, 0)),
        out_shape=jax.ShapeDtypeStruct((rows, n), F32),
        compiler_params=_cparams(("parallel",)),
        name=name,
    )(res, *a_list, *w_list)


def _tile_lanes(t, width):
    n = width // LANES
    return t if n == 1 else jnp.concatenate([t] * n, axis=1)


def _rope(x, cos, sin_lo, sin_hi):
    w = x.shape[1]
    up = pltpu.roll(x, w - HEAD // 2, 1)
    dn = pltpu.roll(x, HEAD // 2, 1)
    return x * _tile_lanes(cos, w) + up * _tile_lanes(sin_lo, w) + dn * _tile_lanes(sin_hi, w)


def _head_rms(x, gmat):
    s = x * x
    hi = s.astype(BF16)
    lo = (s - hi.astype(F32)).astype(BF16)
    ms = (_dot(hi, gmat) + _dot(lo, gmat)) * (1.0 / HEAD)
    return x * lax.rsqrt(ms + EPS)


def _prep_even_kernel(qk_ref, qb_ref, kb_ref, fb_ref, cos_ref, slo_ref, shi_ref,
                      qg_ref, kg_ref, bf_ref, gmat_ref,
                      qka_ref, qbn_ref, kbn_ref, lf_ref):
    cos, slo, shi = cos_ref[...], slo_ref[...], shi_ref[...]
    lane = lax.broadcasted_iota(I32, (1, 4 * LANES), 1)
    kscale = jnp.where(lane < 2 * LANES, 1.0, A_DK ** -0.5)
    qka_ref[...] = _rope(qk_ref[...], cos, slo, shi) * kscale
    gmat = gmat_ref[...]
    qbn_ref[...] = _head_rms(qb_ref[...], gmat) * (qg_ref[...] * (B_DH ** -0.5 * LOG2E))
    kbn_ref[...] = _head_rms(kb_ref[...], gmat) * kg_ref[...]
    lf_ref[...] = _log_sigmoid(fb_ref[...] + bf_ref[...])


def prep_even(h, cos, slo, shi, qg, kg, bf, gmat, *, tm):
    rows = h.shape[0]
    blk = lambda w, c: pl.BlockSpec((tm, w), lambda i, c=c: (i, c))
    row = lambda w: pl.BlockSpec((1, w), lambda i: (0, 0))
    return pl.pallas_call(
        _prep_even_kernel,
        grid=(rows // tm,),
        in_specs=[blk(512, EV_QK // 512), blk(512, EV_QB // 512), blk(512, EV_KB // 512),
                  blk(LANES, EV_FB // LANES), blk(LANES, 0), blk(LANES, 0), blk(LANES, 0),
                  row(512), row(512), row(LANES),
                  pl.BlockSpec((512, 512), lambda i: (0, 0))],
        out_specs=[blk(512, 0), blk(512, 0), blk(512, 0), blk(LANES, 0)],
        out_shape=[jax.ShapeDtypeStruct((rows, 512), F32)] * 3
        + [jax.ShapeDtypeStruct((rows, LANES), F32)],
        compiler_params=_cparams(("parallel",)),
        name="prep_even",
    )(h, h, h, h, cos, slo, shi, qg, kg, bf, gmat)


def _prep_odd_kernel(qc_ref, iq_ref, qkd_ref, kvc_ref, misc_ref, cos_ref, slo_ref, shi_ref,
                     qg_ref, kg_ref, wg_ref, bg_ref, gmat_ref,
                     qcr_ref, iqr_ref, qkdo_ref, kcr_ref, ikr_ref, kcb_ref, vcb_ref, ikb_ref,
                     la_ref):
    cos, slo, shi = cos_ref[...], slo_ref[...], shi_ref[...]
    gmat = gmat_ref[...]
    qc = _head_rms(qc_ref[...], gmat) * (qg_ref[...] * (C_DH ** -0.5 * LOG2E))
    qcr_ref[...] = _rope(qc, cos, slo, shi)
    iqr_ref[...] = _rope(iq_ref[...], cos, slo, shi)
    lane = lax.broadcasted_iota(I32, (1, 4 * LANES), 1)
    qkdo_ref[...] = qkd_ref[...] * jnp.where(lane < 2 * LANES, D_DK ** -0.5, 1.0)
    kvc = kvc_ref[...]
    kc = _head_rms(kvc[:, :LANES], gmat[:LANES, :LANES]) * kg_ref[...]
    kcr = _rope(kc, cos, slo, shi)
    kcr_ref[...] = kcr
    kcb_ref[...] = kcr.astype(BF16)
    vcb_ref[...] = kvc[:, LANES:].astype(BF16)
    misc = misc_ref[...]
    ikr = _rope(misc, cos, slo, shi)
    lane1 = lax.broadcasted_iota(I32, (1, LANES), 1)
    ik2 = jnp.where(lane1 < HEAD, ikr, pltpu.roll(ikr, HEAD, 1))
    ikr_ref[...] = ik2
    ikb_ref[...] = ik2.astype(BF16)
    z = _dot(misc.astype(BF16), wg_ref[...]) + bg_ref[...]
    la_ref[...] = _log_sigmoid(z) * (1.0 / D_GATE_NORM)


def prep_odd(h, cos, slo, shi, qg, kg, wg, bg, gmat, *, tm):
    rows = h.shape[0]
    blk = lambda w, c: pl.BlockSpec((tm, w), lambda i, c=c: (i, c))
    row = lambda w: pl.BlockSpec((1, w), lambda i: (0, 0))
    return pl.pallas_call(
        _prep_odd_kernel,
        grid=(rows // tm,),
        in_specs=[blk(512, OD_QC // 512), blk(512, OD_IQ // 512), blk(512, OD_QKD // 512),
                  blk(256, OD_KVC // 256), blk(LANES, OD_MISC // LANES),
                  blk(LANES, 0), blk(LANES, 0), blk(LANES, 0),
                  row(512), row(LANES),
                  pl.BlockSpec((LANES, 256), lambda i: (0, 0)), row(256),
                  pl.BlockSpec((512, 512), lambda i: (0, 0))],
        out_specs=[blk(512, 0), blk(512, 0), blk(512, 0), blk(LANES, 0), blk(LANES, 0),
                   blk(LANES, 0), blk(LANES, 0), blk(LANES, 0), blk(256, 0)],
        out_shape=[jax.ShapeDtypeStruct((rows, 512), F32)] * 3
        + [jax.ShapeDtypeStruct((rows, LANES), F32)] * 2
        + [jax.ShapeDtypeStruct((rows, LANES), BF16)] * 3
        + [jax.ShapeDtypeStruct((rows, 256), F32)],
        compiler_params=_cparams(("parallel",)),
        name="prep_odd",
    )(h, h, h, h, h, cos, slo, shi, qg, kg, wg, bg, gmat)


def _scan_kernel(q_ref, k_ref, v_ref, g_ref, la_ref, gain_ref, s0_ref, tri_ref, bmask_ref,
                 o_ref, sfin_ref, st_ref, *, ts, nsteps):
    step = pl.program_id(1)

    @pl.when(step == 0)
    def _():
        st_ref[...] = s0_ref[0]

    q, k, v = q_ref[...], k_ref[...], v_ref[...]
    tri = tri_ref[...]
    a1, a2, a3 = _split3(la_ref[...])
    cum = _dot(tri, a1) + _dot(tri, a2) + _dot(tri, a3)
    qt = q * jnp.exp(cum)
    kt = k * jnp.exp(-cum)
    hm = _half_masks()
    bmask = bmask_ref[...]
    ri = lax.broadcasted_iota(I32, (SUB, SUB), 0)
    ci = lax.broadcasted_iota(I32, (SUB, SUB), 1)
    causal = ci <= ri
    o_rows = []
    for u in range(ts // SUB):
        r0 = u * SUB
        last = cum[r0 + SUB - 1:r0 + SUB, :]
        k2 = k[r0:r0 + SUB, :] * jnp.exp(last - cum[r0:r0 + SUB, :])
        elast = jnp.exp(last)
        o_pairs = []
        for p in range(2):
            l0 = p * LANES
            qp = qt[r0:r0 + SUB, l0:l0 + LANES]
            kp = kt[r0:r0 + SUB, l0:l0 + LANES].astype(BF16)
            st = st_ref[p]
            inter = _dot_nt(qp.astype(BF16), st.astype(BF16))
            intra = []
            for hh in range(2):
                att = _dot_nt((qp * hm[hh]).astype(BF16), kp)
                att = jnp.where(causal, att, 0.0)
                c0 = (2 * p + hh) * LANES
                intra.append(_dot(att.astype(BF16), v[r0:r0 + SUB, c0:c0 + LANES].astype(BF16)))
            o_pairs.append(inter + jnp.concatenate(intra, axis=1))
            upd = _dot_tn(v[r0:r0 + SUB, 2 * l0:2 * l0 + 2 * LANES].astype(BF16),
                          k2[:, l0:l0 + LANES].astype(BF16))
            st_ref[p] = st * elast[:, l0:l0 + LANES] + upd * bmask
        o_rows.append(jnp.concatenate(o_pairs, axis=1))
    o = jnp.concatenate(o_rows, axis=0)
    g = g_ref[...]
    gain = gain_ref[...]
    outs = []
    for h in range(4):
        oh = o[:, h * LANES:(h + 1) * LANES]
        ms = jnp.mean(oh * oh, axis=-1, keepdims=True)
        outs.append(oh * lax.rsqrt(ms + EPS) * gain[:, h * LANES:(h + 1) * LANES]
                    * _silu(g[:, h * LANES:(h + 1) * LANES]))
    o_ref[...] = jnp.concatenate(outs, axis=1)

    @pl.when(step == nsteps - 1)
    def _():
        sfin_ref[0] = st_ref[...]


def _scan_consts(ts):
    r = np.arange(ts)
    tri = ((r[:, None] // SUB == r[None, :] // SUB) & (r[None, :] <= r[:, None])).astype(np.float32)
    row = np.arange(2 * LANES)[:, None] // LANES
    col = np.arange(LANES)[None, :] // HEAD
    bmask = (row == col).astype(np.float32)
    return jnp.asarray(tri, BF16), jnp.asarray(bmask, F32)


def linear_scan(q_arr, q_col, k_arr, k_col, v_arr, v_col, g_arr, g_col, la, gain, s0,
                *, row0, nb, t, ts, name):
    nsteps = t // ts
    blk0 = row0 // ts
    tri, bmask = _scan_consts(ts)
    src = lambda w, c: pl.BlockSpec((ts, w), lambda b, s, c=c: (blk0 + b * nsteps + s, c))
    loc = lambda w: pl.BlockSpec((ts, w), lambda b, s: (b * nsteps + s, 0))
    return pl.pallas_call(
        functools.partial(_scan_kernel, ts=ts, nsteps=nsteps),
        grid=(nb, nsteps),
        in_specs=[src(256, q_col), src(256, k_col), src(512, v_col), src(512, g_col), loc(256),
                  pl.BlockSpec((1, 512), lambda b, s: (0, 0)),
                  pl.BlockSpec((1, 2, 256, LANES), lambda b, s: (b, 0, 0, 0)),
                  pl.BlockSpec((ts, ts), lambda b, s: (0, 0)),
                  pl.BlockSpec((256, LANES), lambda b, s: (0, 0))],
        out_specs=[loc(512), pl.BlockSpec((1, 2, 256, LANES), lambda b, s: (b, 0, 0, 0))],
        out_shape=[jax.ShapeDtypeStruct((nb * t, 512), F32),
                   jax.ShapeDtypeStruct((nb, 2, 256, LANES), F32)],
        scratch_shapes=[pltpu.VMEM((2, 256, LANES), F32)],
        compiler_params=_cparams(("parallel", "arbitrary")),
        name=name,
    )(q_arr, k_arr, v_arr, g_arr, la, gain, s0, tri, bmask)


def _state_to_pairs(s):
    nb = s.shape[0]
    st = jnp.swapaxes(s, -1, -2).reshape(nb, 2, 2, LANES, HEAD)
    z = jnp.zeros_like(st[:, :, 0])
    top = jnp.concatenate([st[:, :, 0], z], axis=-1)
    bot = jnp.concatenate([z, st[:, :, 1]], axis=-1)
    return jnp.concatenate([top, bot], axis=-2)


def _pairs_to_state(sp):
    nb = sp.shape[0]
    h0 = sp[:, :, :LANES, :HEAD]
    h1 = sp[:, :, LANES:, HEAD:]
    st = jnp.stack([h0, h1], axis=2).reshape(nb, 4, LANES, HEAD)
    return jnp.swapaxes(st, -1, -2)


def _cumsum_kernel(x_ref, tri_ref, o_ref, carry_ref):
    @pl.when(pl.program_id(1) == 0)
    def _():
        carry_ref[...] = jnp.zeros_like(carry_ref)

    tri = tri_ref[...]
    a1, a2, a3 = _split3(x_ref[...])
    c = _dot(tri, a1) + _dot(tri, a2) + _dot(tri, a3) + carry_ref[0:1, :]
    o_ref[...] = c * LOG2E
    carry_ref[...] = jnp.broadcast_to(c[-1:, :], carry_ref.shape)


def row_cumsum(x, *, row0, nb, n, tb):
    r = np.arange(tb)
    tri = jnp.asarray((r[None, :] <= r[:, None]).astype(np.float32), BF16)
    nblk = n // tb
    blk0 = row0 // tb
    return pl.pallas_call(
        _cumsum_kernel,
        grid=(nb, nblk),
        in_specs=[pl.BlockSpec((tb, LANES), lambda b, j: (blk0 + b * nblk + j, 0)),
                  pl.BlockSpec((tb, tb), lambda b, j: (0, 0))],
        out_specs=pl.BlockSpec((tb, LANES), lambda b, j: (b * nblk + j, 0)),
        out_shape=jax.ShapeDtypeStruct((nb * n, LANES), F32),
        scratch_shapes=[pltpu.VMEM((SUBLANES, LANES), F32)],
        compiler_params=_cparams(("parallel", "arbitrary")),
        name="logf_cumsum",
    )(x, tri)


def _col_reduce(x, op):
    tk, tq = x.shape
    if tk > FOLD and tk % FOLD == 0:
        x = op(x.reshape(tk // FOLD, FOLD, tq), axis=0)
    return op(x, axis=0, keepdims=True)


def _values_with_ones(vt_head):
    ones = jnp.ones((ONES_ROWS, vt_head.shape[1]), F32)
    return jnp.concatenate([vt_head, ones], axis=0).astype(BF16)


def _attn_update(s, smax, vt_ext, m_ref, acc_ref, idx):
    m_prev = m_ref[idx]
    m_new = jnp.maximum(m_prev, smax)
    alpha = jnp.exp2(m_prev - m_new)
    p = jnp.exp2(s - m_new)
    acc_ref[idx] = alpha * acc_ref[idx] + _dot(vt_ext, p.astype(BF16))
    m_ref[idx] = m_new


def _eye(n):
    r = lax.broadcasted_iota(I32, (n, n), 0)
    c = lax.broadcasted_iota(I32, (n, n), 1)
    return jnp.where(r == c, 1.0, 0.0).astype(BF16)


def _transpose_exact(x, eye):
    a1, a2, a3 = _split3(x)
    return _dot_nt(eye, a1) + _dot_nt(eye, a2) + _dot_nt(eye, a3)


def _write_heads(o_ref, acc_ref, tq):
    eye = _eye(tq)
    for c in range(4):
        a0, a1 = acc_ref[2 * c], acc_ref[2 * c + 1]
        ot = jnp.concatenate([a0[:HEAD] / a0[HEAD:HEAD + 1], a1[:HEAD] / a1[HEAD:HEAD + 1]], axis=0)
        o_ref[:, c * LANES:(c + 1) * LANES] = _transpose_exact(ot, eye)


def _fox_kernel(qi_ref, kj_ref, q_ref, k_ref, v_ref, c_ref, o_ref, m_ref, acc_ref,
                *, tq, tk, q_off):
    s_id = pl.program_id(1)
    qi = qi_ref[s_id]
    kj = kj_ref[s_id]
    j_last = (q_off + (qi + 1) * tq - 1) // tk

    @pl.when(kj == 0)
    def _():
        m_ref[...] = jnp.full_like(m_ref, NEG)
        acc_ref[...] = jnp.zeros_like(acc_ref)

    def block(masked):
        hm = _half_masks()
        q, k, v, c = q_ref[...], k_ref[...], v_ref[...], c_ref[...]
        if masked:
            kpos = kj * tk + lax.broadcasted_iota(I32, (tk, tq), 0)
            qpos = q_off + qi * tq + lax.broadcasted_iota(I32, (tk, tq), 1)
            visible = kpos <= qpos
        scores, maxima = [], []
        for h in range(B_HEADS):
            l0 = (h // 2) * LANES
            qm = (q[:, l0:l0 + LANES] * hm[h % 2]).astype(BF16)
            s = _dot_nt(k[:, l0:l0 + LANES].astype(BF16), qm) - c[:, h:h + 1]
            if masked:
                s = jnp.where(visible, s, NEG)
            scores.append(s)
            maxima.append(_col_reduce(s, jnp.max))
        for p in range(4):
            l0 = p * LANES
            vt = v[:, l0:l0 + LANES].T
            for hh in range(2):
                h = 2 * p + hh
                _attn_update(scores[h], maxima[h], _values_with_ones(vt[hh * HEAD:(hh + 1) * HEAD]),
                             m_ref, acc_ref, h)

    @pl.when(kj < j_last)
    def _():
        block(False)

    @pl.when(kj == j_last)
    def _():
        block(True)
        _write_heads(o_ref, acc_ref, tq)


def fox_attention(q_arr, k_arr, v_arr, v_col, c_arr, *, q_row0, nb, tq_total, tk_total, q_off,
                  tq, tk, name):
    nq = tq_total // tq
    steps = [(i, j) for i in range(nq) for j in range((q_off + (i + 1) * tq - 1) // tk + 1)]
    qi = jnp.asarray([s[0] for s in steps], I32)
    kj = jnp.asarray([s[1] for s in steps], I32)
    qb0 = q_row0 // tq
    nkb = tk_total // tk
    grid_spec = pltpu.PrefetchScalarGridSpec(
        num_scalar_prefetch=2,
        grid=(nb, len(steps)),
        in_specs=[
            pl.BlockSpec((tq, 512), lambda b, s, qi, kj: (qb0 + b * nq + qi[s], 0)),
            pl.BlockSpec((tk, 512), lambda b, s, qi, kj: (b * nkb + kj[s], 0)),
            pl.BlockSpec((tk, 512), lambda b, s, qi, kj: (b * nkb + kj[s], v_col)),
            pl.BlockSpec((tk, LANES), lambda b, s, qi, kj: (b * nkb + kj[s], 0)),
        ],
        out_specs=pl.BlockSpec((tq, 512), lambda b, s, qi, kj: (b * nq + qi[s], 0)),
        scratch_shapes=[pltpu.VMEM((8, 1, tq), F32), pltpu.VMEM((8, HEAD + ONES_ROWS, tq), F32)],
    )
    return pl.pallas_call(
        functools.partial(_fox_kernel, tq=tq, tk=tk, q_off=q_off),
        grid_spec=grid_spec,
        out_shape=jax.ShapeDtypeStruct((nb * tq_total, 4 * LANES), F32),
        compiler_params=_cparams(("parallel", "arbitrary")),
        name=name,
    )(qi, kj, q_arr, k_arr, v_arr, c_arr)


def _sortable_key(x):
    b = lax.bitcast_convert_type(x + 0.0, I32)
    return jnp.where(b < 0, b ^ jnp.int32(0x7FFFFFFF), b)


def _dsa_kernel(iq_ref, misc_ref, qc_ref, kc_ref, vc_ref, ik_ref, o_ref,
                keys_ref, qm_ref, m_ref, acc_ref, thr_ref, jcut_ref,
                *, tq, tk, n_keys, q_off, ksel):
    i = pl.program_id(1)
    hm = _half_masks()
    qcol = lax.broadcasted_iota(I32, (1, tq), 1)
    lim = jnp.minimum(((q_off + i * tq + qcol) // CHUNK + 1) * CHUNK, n_keys)
    lim_max = jnp.minimum(((q_off + (i + 1) * tq - 1) // CHUNK + 1) * CHUNK, n_keys)
    nkb = (lim_max + tk - 1) // tk
    eye = _eye(LANES)

    def key_pos(kb):
        return kb * tk + lax.broadcasted_iota(I32, (tk, tq), 0)

    iq = iq_ref[...]
    for hd in range(IDX_HEADS):
        c0 = (hd // 2) * LANES
        qm_ref[hd] = (iq[:, c0:c0 + LANES] * hm[hd % 2]).astype(BF16)
    iwt = _transpose_exact(misc_ref[...], eye) * IDX_SCALE

    def score_body(kb, carry):
        k0 = pl.multiple_of(kb * tk, tk)
        ikb = ik_ref[pl.ds(k0, tk), :]
        sc = jnp.zeros((tk, tq), F32)
        for hd in range(IDX_HEADS):
            a = _dot_nt(ikb, qm_ref[hd])
            sc = sc + jnp.maximum(a, 0.0) * iwt[MISC_IW + hd:MISC_IW + hd + 1, :]
        keys_ref[kb] = jnp.where(key_pos(kb) < lim, _sortable_key(sc), INT_MIN)
        return carry

    lax.fori_loop(0, nkb, score_body, 0)

    def count(pred):
        def body(kb, acc):
            hit = jnp.where(pred(keys_ref[kb], key_pos(kb)), 1.0, 0.0)
            return acc + jnp.sum(hit.reshape(tk // COUNT_FOLD, COUNT_FOLD, tq), axis=0)
        acc = lax.fori_loop(0, nkb, body, jnp.zeros((COUNT_FOLD, tq), F32))
        return jnp.sum(acc, axis=0, keepdims=True)

    def any_row(flag):
        return jnp.max(jnp.where(flag, 1.0, 0.0)) > 0.0

    thr_ref[...] = jnp.full((1, tq), INT_MIN + 1, I32)
    jcut_ref[...] = jnp.full((1, tq), 2 ** 30, I32)
    kf = float(ksel)

    @pl.when(lim_max > ksel)
    def _():
        c0 = count(lambda blk, kpos: blk >= 0)
        res0 = jnp.where(c0 >= kf, 0, INT_MIN).astype(I32)
        cres0 = jnp.where(c0 >= kf, c0, 2.0 * n_keys)
        few = lim <= ksel

        def unsettled(cres):
            return jnp.logical_and(cres != kf, jnp.logical_not(few))

        def bit_cond(state):
            t, _, _, go = state
            return jnp.logical_and(t < 31, go)

        def bit_body(state):
            t, res, cres, _ = state
            cand = res + jnp.left_shift(jnp.int32(1), 30 - t)
            cnt = count(lambda blk, kpos: blk >= cand)
            take = cnt >= kf
            res = jnp.where(take, cand, res)
            cres = jnp.where(take, cnt, cres)
            return t + 1, res, cres, any_row(unsettled(cres))

        _, res, cres, open_rows = lax.while_loop(
            bit_cond, bit_body, (jnp.int32(0), res0, cres0, any_row(unsettled(cres0))))
        thr = jnp.maximum(res, INT_MIN + 1)
        thr_ref[...] = thr

        @pl.when(open_rows)
        def _():
            need = kf - count(lambda blk, kpos: blk > thr)
            n_eq = count(lambda blk, kpos: blk == thr)
            split = n_eq > need

            @pl.when(any_row(split))
            def _():
                def idx_body(t, lo):
                    cand = lo + jnp.left_shift(jnp.int32(1), 14 - t)
                    cnt = count(lambda blk, kpos: (blk == thr) & (kpos < cand))
                    return jnp.where(cnt < need, cand, lo)

                lo = lax.fori_loop(0, 15, idx_body, jnp.zeros((1, tq), I32))
                jcut_ref[...] = jnp.where(split, lo, 2 ** 30)

    m_ref[...] = jnp.full_like(m_ref, NEG)
    acc_ref[...] = jnp.zeros_like(acc_ref)
    qc = qc_ref[...]
    for hd in range(C_HEADS):
        c0 = (hd // 2) * LANES
        qm_ref[hd] = (qc[:, c0:c0 + LANES] * hm[hd % 2]).astype(BF16)
    thr = thr_ref[...]
    jcut = jcut_ref[...]

    def attn_body(kb, carry):
        k0 = pl.multiple_of(kb * tk, tk)
        kblk = kc_ref[pl.ds(k0, tk), :]
        vt = _dot_nt(eye, vc_ref[pl.ds(k0, tk), :])
        vts = [_values_with_ones(vt[hh * HEAD:(hh + 1) * HEAD]) for hh in range(2)]
        keys = keys_ref[kb]
        sel = (keys > thr) | ((keys == thr) & (key_pos(kb) <= jcut))
        scores = [jnp.where(sel, _dot_nt(kblk, qm_ref[hd]), NEG) for hd in range(C_HEADS)]
        maxima = [_col_reduce(s, jnp.max) for s in scores]
        for hd in range(C_HEADS):
            _attn_update(scores[hd], maxima[hd], vts[hd % 2], m_ref, acc_ref, hd)
        return carry

    lax.fori_loop(0, nkb, attn_body, 0)
    _write_heads(o_ref, acc_ref, tq)


def dsa_attention(iq_arr, misc_arr, misc_col, qc_arr, kc, vc, ik, *, q_row0, nb, tq_total,
                  n_keys, n_keys_pad, q_off, tq, tk, name):
    nq = tq_total // tq
    qb0 = q_row0 // tq
    ksel = min(TOPK_MAX, n_keys // 4)
    qspec = lambda w, c: pl.BlockSpec((tq, w), lambda b, i, c=c: (qb0 + b * nq + i, c))
    kspec = pl.BlockSpec((n_keys_pad, LANES), lambda b, i: (b, 0))
    return pl.pallas_call(
        functools.partial(_dsa_kernel, tq=tq, tk=tk, n_keys=n_keys, q_off=q_off, ksel=ksel),
        grid=(nb, nq),
        in_specs=[qspec(512, 0), qspec(LANES, misc_col), qspec(512, 0), kspec, kspec, kspec],
        out_specs=pl.BlockSpec((tq, 512), lambda b, i: (b * nq + i, 0)),
        out_shape=jax.ShapeDtypeStruct((nb * tq_total, 512), F32),
        scratch_shapes=[pltpu.VMEM((n_keys_pad // tk, tk, tq), I32),
                        pltpu.VMEM((8, tq, LANES), BF16),
                        pltpu.VMEM((8, 1, tq), F32),
                        pltpu.VMEM((8, HEAD + ONES_ROWS, tq), F32),
                        pltpu.VMEM((1, tq), I32), pltpu.VMEM((1, tq), I32)],
        compiler_params=_cparams(("parallel", "arbitrary")),
        name=name,
    )(iq_arr, misc_arr, qc_arr, kc, vc, ik)


def _rope_tables(pos):
    half = HEAD // 2
    inv = jnp.power(ROPE_THETA, -jnp.arange(half, dtype=F32) / half)
    ang = pos.astype(F32)[:, None] * inv[None, :]
    cos, sin = jnp.cos(ang), jnp.sin(ang)
    z = jnp.zeros_like(sin)
    return (jnp.tile(cos, (1, 4)), jnp.tile(jnp.concatenate([-sin, z], 1), (1, 2)),
            jnp.tile(jnp.concatenate([z, sin], 1), (1, 2)))


def _head_mean_matrix():
    r = np.arange(512)
    return jnp.asarray((r[:, None] // HEAD == r[None, :] // HEAD).astype(np.float32), BF16)


def _pick_tile(n, prefs):
    for t in prefs:
        if n % t == 0:
            return t
    raise ValueError(f"no tile in {prefs} divides {n}")


def _pad_keys(x, n_pad):
    nb, n, w = x.shape
    if n_pad > n:
        x = jnp.concatenate([x, jnp.zeros((nb, n_pad - n, w), x.dtype)], axis=1)
    return x.reshape(nb * n_pad, w)


def kernel(x_prompt, x_sample, state_a, cache_b_k, cache_b_v, cache_b_logf, cache_c_k, cache_c_v,
           cache_c_idx, state_d, norm_mix, norm_ffn, even_w_in, even_b_f, even_q_norm, even_k_norm,
           even_w_out, odd_w_in, odd_w_gate_up, odd_b_gate, odd_q_norm, odd_k_norm, odd_o_norm,
           odd_w_out, ffn_w1, ffn_w2):
    bp, tp = x_prompt.shape[:2]
    nb, ts = x_sample.shape[:2]
    past = cache_b_k.shape[2]
    depth = norm_mix.shape[0]
    rp, rs = bp * tp, nb * ts
    rows = rp + rs
    assert bp == 1 and tp % 128 == 0 and ts % SUB == 0 and rp % ts == 0 and past % CHUNK == 0

    y = jnp.concatenate([x_prompt.reshape(rp, D_MODEL), x_sample.reshape(rs, D_MODEL)], axis=0)
    pos = jnp.concatenate([jnp.arange(tp, dtype=I32),
                           jnp.tile(past + jnp.arange(ts, dtype=I32), nb)])
    cos, slo, shi = _rope_tables(pos)
    gmat = _head_mean_matrix()

    tm = _pick_tile(rows, (512, 256, 128, 64, 32))
    tm_prep = _pick_tile(rows, (256, 128, 64, 32))
    ts_p = _pick_tile(tp, (128, 64, 32, 16))
    tq_fox = _pick_tile(tp, (256, 128))
    tk_fox = _pick_tile(tp, (512, 256, 128))
    tq_dsa = _pick_tile(tp, (256, 128))
    tk_dsa = _pick_tile(tp, (512, 256, 128))
    nk_s = past + ts
    nk_s_pad = -(-nk_s // 384) * 384
    tb_p = _pick_tile(tp, (512, 256, 128))

    lg = jnp.log1p(-jnp.exp2(-5.0 - jnp.arange(A_HEADS, dtype=F32)))
    la_ret = jnp.repeat(lg, A_DK)[None, :]
    ones_gain = jnp.ones((1, 512), F32)
    zero_state = jnp.zeros((bp, 2, 256, LANES), F32)

    a_p, a_s, bk_p, bk_s, bv_p, bv_s, bf_p, bf_s = [], [], [], [], [], [], [], []
    ck_p, ck_s, cv_p, cv_s, ci_p, ci_s, d_p, d_s = [], [], [], [], [], [], [], []

    for l in range(depth):
        i = l // 2
        if l % 2 == 0:
            w_in = jnp.concatenate(
                [even_w_in[i], jnp.zeros((D_MODEL, EV_WIDTH - even_w_in.shape[2]), F32)], 1).astype(BF16)
            h = norm_matmul(y, norm_mix[l], w_in, tm=tm, tn=EV_WIDTH, name="even_in_proj")
            bf = jnp.concatenate([even_b_f[i], jnp.zeros((LANES - B_HEADS,), F32)])[None, :]
            qka, qbn, kbn, lf = prep_even(
                h, cos, slo, shi, jnp.tile(even_q_norm[i], B_HEADS)[None, :],
                jnp.tile(even_k_norm[i], B_HEADS)[None, :], bf, gmat, tm=tm_prep)
            oa_p, sa_p = linear_scan(qka, 0, qka, 1, h, EV_VA // 512, h, EV_GA // 512,
                                     jnp.broadcast_to(la_ret, (rp, 256)), ones_gain, zero_state,
                                     row0=0, nb=bp, t=tp, ts=ts_p, name="retention_prompt")
            oa_s, sa_s = linear_scan(qka, 0, qka, 1, h, EV_VA // 512, h, EV_GA // 512,
                                     jnp.broadcast_to(la_ret, (rs, 256)), ones_gain,
                                     _state_to_pairs(state_a[i]),
                                     row0=rp, nb=nb, t=ts, ts=ts, name="retention_decode")
            lf8 = lf[:, :B_HEADS]
            c_p = row_cumsum(lf, row0=0, nb=bp, n=tp, tb=tb_p)
            ob_p = fox_attention(qbn, kbn, h, EV_VB // 512, c_p,
                                 q_row0=0, nb=bp, tq_total=tp, tk_total=tp, q_off=0,
                                 tq=tq_fox, tk=tk_fox, name="fox_prompt")
            kb_new = kbn[rp:].reshape(nb, ts, 512)
            vb_new = h[rp:, EV_VB:EV_VB + 512].reshape(nb, ts, 512)
            lf_past = jnp.pad(cache_b_logf[i], ((0, 0), (0, 0), (0, LANES - B_HEADS)))
            lf_all = _pad_keys(jnp.concatenate([lf_past, lf[rp:].reshape(nb, ts, LANES)], 1), nk_s_pad)
            c_s = row_cumsum(lf_all, row0=0, nb=nb, n=nk_s_pad, tb=nk_s_pad // 3)
            k_all = _pad_keys(jnp.concatenate([cache_b_k[i].reshape(nb, past, 512), kb_new], 1), nk_s_pad)
            v_all = _pad_keys(jnp.concatenate([cache_b_v[i].reshape(nb, past, 512), vb_new], 1), nk_s_pad)
            ob_s = fox_attention(qbn, k_all, v_all, 0, c_s,
                                 q_row0=rp, nb=nb, tq_total=ts, tk_total=nk_s_pad, q_off=past,
                                 tq=ts, tk=nk_s_pad, name="fox_decode")
            w_out = even_w_out[i].astype(BF16)
            y = matmul_residual(y, [jnp.concatenate([oa_p, oa_s], 0), jnp.concatenate([ob_p, ob_s], 0)],
                                [w_out[:512], w_out[512:]], tm=tm, name="even_out_proj")
            a_p.append(_pairs_to_state(sa_p))
            a_s.append(_pairs_to_state(sa_s))
            bk_p.append(kbn[:rp].reshape(bp, tp, B_HEADS, B_DH))
            bk_s.append(kb_new.reshape(nb, ts, B_HEADS, B_DH))
            bv_p.append(h[:rp, EV_VB:EV_VB + 512].reshape(bp, tp, B_HEADS, B_DH))
            bv_s.append(vb_new.reshape(nb, ts, B_HEADS, B_DH))
            bf_p.append(lf8[:rp].reshape(bp, tp, B_HEADS))
            bf_s.append(lf8[rp:].reshape(nb, ts, B_HEADS))
        else:
            w = odd_w_in[i]
            offs = np.cumsum([0, 512, 128, 128, 512, 64, 8, 256, 256, 512, 512, 16])
            qc_w, kc_w, vc_w, iq_w, ik_w, iw_w, qd_w, kd_w, vd_w, gd_w, gr_w = [
                w[:, int(a):int(b)] for a, b in zip(offs[:-1], offs[1:])]
            qc_w = qc_w.reshape(D_MODEL, 2, 4, C_DH).transpose(0, 2, 1, 3).reshape(D_MODEL, 512)
            w_in = jnp.concatenate(
                [qc_w, iq_w, vd_w, gd_w, qd_w, kd_w, kc_w, vc_w, ik_w, iw_w, gr_w,
                 jnp.zeros((D_MODEL, OD_WIDTH - OD_MISC - 88), F32)], 1).astype(BF16)
            h = norm_matmul(y, norm_mix[l], w_in, tm=tm, tn=OD_WIDTH, name="odd_in_proj")
            wg = jnp.zeros((LANES, 256), F32).at[MISC_GR:MISC_GR + D_GATE_RANK].set(
                odd_w_gate_up[i]).astype(BF16)
            qcr, iqr, qkd, kcr, ikr, kcb, vcb, ikb, la = prep_odd(
                h, cos, slo, shi, jnp.tile(odd_q_norm[i], C_HEADS)[None, :],
                jnp.tile(odd_k_norm[i], C_KV_HEADS)[None, :], wg, odd_b_gate[i][None, :], gmat,
                tm=tm_prep)
            oc_p = dsa_attention(iqr, h, OD_MISC // LANES, qcr, kcb[:rp], vcb[:rp], ikb[:rp],
                                 q_row0=0, nb=bp, tq_total=tp, n_keys=tp, n_keys_pad=tp, q_off=0,
                                 tq=tq_dsa, tk=tk_dsa, name="dsa_prompt")

            def with_past(cache, new):
                return _pad_keys(jnp.concatenate(
                    [cache.reshape(nb, past, -1).astype(BF16), new[rp:].reshape(nb, ts, -1)], 1), nk_s_pad)

            ik_past = jnp.concatenate([cache_c_idx[i], cache_c_idx[i]], axis=-1)
            oc_s = dsa_attention(iqr, h, OD_MISC // LANES, qcr,
                                 with_past(cache_c_k[i], kcb), with_past(cache_c_v[i], vcb),
                                 with_past(ik_past, ikb),
                                 q_row0=rp, nb=nb, tq_total=ts, n_keys=nk_s, n_keys_pad=nk_s_pad,
                                 q_off=past, tq=ts, tk=384, name="dsa_decode")
            gain = jnp.tile(odd_o_norm[i], D_HEADS)[None, :]
            od_p, sd_p = linear_scan(qkd, 0, qkd, 1, h, OD_VD // 512, h, OD_GD // 512, la[:rp], gain,
                                     zero_state, row0=0, nb=bp, t=tp, ts=ts_p, name="gla_prompt")
            od_s, sd_s = linear_scan(qkd, 0, qkd, 1, h, OD_VD // 512, h, OD_GD // 512, la[rp:], gain,
                                     _state_to_pairs(state_d[i]), row0=rp, nb=nb, t=ts, ts=ts,
                                     name="gla_decode")
            w_out = odd_w_out[i]
            w_oc = w_out[:512].reshape(2, 4, C_DH, D_MODEL).transpose(1, 0, 2, 3).reshape(512, D_MODEL)
            y = matmul_residual(y, [jnp.concatenate([oc_p, oc_s], 0), jnp.concatenate([od_p, od_s], 0)],
                                [w_oc.astype(BF16), w_out[512:].astype(BF16)], tm=tm,
                                name="odd_out_proj")
            d_p.append(_pairs_to_state(sd_p))
            d_s.append(_pairs_to_state(sd_s))
            ck_p.append(kcr[:rp].reshape(bp, tp, C_KV_HEADS, C_DH))
            ck_s.append(kcr[rp:].reshape(nb, ts, C_KV_HEADS, C_DH))
            cv_p.append(h[:rp, OD_KVC + LANES:OD_KVC + 2 * LANES].reshape(bp, tp, C_KV_HEADS, C_DH))
            cv_s.append(h[rp:, OD_KVC + LANES:OD_KVC + 2 * LANES].reshape(nb, ts, C_KV_HEADS, C_DH))
            ci_p.append(ikr[:rp, :IDX_DIM].reshape(bp, tp, IDX_DIM))
            ci_s.append(ikr[rp:, :IDX_DIM].reshape(nb, ts, IDX_DIM))
        hid = norm_matmul(y, norm_ffn[l], ffn_w1[l].astype(BF16), tm=tm, tn=2048, relu2=True,
                          out_dtype=BF16, name="mlp_up")
        y = matmul_residual(y, [hid], [ffn_w2[l].astype(BF16)], tm=tm, name="mlp_down")

    return (y[:rp].reshape(bp, tp, D_MODEL), y[rp:].reshape(nb, ts, D_MODEL),
            jnp.stack(a_p), jnp.stack(a_s), jnp.stack(bk_p), jnp.stack(bk_s),
            jnp.stack(bv_p), jnp.stack(bv_s), jnp.stack(bf_p), jnp.stack(bf_s),
            jnp.stack(ck_p), jnp.stack(ck_s), jnp.stack(cv_p), jnp.stack(cv_s),
            jnp.stack(ci_p), jnp.stack(ci_s), jnp.stack(d_p), jnp.stack(d_s))
```

```python
import functools

import numpy as np
import jax
import jax.numpy as jnp
from jax import lax
from jax.experimental import pallas as pl
from jax.experimental.pallas import tpu as pltpu

F32 = jnp.float32
BF16 = jnp.bfloat16
I32 = jnp.int32
I16 = jnp.int16

D_MODEL = 1024
CHUNK = 64
ROPE_THETA = 10000.0
EPS = 1e-6
A_HEADS, A_DK, A_DV = 4, 64, 128
B_HEADS, B_DH = 8, 64
C_HEADS, C_KV_HEADS, C_DH = 8, 2, 64
IDX_HEADS, IDX_DIM = 8, 64
IDX_SCALE = (IDX_HEADS * IDX_DIM) ** -0.5
TOPK_MAX = 256
D_HEADS, D_DK, D_DV = 4, 64, 128
D_GATE_RANK = 16
D_GATE_NORM = 16.0
D_FF = 4 * D_MODEL

LANES = 128
SUBLANES = 8
VMEM_LIMIT_BYTES = 56 * 1024 * 1024

HEAD = 64
SUB = 16
FOLD = 64
COUNT_FOLD = 32
ONES_ROWS = 16
NEG = -1e30
LOG2E = 1.4426950408889634
INT_MIN = -(2 ** 31)

EV_QK, EV_VA, EV_GA, EV_QB, EV_KB, EV_VB, EV_FB = 0, 512, 1024, 1536, 2048, 2560, 3072
EV_WIDTH = 3200
OD_QC, OD_IQ, OD_VD, OD_GD, OD_QKD, OD_KVC, OD_MISC = 0, 512, 1024, 1536, 2048, 2560, 2816
OD_WIDTH = 2944
MISC_IK, MISC_IW, MISC_GR = 0, 64, 72


def _cparams(sem):
    return pltpu.CompilerParams(dimension_semantics=sem, vmem_limit_bytes=VMEM_LIMIT_BYTES)


def _dot(a, b):
    return jnp.dot(a, b, preferred_element_type=F32)


def _dot_nt(a, b):
    return lax.dot_general(a, b, (((1,), (1,)), ((), ())), preferred_element_type=F32)


def _dot_tn(a, b):
    return lax.dot_general(a, b, (((0,), (0,)), ((), ())), preferred_element_type=F32)


def _split3(x):
    a1 = x.astype(BF16)
    r1 = x - a1.astype(F32)
    a2 = r1.astype(BF16)
    a3 = (r1 - a2.astype(F32)).astype(BF16)
    return a1, a2, a3


def _half_masks():
    lane = lax.broadcasted_iota(I32, (1, LANES), 1)
    lo = (lane < HEAD).astype(F32)
    return lo, 1.0 - lo


def _log_sigmoid(x):
    return jnp.minimum(x, 0.0) - jnp.log1p(jnp.exp(-jnp.abs(x)))


def _silu(x):
    return x / (1.0 + jnp.exp(-x))


def _norm_mm_kernel(x_ref, g_ref, w_ref, o_ref, *, relu2):
    x = x_ref[...]
    ms = jnp.mean(x * x, axis=-1, keepdims=True)
    xn = (x * lax.rsqrt(ms + EPS) * g_ref[...]).astype(BF16)
    y = _dot(xn, w_ref[...])
    if relu2:
        y = jnp.maximum(y, 0.0)
        y = y * y
    o_ref[...] = y.astype(o_ref.dtype)


def norm_matmul(x, g, w, *, tm, tn, relu2=False, out_dtype=F32, name):
    rows, k = x.shape
    n = w.shape[1]
    return pl.pallas_call(
        functools.partial(_norm_mm_kernel, relu2=relu2),
        grid=(rows // tm, n // tn),
        in_specs=[pl.BlockSpec((tm, k), lambda i, j: (i, 0)),
                  pl.BlockSpec((1, k), lambda i, j: (0, 0)),
                  pl.BlockSpec((k, tn), lambda i, j: (0, j))],
        out_specs=pl.BlockSpec((tm, tn), lambda i, j: (i, j)),
        out_shape=jax.ShapeDtypeStruct((rows, n), out_dtype),
        compiler_params=_cparams(("parallel", "parallel")),
        name=name,
    )(x, g.reshape(1, k), w)


def _mm_res_kernel(*refs, n_in):
    res_ref = refs[0]
    o_ref = refs[-1]
    acc = res_ref[...]
    for a_ref, w_ref in zip(refs[1:1 + n_in], refs[1 + n_in:1 + 2 * n_in]):
        acc = acc + _dot(a_ref[...].astype(BF16), w_ref[...])
    o_ref[...] = acc


def matmul_residual(res, a_list, w_list, *, tm, name):
    rows, n = res.shape
    n_in = len(a_list)
    in_specs = [pl.BlockSpec((tm, n), lambda i: (i, 0))]
    in_specs += [pl.BlockSpec((tm, a.shape[1]), lambda i: (i, 0)) for a in a_list]
    in_specs += [pl.BlockSpec(w.shape, lambda i: (0, 0)) for w in w_list]
    return pl.pallas_call(
        functools.partial(_mm_res_kernel, n_in=n_in),
        grid=(rows // tm,),
        in_specs=in_specs,
        out_specs=pl.BlockSpec((tm, n), lambda i: (i, 0)),
        out_shape=jax.ShapeDtypeStruct((rows, n), F32),
        compiler_params=_cparams(("parallel",)),
        name=name,
    )(res, *a_list, *w_list)


def _tile_lanes(t, width):
    n = width // LANES
    return t if n == 1 else jnp.concatenate([t] * n, axis=1)


def _rope(x, cos, sin_lo, sin_hi):
    w = x.shape[1]
    up = pltpu.roll(x, w - HEAD // 2, 1)
    dn = pltpu.roll(x, HEAD // 2, 1)
    return x * _tile_lanes(cos, w) + up * _tile_lanes(sin_lo, w) + dn * _tile_lanes(sin_hi, w)


def _head_rms(x, gmat):
    s = x * x
    hi = s.astype(BF16)
    lo = (s - hi.astype(F32)).astype(BF16)
    ms = (_dot(hi, gmat) + _dot(lo, gmat)) * (1.0 / HEAD)
    return x * lax.rsqrt(ms + EPS)


def _prep_even_kernel(qk_ref, qb_ref, kb_ref, fb_ref, cos_ref, slo_ref, shi_ref,
                      qg_ref, kg_ref, bf_ref, gmat_ref,
                      qka_ref, qbn_ref, kbn_ref, lf_ref):
    cos, slo, shi = cos_ref[...], slo_ref[...], shi_ref[...]
    lane = lax.broadcasted_iota(I32, (1, 4 * LANES), 1)
    kscale = jnp.where(lane < 2 * LANES, 1.0, A_DK ** -0.5)
    qka_ref[...] = _rope(qk_ref[...], cos, slo, shi) * kscale
    gmat = gmat_ref[...]
    qbn_ref[...] = _head_rms(qb_ref[...], gmat) * (qg_ref[...] * (B_DH ** -0.5 * LOG2E))
    kbn_ref[...] = _head_rms(kb_ref[...], gmat) * kg_ref[...]
    lf_ref[...] = _log_sigmoid(fb_ref[...] + bf_ref[...])


def prep_even(h, cos, slo, shi, qg, kg, bf, gmat, *, tm):
    rows = h.shape[0]
    blk = lambda w, c: pl.BlockSpec((tm, w), lambda i, c=c: (i, c))
    row = lambda w: pl.BlockSpec((1, w), lambda i: (0, 0))
    return pl.pallas_call(
        _prep_even_kernel,
        grid=(rows // tm,),
        in_specs=[blk(512, EV_QK // 512), blk(512, EV_QB // 512), blk(512, EV_KB // 512),
                  blk(LANES, EV_FB // LANES), blk(LANES, 0), blk(LANES, 0), blk(LANES, 0),
                  row(512), row(512), row(LANES),
                  pl.BlockSpec((512, 512), lambda i: (0, 0))],
        out_specs=[blk(512, 0), blk(512, 0), blk(512, 0), blk(LANES, 0)],
        out_shape=[jax.ShapeDtypeStruct((rows, 512), F32)] * 3
        + [jax.ShapeDtypeStruct((rows, LANES), F32)],
        compiler_params=_cparams(("parallel",)),
        name="prep_even",
    )(h, h, h, h, cos, slo, shi, qg, kg, bf, gmat)


def _prep_odd_kernel(qc_ref, iq_ref, qkd_ref, kvc_ref, misc_ref, cos_ref, slo_ref, shi_ref,
                     qg_ref, kg_ref, wg_ref, bg_ref, gmat_ref,
                     qcr_ref, iqr_ref, qkdo_ref, kcr_ref, ikr_ref, kcb_ref, vcb_ref, ikb_ref,
                     la_ref):
    cos, slo, shi = cos_ref[...], slo_ref[...], shi_ref[...]
    gmat = gmat_ref[...]
    qc = _head_rms(qc_ref[...], gmat) * (qg_ref[...] * (C_DH ** -0.5 * LOG2E))
    qcr_ref[...] = _rope(qc, cos, slo, shi)
    iqr_ref[...] = _rope(iq_ref[...], cos, slo, shi)
    lane = lax.broadcasted_iota(I32, (1, 4 * LANES), 1)
    qkdo_ref[...] = qkd_ref[...] * jnp.where(lane < 2 * LANES, D_DK ** -0.5, 1.0)
    kvc = kvc_ref[...]
    kc = _head_rms(kvc[:, :LANES], gmat[:LANES, :LANES]) * kg_ref[...]
    kcr = _rope(kc, cos, slo, shi)
    kcr_ref[...] = kcr
    kcb_ref[...] = kcr.astype(BF16)
    vcb_ref[...] = kvc[:, LANES:].astype(BF16)
    misc = misc_ref[...]
    ikr = _rope(misc, cos, slo, shi)
    lane1 = lax.broadcasted_iota(I32, (1, LANES), 1)
    ik2 = jnp.where(lane1 < HEAD, ikr, pltpu.roll(ikr, HEAD, 1))
    ikr_ref[...] = ik2
    ikb_ref[...] = ik2.astype(BF16)
    z = _dot(misc.astype(BF16), wg_ref[...]) + bg_ref[...]
    la_ref[...] = _log_sigmoid(z) * (1.0 / D_GATE_NORM)


def prep_odd(h, cos, slo, shi, qg, kg, wg, bg, gmat, *, tm):
    rows = h.shape[0]
    blk = lambda w, c: pl.BlockSpec((tm, w), lambda i, c=c: (i, c))
    row = lambda w: pl.BlockSpec((1, w), lambda i: (0, 0))
    return pl.pallas_call(
        _prep_odd_kernel,
        grid=(rows // tm,),
        in_specs=[blk(512, OD_QC // 512), blk(512, OD_IQ // 512), blk(512, OD_QKD // 512),
                  blk(256, OD_KVC // 256), blk(LANES, OD_MISC // LANES),
                  blk(LANES, 0), blk(LANES, 0), blk(LANES, 0),
                  row(512), row(LANES),
                  pl.BlockSpec((LANES, 256), lambda i: (0, 0)), row(256),
                  pl.BlockSpec((512, 512), lambda i: (0, 0))],
        out_specs=[blk(512, 0), blk(512, 0), blk(512, 0), blk(LANES, 0), blk(LANES, 0),
                   blk(LANES, 0), blk(LANES, 0), blk(LANES, 0), blk(256, 0)],
        out_shape=[jax.ShapeDtypeStruct((rows, 512), F32)] * 3
        + [jax.ShapeDtypeStruct((rows, LANES), F32)] * 2
        + [jax.ShapeDtypeStruct((rows, LANES), BF16)] * 3
        + [jax.ShapeDtypeStruct((rows, 256), F32)],
        compiler_params=_cparams(("parallel",)),
        name="prep_odd",
    )(h, h, h, h, h, cos, slo, shi, qg, kg, wg, bg, gmat)


def _scan_kernel(q_ref, k_ref, v_ref, g_ref, la_ref, gain_ref, s0_ref, tri_ref, bmask_ref,
                 o_ref, sfin_ref, st_ref, *, ts, nsteps):
    step = pl.program_id(1)

    @pl.when(step == 0)
    def _():
        st_ref[...] = s0_ref[0]

    q, k, v = q_ref[...], k_ref[...], v_ref[...]
    tri = tri_ref[...]
    a1, a2, a3 = _split3(la_ref[...])
    cum = _dot(tri, a1) + _dot(tri, a2) + _dot(tri, a3)
    qt = q * jnp.exp(cum)
    kt = k * jnp.exp(-cum)
    hm = _half_masks()
    bmask = bmask_ref[...]
    ri = lax.broadcasted_iota(I32, (SUB, SUB), 0)
    ci = lax.broadcasted_iota(I32, (SUB, SUB), 1)
    causal = ci <= ri
    o_rows = []
    for u in range(ts // SUB):
        r0 = u * SUB
        last = cum[r0 + SUB - 1:r0 + SUB, :]
        k2 = k[r0:r0 + SUB, :] * jnp.exp(last - cum[r0:r0 + SUB, :])
        elast = jnp.exp(last)
        o_pairs = []
        for p in range(2):
            l0 = p * LANES
            qp = qt[r0:r0 + SUB, l0:l0 + LANES]
            kp = kt[r0:r0 + SUB, l0:l0 + LANES].astype(BF16)
            st = st_ref[p]
            inter = _dot_nt(qp.astype(BF16), st.astype(BF16))
            intra = []
            for hh in range(2):
                att = _dot_nt((qp * hm[hh]).astype(BF16), kp)
                att = jnp.where(causal, att, 0.0)
                c0 = (2 * p + hh) * LANES
                intra.append(_dot(att.astype(BF16), v[r0:r0 + SUB, c0:c0 + LANES].astype(BF16)))
            o_pairs.append(inter + jnp.concatenate(intra, axis=1))
            upd = _dot_tn(v[r0:r0 + SUB, 2 * l0:2 * l0 + 2 * LANES].astype(BF16),
                          k2[:, l0:l0 + LANES].astype(BF16))
            st_ref[p] = st * elast[:, l0:l0 + LANES] + upd * bmask
        o_rows.append(jnp.concatenate(o_pairs, axis=1))
    o = jnp.concatenate(o_rows, axis=0)
    g = g_ref[...]
    gain = gain_ref[...]
    outs = []
    for h in range(4):
        oh = o[:, h * LANES:(h + 1) * LANES]
        ms = jnp.mean(oh * oh, axis=-1, keepdims=True)
        outs.append(oh * lax.rsqrt(ms + EPS) * gain[:, h * LANES:(h + 1) * LANES]
                    * _silu(g[:, h * LANES:(h + 1) * LANES]))
    o_ref[...] = jnp.concatenate(outs, axis=1)

    @pl.when(step == nsteps - 1)
    def _():
        sfin_ref[0] = st_ref[...]


def _scan_consts(ts):
    r = np.arange(ts)
    tri = ((r[:, None] // SUB == r[None, :] // SUB) & (r[None, :] <= r[:, None])).astype(np.float32)
    row = np.arange(2 * LANES)[:, None] // LANES
    col = np.arange(LANES)[None, :] // HEAD
    bmask = (row == col).astype(np.float32)
    return jnp.asarray(tri, BF16), jnp.asarray(bmask, F32)


def linear_scan(q_arr, q_col, k_arr, k_col, v_arr, v_col, g_arr, g_col, la, gain, s0,
                *, row0, nb, t, ts, name):
    nsteps = t // ts
    blk0 = row0 // ts
    tri, bmask = _scan_consts(ts)
    src = lambda w, c: pl.BlockSpec((ts, w), lambda b, s, c=c: (blk0 + b * nsteps + s, c))
    loc = lambda w: pl.BlockSpec((ts, w), lambda b, s: (b * nsteps + s, 0))
    return pl.pallas_call(
        functools.partial(_scan_kernel, ts=ts, nsteps=nsteps),
        grid=(nb, nsteps),
        in_specs=[src(256, q_col), src(256, k_col), src(512, v_col), src(512, g_col), loc(256),
                  pl.BlockSpec((1, 512), lambda b, s: (0, 0)),
                  pl.BlockSpec((1, 2, 256, LANES), lambda b, s: (b, 0, 0, 0)),
                  pl.BlockSpec((ts, ts), lambda b, s: (0, 0)),
                  pl.BlockSpec((256, LANES), lambda b, s: (0, 0))],
        out_specs=[loc(512), pl.BlockSpec((1, 2, 256, LANES), lambda b, s: (b, 0, 0, 0))],
        out_shape=[jax.ShapeDtypeStruct((nb * t, 512), F32),
                   jax.ShapeDtypeStruct((nb, 2, 256, LANES), F32)],
        scratch_shapes=[pltpu.VMEM((2, 256, LANES), F32)],
        compiler_params=_cparams(("parallel", "arbitrary")),
        name=name,
    )(q_arr, k_arr, v_arr, g_arr, la, gain, s0, tri, bmask)


def _state_to_pairs(s):
    nb = s.shape[0]
    st = jnp.swapaxes(s, -1, -2).reshape(nb, 2, 2, LANES, HEAD)
    z = jnp.zeros_like(st[:, :, 0])
    top = jnp.concatenate([st[:, :, 0], z], axis=-1)
    bot = jnp.concatenate([z, st[:, :, 1]], axis=-1)
    return jnp.concatenate([top, bot], axis=-2)


def _pairs_to_state(sp):
    nb = sp.shape[0]
    h0 = sp[:, :, :LANES, :HEAD]
    h1 = sp[:, :, LANES:, HEAD:]
    st = jnp.stack([h0, h1], axis=2).reshape(nb, 4, LANES, HEAD)
    return jnp.swapaxes(st, -1, -2)


def _cumsum_kernel(x_ref, tri_ref, o_ref, carry_ref):
    @pl.when(pl.program_id(1) == 0)
    def _():
        carry_ref[...] = jnp.zeros_like(carry_ref)

    tri = tri_ref[...]
    a1, a2, a3 = _split3(x_ref[...])
    c = _dot(tri, a1) + _dot(tri, a2) + _dot(tri, a3) + carry_ref[0:1, :]
    o_ref[...] = c * LOG2E
    carry_ref[...] = jnp.broadcast_to(c[-1:, :], carry_ref.shape)


def row_cumsum(x, *, row0, nb, n, tb):
    r = np.arange(tb)
    tri = jnp.asarray((r[None, :] <= r[:, None]).astype(np.float32), BF16)
    nblk = n // tb
    blk0 = row0 // tb
    return pl.pallas_call(
        _cumsum_kernel,
        grid=(nb, nblk),
        in_specs=[pl.BlockSpec((tb, LANES), lambda b, j: (blk0 + b * nblk + j, 0)),
                  pl.BlockSpec((tb, tb), lambda b, j: (0, 0))],
        out_specs=pl.BlockSpec((tb, LANES), lambda b, j: (b * nblk + j, 0)),
        out_shape=jax.ShapeDtypeStruct((nb * n, LANES), F32),
        scratch_shapes=[pltpu.VMEM((SUBLANES, LANES), F32)],
        compiler_params=_cparams(("parallel", "arbitrary")),
        name="logf_cumsum",
    )(x, tri)


def _col_reduce(x, op):
    tk, tq = x.shape
    if tk > FOLD and tk % FOLD == 0:
        x = op(x.reshape(tk // FOLD, FOLD, tq), axis=0)
    return op(x, axis=0, keepdims=True)


def _values_with_ones(vt_head):
    ones = jnp.ones((ONES_ROWS, vt_head.shape[1]), F32)
    return jnp.concatenate([vt_head, ones], axis=0).astype(BF16)


def _attn_update(s, smax, vt_ext, m_ref, acc_ref, idx):
    m_prev = m_ref[idx]
    m_new = jnp.maximum(m_prev, smax)
    alpha = jnp.exp2(m_prev - m_new)
    p = jnp.exp2(s - m_new)
    acc_ref[idx] = alpha * acc_ref[idx] + _dot(vt_ext, p.astype(BF16))
    m_ref[idx] = m_new


def _eye(n):
    r = lax.broadcasted_iota(I32, (n, n), 0)
    c = lax.broadcasted_iota(I32, (n, n), 1)
    return jnp.where(r == c, 1.0, 0.0).astype(BF16)


def _transpose_exact(x, eye):
    a1, a2, a3 = _split3(x)
    return _dot_nt(eye, a1) + _dot_nt(eye, a2) + _dot_nt(eye, a3)


def _write_heads(o_ref, acc_ref, tq):
    eye = _eye(tq)
    for c in range(4):
        a0, a1 = acc_ref[2 * c], acc_ref[2 * c + 1]
        ot = jnp.concatenate([a0[:HEAD] / a0[HEAD:HEAD + 1], a1[:HEAD] / a1[HEAD:HEAD + 1]], axis=0)
        o_ref[:, c * LANES:(c + 1) * LANES] = _transpose_exact(ot, eye)


def _fox_kernel(qi_ref, kj_ref, q_ref, k_ref, v_ref, c_ref, o_ref, m_ref, acc_ref,
                *, tq, tk, q_off):
    s_id = pl.program_id(1)
    qi = qi_ref[s_id]
    kj = kj_ref[s_id]
    j_last = (q_off + (qi + 1) * tq - 1) // tk

    @pl.when(kj == 0)
    def _():
        m_ref[...] = jnp.full_like(m_ref, NEG)
        acc_ref[...] = jnp.zeros_like(acc_ref)

    def block(masked):
        hm = _half_masks()
        q, k, v, c = q_ref[...], k_ref[...], v_ref[...], c_ref[...]
        if masked:
            kpos = kj * tk + lax.broadcasted_iota(I32, (tk, tq), 0)
            qpos = q_off + qi * tq + lax.broadcasted_iota(I32, (tk, tq), 1)
            visible = kpos <= qpos
        scores, maxima = [], []
        for h in range(B_HEADS):
            l0 = (h // 2) * LANES
            qm = (q[:, l0:l0 + LANES] * hm[h % 2]).astype(BF16)
            s = _dot_nt(k[:, l0:l0 + LANES].astype(BF16), qm) - c[:, h:h + 1]
            if masked:
                s = jnp.where(visible, s, NEG)
            scores.append(s)
            maxima.append(_col_reduce(s, jnp.max))
        for p in range(4):
            l0 = p * LANES
            vt = v[:, l0:l0 + LANES].T
            for hh in range(2):
                h = 2 * p + hh
                _attn_update(scores[h], maxima[h], _values_with_ones(vt[hh * HEAD:(hh + 1) * HEAD]),
                             m_ref, acc_ref, h)

    @pl.when(kj < j_last)
    def _():
        block(False)

    @pl.when(kj == j_last)
    def _():
        block(True)
        _write_heads(o_ref, acc_ref, tq)


def fox_attention(q_arr, k_arr, v_arr, v_col, c_arr, *, q_row0, nb, tq_total, tk_total, q_off,
                  tq, tk, name):
    nq = tq_total // tq
    steps = [(i, j) for i in range(nq) for j in range((q_off + (i + 1) * tq - 1) // tk + 1)]
    qi = jnp.asarray([s[0] for s in steps], I32)
    kj = jnp.asarray([s[1] for s in steps], I32)
    qb0 = q_row0 // tq
    nkb = tk_total // tk
    grid_spec = pltpu.PrefetchScalarGridSpec(
        num_scalar_prefetch=2,
        grid=(nb, len(steps)),
        in_specs=[
            pl.BlockSpec((tq, 512), lambda b, s, qi, kj: (qb0 + b * nq + qi[s], 0)),
            pl.BlockSpec((tk, 512), lambda b, s, qi, kj: (b * nkb + kj[s], 0)),
            pl.BlockSpec((tk, 512), lambda b, s, qi, kj: (b * nkb + kj[s], v_col)),
            pl.BlockSpec((tk, LANES), lambda b, s, qi, kj: (b * nkb + kj[s], 0)),
        ],
        out_specs=pl.BlockSpec((tq, 512), lambda b, s, qi, kj: (b * nq + qi[s], 0)),
        scratch_shapes=[pltpu.VMEM((8, 1, tq), F32), pltpu.VMEM((8, HEAD + ONES_ROWS, tq), F32)],
    )
    return pl.pallas_call(
        functools.partial(_fox_kernel, tq=tq, tk=tk, q_off=q_off),
        grid_spec=grid_spec,
        out_shape=jax.ShapeDtypeStruct((nb * tq_total, 4 * LANES), F32),
        compiler_params=_cparams(("parallel", "arbitrary")),
        name=name,
    )(qi, kj, q_arr, k_arr, v_arr, c_arr)


def _sortable_key(x):
    b = lax.bitcast_convert_type(x + 0.0, I32)
    return jnp.where(b < 0, b ^ jnp.int32(0x7FFFFFFF), b)


def _dsa_kernel(iq_ref, misc_ref, qc_ref, kc_ref, vc_ref, ik_ref, o_ref,
                keys_ref, half_ref, qm_ref, m_ref, acc_ref, thr_ref, jcut_ref,
                *, tq, tk, n_keys, q_off, ksel):
    i = pl.program_id(1)
    hm = _half_masks()
    qcol = lax.broadcasted_iota(I32, (1, tq), 1)
    lim = jnp.minimum(((q_off + i * tq + qcol) // CHUNK + 1) * CHUNK, n_keys)
    lim_max = jnp.minimum(((q_off + (i + 1) * tq - 1) // CHUNK + 1) * CHUNK, n_keys)
    nkb = (lim_max + tk - 1) // tk
    eye = _eye(LANES)

    def key_pos(kb):
        return kb * tk + lax.broadcasted_iota(I32, (tk, tq), 0)

    iq = iq_ref[...]
    for hd in range(IDX_HEADS):
        c0 = (hd // 2) * LANES
        qm_ref[hd] = (iq[:, c0:c0 + LANES] * hm[hd % 2]).astype(BF16)
    iwt = _transpose_exact(misc_ref[...], eye) * IDX_SCALE

    def score_body(kb, carry):
        k0 = pl.multiple_of(kb * tk, tk)
        ikb = ik_ref[pl.ds(k0, tk), :]
        sc = jnp.zeros((tk, tq), F32)
        for hd in range(IDX_HEADS):
            a = _dot_nt(ikb, qm_ref[hd])
            sc = sc + jnp.maximum(a, 0.0) * iwt[MISC_IW + hd:MISC_IW + hd + 1, :]
        key = jnp.where(key_pos(kb) < lim, _sortable_key(sc), INT_MIN)
        keys_ref[kb] = key
        half_ref[kb] = jnp.right_shift(key, 16).astype(I16)
        return carry

    lax.fori_loop(0, nkb, score_body, 0)

    def count(pred):
        def body(kb, acc):
            hit = jnp.where(pred(keys_ref[kb], key_pos(kb)), 1.0, 0.0)
            return acc + jnp.sum(hit.reshape(tk // COUNT_FOLD, COUNT_FOLD, tq), axis=0)
        acc = lax.fori_loop(0, nkb, body, jnp.zeros((COUNT_FOLD, tq), F32))
        return jnp.sum(acc, axis=0, keepdims=True)

    def count_half(cand):
        def body(kb, acc):
            hit = jnp.where(half_ref[kb] >= cand, jnp.int16(1), jnp.int16(0))
            parts = [hit[r * COUNT_FOLD:(r + 1) * COUNT_FOLD] for r in range(tk // COUNT_FOLD)]
            while len(parts) > 1:
                parts = [a + b for a, b in zip(parts[0::2], parts[1::2])] + parts[len(parts) & ~1:]
            return acc + parts[0]
        acc = lax.fori_loop(0, nkb, body, jnp.zeros((COUNT_FOLD, tq), I16))
        return jnp.sum(acc.astype(F32), axis=0, keepdims=True)

    def any_row(flag):
        return jnp.max(jnp.where(flag, 1.0, 0.0)) > 0.0

    thr_ref[...] = jnp.full((1, tq), INT_MIN + 1, I32)
    jcut_ref[...] = jnp.full((1, tq), 2 ** 30, I32)
    kf = float(ksel)

    @pl.when(lim_max > ksel)
    def _():
        few = lim <= ksel

        def unsettled(cres):
            return jnp.logical_and(cres != kf, jnp.logical_not(few))

        def search(first, last, to_half, state):
            def cond(state):
                t, _, _, go = state
                return jnp.logical_and(t < last, go)

            def body(state):
                t, res, cres, _ = state
                cand = res + jnp.left_shift(jnp.int32(1), 30 - t)
                cnt = count_half(to_half(cand))
                take = cnt >= kf
                res = jnp.where(take, cand, res)
                cres = jnp.where(take, cnt, cres)
                return t + 1, res, cres, any_row(unsettled(cres))

            return lax.while_loop(cond, body, (jnp.int32(first),) + state)[1:]

        def high_half(x):
            return jnp.right_shift(x, 16).astype(I16)

        def low_half(x):
            return ((x & 0xFFFF) - 32768).astype(I16)

        c0 = count_half(jnp.zeros((1, tq), I16))
        res = jnp.where(c0 >= kf, 0, INT_MIN).astype(I32)
        cres = jnp.where(c0 >= kf, c0, 2.0 * n_keys)
        res, cres, go = search(0, 15, high_half, (res, cres, any_row(unsettled(cres))))

        @pl.when(go)
        def _():
            rh = high_half(res)

            def relabel(kb, carry):
                hi = half_ref[kb]
                lo = low_half(keys_ref[kb])
                half_ref[kb] = jnp.where(hi > rh, jnp.int16(32767),
                                         jnp.where(hi < rh, jnp.int16(-32768), lo))
                return carry

            lax.fori_loop(0, nkb, relabel, 0)

        res, cres, open_rows = search(15, 31, low_half, (res, cres, go))
        thr = jnp.maximum(res, INT_MIN + 1)
        thr_ref[...] = thr

        @pl.when(open_rows)
        def _():
            need = kf - count(lambda blk, kpos: blk > thr)
            n_eq = count(lambda blk, kpos: blk == thr)
            split = n_eq > need

            @pl.when(any_row(split))
            def _():
                def idx_body(t, lo):
                    cand = lo + jnp.left_shift(jnp.int32(1), 14 - t)
                    cnt = count(lambda blk, kpos: (blk == thr) & (kpos < cand))
                    return jnp.where(cnt < need, cand, lo)

                lo = lax.fori_loop(0, 15, idx_body, jnp.zeros((1, tq), I32))
                jcut_ref[...] = jnp.where(split, lo, 2 ** 30)

    m_ref[...] = jnp.full_like(m_ref, NEG)
    acc_ref[...] = jnp.zeros_like(acc_ref)
    qc = qc_ref[...]
    for hd in range(C_HEADS):
        c0 = (hd // 2) * LANES
        qm_ref[hd] = (qc[:, c0:c0 + LANES] * hm[hd % 2]).astype(BF16)
    thr = thr_ref[...]
    jcut = jcut_ref[...]

    def attn_body(kb, carry):
        k0 = pl.multiple_of(kb * tk, tk)
        kblk = kc_ref[pl.ds(k0, tk), :]
        vt = _dot_nt(eye, vc_ref[pl.ds(k0, tk), :])
        vts = [_values_with_ones(vt[hh * HEAD:(hh + 1) * HEAD]) for hh in range(2)]
        keys = keys_ref[kb]
        sel = (keys > thr) | ((keys == thr) & (key_pos(kb) <= jcut))
        scores = [jnp.where(sel, _dot_nt(kblk, qm_ref[hd]), NEG) for hd in range(C_HEADS)]
        maxima = [_col_reduce(s, jnp.max) for s in scores]
        for hd in range(C_HEADS):
            _attn_update(scores[hd], maxima[hd], vts[hd % 2], m_ref, acc_ref, hd)
        return carry

    lax.fori_loop(0, nkb, attn_body, 0)
    _write_heads(o_ref, acc_ref, tq)


def dsa_attention(iq_arr, misc_arr, misc_col, qc_arr, kc, vc, ik, *, q_row0, nb, tq_total,
                  n_keys, n_keys_pad, q_off, tq, tk, name):
    nq = tq_total // tq
    qb0 = q_row0 // tq
    ksel = min(TOPK_MAX, n_keys // 4)
    qspec = lambda w, c: pl.BlockSpec((tq, w), lambda b, i, c=c: (qb0 + b * nq + i, c))
    kspec = pl.BlockSpec((n_keys_pad, LANES), lambda b, i: (b, 0))
    return pl.pallas_call(
        functools.partial(_dsa_kernel, tq=tq, tk=tk, n_keys=n_keys, q_off=q_off, ksel=ksel),
        grid=(nb, nq),
        in_specs=[qspec(512, 0), qspec(LANES, misc_col), qspec(512, 0), kspec, kspec, kspec],
        out_specs=pl.BlockSpec((tq, 512), lambda b, i: (b * nq + i, 0)),
        out_shape=jax.ShapeDtypeStruct((nb * tq_total, 512), F32),
        scratch_shapes=[pltpu.VMEM((n_keys_pad // tk, tk, tq), I32),
                        pltpu.VMEM((n_keys_pad // tk, tk, tq), I16),
                        pltpu.VMEM((8, tq, LANES), BF16),
                        pltpu.VMEM((8, 1, tq), F32),
                        pltpu.VMEM((8, HEAD + ONES_ROWS, tq), F32),
                        pltpu.VMEM((1, tq), I32), pltpu.VMEM((1, tq), I32)],
        compiler_params=_cparams(("parallel", "arbitrary")),
        name=name,
    )(iq_arr, misc_arr, qc_arr, kc, vc, ik)


def _rope_tables(pos):
    half = HEAD // 2
    inv = jnp.power(ROPE_THETA, -jnp.arange(half, dtype=F32) / half)
    ang = pos.astype(F32)[:, None] * inv[None, :]
    cos, sin = jnp.cos(ang), jnp.sin(ang)
    z = jnp.zeros_like(sin)
    return (jnp.tile(cos, (1, 4)), jnp.tile(jnp.concatenate([-sin, z], 1), (1, 2)),
            jnp.tile(jnp.concatenate([z, sin], 1), (1, 2)))


def _head_mean_matrix():
    r = np.arange(512)
    return jnp.asarray((r[:, None] // HEAD == r[None, :] // HEAD).astype(np.float32), BF16)


def _pick_tile(n, prefs):
    for t in prefs:
        if n % t == 0:
            return t
    raise ValueError(f"no tile in {prefs} divides {n}")


def _pad_keys(x, n_pad):
    nb, n, w = x.shape
    if n_pad > n:
        x = jnp.concatenate([x, jnp.zeros((nb, n_pad - n, w), x.dtype)], axis=1)
    return x.reshape(nb * n_pad, w)


def kernel(x_prompt, x_sample, state_a, cache_b_k, cache_b_v, cache_b_logf, cache_c_k, cache_c_v,
           cache_c_idx, state_d, norm_mix, norm_ffn, even_w_in, even_b_f, even_q_norm, even_k_norm,
           even_w_out, odd_w_in, odd_w_gate_up, odd_b_gate, odd_q_norm, odd_k_norm, odd_o_norm,
           odd_w_out, ffn_w1, ffn_w2):
    bp, tp = x_prompt.shape[:2]
    nb, ts = x_sample.shape[:2]
    past = cache_b_k.shape[2]
    depth = norm_mix.shape[0]
    rp, rs = bp * tp, nb * ts
    rows = rp + rs
    assert bp == 1 and tp % 128 == 0 and ts % SUB == 0 and rp % ts == 0 and past % CHUNK == 0

    y = jnp.concatenate([x_prompt.reshape(rp, D_MODEL), x_sample.reshape(rs, D_MODEL)], axis=0)
    pos = jnp.concatenate([jnp.arange(tp, dtype=I32),
                           jnp.tile(past + jnp.arange(ts, dtype=I32), nb)])
    cos, slo, shi = _rope_tables(pos)
    gmat = _head_mean_matrix()

    tm = _pick_tile(rows, (512, 256, 128, 64, 32))
    tm_prep = _pick_tile(rows, (256, 128, 64, 32))
    ts_p = _pick_tile(tp, (128, 64, 32, 16))
    tq_fox = _pick_tile(tp, (256, 128))
    tk_fox = _pick_tile(tp, (512, 256, 128))
    tq_dsa = _pick_tile(tp, (256, 128))
    tk_dsa = _pick_tile(tp, (512, 256, 128))
    nk_s = past + ts
    nk_s_pad = -(-nk_s // 384) * 384
    tb_p = _pick_tile(tp, (512, 256, 128))

    lg = jnp.log1p(-jnp.exp2(-5.0 - jnp.arange(A_HEADS, dtype=F32)))
    la_ret = jnp.repeat(lg, A_DK)[None, :]
    ones_gain = jnp.ones((1, 512), F32)
    zero_state = jnp.zeros((bp, 2, 256, LANES), F32)

    a_p, a_s, bk_p, bk_s, bv_p, bv_s, bf_p, bf_s = [], [], [], [], [], [], [], []
    ck_p, ck_s, cv_p, cv_s, ci_p, ci_s, d_p, d_s = [], [], [], [], [], [], [], []

    for l in range(depth):
        i = l // 2
        if l % 2 == 0:
            w_in = jnp.concatenate(
                [even_w_in[i], jnp.zeros((D_MODEL, EV_WIDTH - even_w_in.shape[2]), F32)], 1).astype(BF16)
            h = norm_matmul(y, norm_mix[l], w_in, tm=tm, tn=EV_WIDTH, name="even_in_proj")
            bf = jnp.concatenate([even_b_f[i], jnp.zeros((LANES - B_HEADS,), F32)])[None, :]
            qka, qbn, kbn, lf = prep_even(
                h, cos, slo, shi, jnp.tile(even_q_norm[i], B_HEADS)[None, :],
                jnp.tile(even_k_norm[i], B_HEADS)[None, :], bf, gmat, tm=tm_prep)
            oa_p, sa_p = linear_scan(qka, 0, qka, 1, h, EV_VA // 512, h, EV_GA // 512,
                                     jnp.broadcast_to(la_ret, (rp, 256)), ones_gain, zero_state,
                                     row0=0, nb=bp, t=tp, ts=ts_p, name="retention_prompt")
            oa_s, sa_s = linear_scan(qka, 0, qka, 1, h, EV_VA // 512, h, EV_GA // 512,
                                     jnp.broadcast_to(la_ret, (rs, 256)), ones_gain,
                                     _state_to_pairs(state_a[i]),
                                     row0=rp, nb=nb, t=ts, ts=ts, name="retention_decode")
            lf8 = lf[:, :B_HEADS]
            c_p = row_cumsum(lf, row0=0, nb=bp, n=tp, tb=tb_p)
            ob_p = fox_attention(qbn, kbn, h, EV_VB // 512, c_p,
                                 q_row0=0, nb=bp, tq_total=tp, tk_total=tp, q_off=0,
                                 tq=tq_fox, tk=tk_fox, name="fox_prompt")
            kb_new = kbn[rp:].reshape(nb, ts, 512)
            vb_new = h[rp:, EV_VB:EV_VB + 512].reshape(nb, ts, 512)
            lf_past = jnp.pad(cache_b_logf[i], ((0, 0), (0, 0), (0, LANES - B_HEADS)))
            lf_all = _pad_keys(jnp.concatenate([lf_past, lf[rp:].reshape(nb, ts, LANES)], 1), nk_s_pad)
            c_s = row_cumsum(lf_all, row0=0, nb=nb, n=nk_s_pad, tb=nk_s_pad // 3)
            k_all = _pad_keys(jnp.concatenate([cache_b_k[i].reshape(nb, past, 512), kb_new], 1), nk_s_pad)
            v_all = _pad_keys(jnp.concatenate([cache_b_v[i].reshape(nb, past, 512), vb_new], 1), nk_s_pad)
            ob_s = fox_attention(qbn, k_all, v_all, 0, c_s,
                                 q_row0=rp, nb=nb, tq_total=ts, tk_total=nk_s_pad, q_off=past,
                                 tq=ts, tk=nk_s_pad, name="fox_decode")
            w_out = even_w_out[i].astype(BF16)
            y = matmul_residual(y, [jnp.concatenate([oa_p, oa_s], 0), jnp.concatenate([ob_p, ob_s], 0)],
                                [w_out[:512], w_out[512:]], tm=tm, name="even_out_proj")
            a_p.append(_pairs_to_state(sa_p))
            a_s.append(_pairs_to_state(sa_s))
            bk_p.append(kbn[:rp].reshape(bp, tp, B_HEADS, B_DH))
            bk_s.append(kb_new.reshape(nb, ts, B_HEADS, B_DH))
            bv_p.append(h[:rp, EV_VB:EV_VB + 512].reshape(bp, tp, B_HEADS, B_DH))
            bv_s.append(vb_new.reshape(nb, ts, B_HEADS, B_DH))
            bf_p.append(lf8[:rp].reshape(bp, tp, B_HEADS))
            bf_s.append(lf8[rp:].reshape(nb, ts, B_HEADS))
        else:
            w = odd_w_in[i]
            offs = np.cumsum([0, 512, 128, 128, 512, 64, 8, 256, 256, 512, 512, 16])
            qc_w, kc_w, vc_w, iq_w, ik_w, iw_w, qd_w, kd_w, vd_w, gd_w, gr_w = [
                w[:, int(a):int(b)] for a, b in zip(offs[:-1], offs[1:])]
            qc_w = qc_w.reshape(D_MODEL, 2, 4, C_DH).transpose(0, 2, 1, 3).reshape(D_MODEL, 512)
            w_in = jnp.concatenate(
                [qc_w, iq_w, vd_w, gd_w, qd_w, kd_w, kc_w, vc_w, ik_w, iw_w, gr_w,
                 jnp.zeros((D_MODEL, OD_WIDTH - OD_MISC - 88), F32)], 1).astype(BF16)
            h = norm_matmul(y, norm_mix[l], w_in, tm=tm, tn=OD_WIDTH, name="odd_in_proj")
            wg = jnp.zeros((LANES, 256), F32).at[MISC_GR:MISC_GR + D_GATE_RANK].set(
                odd_w_gate_up[i]).astype(BF16)
            qcr, iqr, qkd, kcr, ikr, kcb, vcb, ikb, la = prep_odd(
                h, cos, slo, shi, jnp.tile(odd_q_norm[i], C_HEADS)[None, :],
                jnp.tile(odd_k_norm[i], C_KV_HEADS)[None, :], wg, odd_b_gate[i][None, :], gmat,
                tm=tm_prep)
            oc_p = dsa_attention(iqr, h, OD_MISC // LANES, qcr, kcb[:rp], vcb[:rp], ikb[:rp],
                                 q_row0=0, nb=bp, tq_total=tp, n_keys=tp, n_keys_pad=tp, q_off=0,
                                 tq=tq_dsa, tk=tk_dsa, name="dsa_prompt")

            def with_past(cache, new):
                return _pad_keys(jnp.concatenate(
                    [cache.reshape(nb, past, -1).astype(BF16), new[rp:].reshape(nb, ts, -1)], 1), nk_s_pad)

            ik_past = jnp.concatenate([cache_c_idx[i], cache_c_idx[i]], axis=-1)
            oc_s = dsa_attention(iqr, h, OD_MISC // LANES, qcr,
                                 with_past(cache_c_k[i], kcb), with_past(cache_c_v[i], vcb),
                                 with_past(ik_past, ikb),
                                 q_row0=rp, nb=nb, tq_total=ts, n_keys=nk_s, n_keys_pad=nk_s_pad,
                                 q_off=past, tq=ts, tk=384, name="dsa_decode")
            gain = jnp.tile(odd_o_norm[i], D_HEADS)[None, :]
            od_p, sd_p = linear_scan(qkd, 0, qkd, 1, h, OD_VD // 512, h, OD_GD // 512, la[:rp], gain,
                                     zero_state, row0=0, nb=bp, t=tp, ts=ts_p, name="gla_prompt")
            od_s, sd_s = linear_scan(qkd, 0, qkd, 1, h, OD_VD // 512, h, OD_GD // 512, la[rp:], gain,
                                     _state_to_pairs(state_d[i]), row0=rp, nb=nb, t=ts, ts=ts,
                                     name="gla_decode")
            w_out = odd_w_out[i]
            w_oc = w_out[:512].reshape(2, 4, C_DH, D_MODEL).transpose(1, 0, 2, 3).reshape(512, D_MODEL)
            y = matmul_residual(y, [jnp.concatenate([oc_p, oc_s], 0), jnp.concatenate([od_p, od_s], 0)],
                                [w_oc.astype(BF16), w_out[512:].astype(BF16)], tm=tm,
                                name="odd_out_proj")
            d_p.append(_pairs_to_state(sd_p))
            d_s.append(_pairs_to_state(sd_s))
            ck_p.append(kcr[:rp].reshape(bp, tp, C_KV_HEADS, C_DH))
            ck_s.append(kcr[rp:].reshape(nb, ts, C_KV_HEADS, C_DH))
            cv_p.append(h[:rp, OD_KVC + LANES:OD_KVC + 2 * LANES].reshape(bp, tp, C_KV_HEADS, C_DH))
            cv_s.append(h[rp:, OD_KVC + LANES:OD_KVC + 2 * LANES].reshape(nb, ts, C_KV_HEADS, C_DH))
            ci_p.append(ikr[:rp, :IDX_DIM].reshape(bp, tp, IDX_DIM))
            ci_s.append(ikr[rp:, :IDX_DIM].reshape(nb, ts, IDX_DIM))
        hid = norm_matmul(y, norm_ffn[l], ffn_w1[l].astype(BF16), tm=tm, tn=2048, relu2=True,
                          out_dtype=BF16, name="mlp_up")
        y = matmul_residual(y, [hid], [ffn_w2[l].astype(BF16)], tm=tm, name="mlp_down")

    return (y[:rp].reshape(bp, tp, D_MODEL), y[rp:].reshape(nb, ts, D_MODEL),
            jnp.stack(a_p), jnp.stack(a_s), jnp.stack(bk_p), jnp.stack(bk_s),
            jnp.stack(bv_p), jnp.stack(bv_s), jnp.stack(bf_p), jnp.stack(bf_s),
            jnp.stack(ck_p), jnp.stack(ck_s), jnp.stack(cv_p), jnp.stack(cv_s),
            jnp.stack(ci_p), jnp.stack(ci_s), jnp.stack(d_p), jnp.stack(d_s))
```

```python
import functools

import numpy as np
import jax
import jax.numpy as jnp
from jax import lax
from jax.experimental import pallas as pl
from jax.experimental.pallas import tpu as pltpu

F32 = jnp.float32
BF16 = jnp.bfloat16
I32 = jnp.int32
I16 = jnp.int16

D_MODEL = 1024
CHUNK = 64
ROPE_THETA = 10000.0
EPS = 1e-6
A_HEADS, A_DK, A_DV = 4, 64, 128
B_HEADS, B_DH = 8, 64
C_HEADS, C_KV_HEADS, C_DH = 8, 2, 64
IDX_HEADS, IDX_DIM = 8, 64
IDX_SCALE = (IDX_HEADS * IDX_DIM) ** -0.5
TOPK_MAX = 256
D_HEADS, D_DK, D_DV = 4, 64, 128
D_GATE_RANK = 16
D_GATE_NORM = 16.0
D_FF = 4 * D_MODEL

LANES = 128
SUBLANES = 8
VMEM_LIMIT_BYTES = 56 * 1024 * 1024

HEAD = 64
SUB = 16
FOLD = 64
COUNT_FOLD = 32
ONES_ROWS = 16
NEG = -1e30
LOG2E = 1.4426950408889634
INT_MIN = -(2 ** 31)

EV_QK, EV_VA, EV_GA, EV_QB, EV_KB, EV_VB, EV_FB = 0, 512, 1024, 1536, 2048, 2560, 3072
EV_WIDTH = 3200
OD_QC, OD_IQ, OD_VD, OD_GD, OD_QKD, OD_KVC, OD_MISC = 0, 512, 1024, 1536, 2048, 2560, 2816
OD_WIDTH = 2944
MISC_IK, MISC_IW, MISC_GR = 0, 64, 72


def _cparams(sem):
    return pltpu.CompilerParams(dimension_semantics=sem, vmem_limit_bytes=VMEM_LIMIT_BYTES)


def _dot(a, b):
    return jnp.dot(a, b, preferred_element_type=F32)


def _dot_nt(a, b):
    return lax.dot_general(a, b, (((1,), (1,)), ((), ())), preferred_element_type=F32)


def _dot_tn(a, b):
    return lax.dot_general(a, b, (((0,), (0,)), ((), ())), preferred_element_type=F32)


def _split3(x):
    a1 = x.astype(BF16)
    r1 = x - a1.astype(F32)
    a2 = r1.astype(BF16)
    a3 = (r1 - a2.astype(F32)).astype(BF16)
    return a1, a2, a3


def _half_masks():
    lane = lax.broadcasted_iota(I32, (1, LANES), 1)
    lo = (lane < HEAD).astype(F32)
    return lo, 1.0 - lo


def _log_sigmoid(x):
    return jnp.minimum(x, 0.0) - jnp.log1p(jnp.exp(-jnp.abs(x)))


def _silu(x):
    return x / (1.0 + jnp.exp(-x))


def _norm_mm_kernel(x_ref, g_ref, w_ref, o_ref, *, relu2):
    x = x_ref[...]
    ms = jnp.mean(x * x, axis=-1, keepdims=True)
    xn = (x * lax.rsqrt(ms + EPS) * g_ref[...]).astype(BF16)
    y = _dot(xn, w_ref[...])
    if relu2:
        y = jnp.maximum(y, 0.0)
        y = y * y
    o_ref[...] = y.astype(o_ref.dtype)


def norm_matmul(x, g, w, *, tm, tn, relu2=False, out_dtype=F32, name):
    rows, k = x.shape
    n = w.shape[1]
    return pl.pallas_call(
        functools.partial(_norm_mm_kernel, relu2=relu2),
        grid=(rows // tm, n // tn),
        in_specs=[pl.BlockSpec((tm, k), lambda i, j: (i, 0)),
                  pl.BlockSpec((1, k), lambda i, j: (0, 0)),
                  pl.BlockSpec((k, tn), lambda i, j: (0, j))],
        out_specs=pl.BlockSpec((tm, tn), lambda i, j: (i, j)),
        out_shape=jax.ShapeDtypeStruct((rows, n), out_dtype),
        compiler_params=_cparams(("parallel", "parallel")),
        name=name,
    )(x, g.reshape(1, k), w)


def _mm_res_kernel(*refs, n_in):
    res_ref = refs[0]
    o_ref = refs[-1]
    acc = res_ref[...]
    for a_ref, w_ref in zip(refs[1:1 + n_in], refs[1 + n_in:1 + 2 * n_in]):
        acc = acc + _dot(a_ref[...].astype(BF16), w_ref[...])
    o_ref[...] = acc


def matmul_residual(res, a_list, w_list, *, tm, name):
    rows, n = res.shape
    n_in = len(a_list)
    in_specs = [pl.BlockSpec((tm, n), lambda i: (i, 0))]
    in_specs += [pl.BlockSpec((tm, a.shape[1]), lambda i: (i, 0)) for a in a_list]
    in_specs += [pl.BlockSpec(w.shape, lambda i: (0, 0)) for w in w_list]
    return pl.pallas_call(
        functools.partial(_mm_res_kernel, n_in=n_in),
        grid=(rows // tm,),
        in_specs=in_specs,
        out_specs=pl.BlockSpec((tm, n), lambda i: (i, 0)),
        out_shape=jax.ShapeDtypeStruct((rows, n), F32),
        compiler_params=_cparams(("parallel",)),
        name=name,
    )(res, *a_list, *w_list)


def _tile_lanes(t, width):
    n = width // LANES
    return t if n == 1 else jnp.concatenate([t] * n, axis=1)


def _rope(x, cos, sin_lo, sin_hi):
    w = x.shape[1]
    up = pltpu.roll(x, w - HEAD // 2, 1)
    dn = pltpu.roll(x, HEAD // 2, 1)
    return x * _tile_lanes(cos, w) + up * _tile_lanes(sin_lo, w) + dn * _tile_lanes(sin_hi, w)


def _head_rms(x, gmat):
    s = x * x
    hi = s.astype(BF16)
    lo = (s - hi.astype(F32)).astype(BF16)
    ms = (_dot(hi, gmat) + _dot(lo, gmat)) * (1.0 / HEAD)
    return x * lax.rsqrt(ms + EPS)


def _prep_even_kernel(qk_ref, qb_ref, kb_ref, fb_ref, cos_ref, slo_ref, shi_ref,
                      qg_ref, kg_ref, bf_ref, gmat_ref,
                      qka_ref, qbn_ref, kbn_ref, lf_ref):
    cos, slo, shi = cos_ref[...], slo_ref[...], shi_ref[...]
    lane = lax.broadcasted_iota(I32, (1, 4 * LANES), 1)
    kscale = jnp.where(lane < 2 * LANES, 1.0, A_DK ** -0.5)
    qka_ref[...] = _rope(qk_ref[...], cos, slo, shi) * kscale
    gmat = gmat_ref[...]
    qbn_ref[...] = _head_rms(qb_ref[...], gmat) * (qg_ref[...] * (B_DH ** -0.5 * LOG2E))
    kbn_ref[...] = _head_rms(kb_ref[...], gmat) * kg_ref[...]
    lf_ref[...] = _log_sigmoid(fb_ref[...] + bf_ref[...])


def prep_even(h, cos, slo, shi, qg, kg, bf, gmat, *, tm):
    rows = h.shape[0]
    blk = lambda w, c: pl.BlockSpec((tm, w), lambda i, c=c: (i, c))
    row = lambda w: pl.BlockSpec((1, w), lambda i: (0, 0))
    return pl.pallas_call(
        _prep_even_kernel,
        grid=(rows // tm,),
        in_specs=[blk(512, EV_QK // 512), blk(512, EV_QB // 512), blk(512, EV_KB // 512),
                  blk(LANES, EV_FB // LANES), blk(LANES, 0), blk(LANES, 0), blk(LANES, 0),
                  row(512), row(512), row(LANES),
                  pl.BlockSpec((512, 512), lambda i: (0, 0))],
        out_specs=[blk(512, 0), blk(512, 0), blk(512, 0), blk(LANES, 0)],
        out_shape=[jax.ShapeDtypeStruct((rows, 512), F32)] * 3
        + [jax.ShapeDtypeStruct((rows, LANES), F32)],
        compiler_params=_cparams(("parallel",)),
        name="prep_even",
    )(h, h, h, h, cos, slo, shi, qg, kg, bf, gmat)


def _prep_odd_kernel(qc_ref, iq_ref, qkd_ref, kvc_ref, misc_ref, cos_ref, slo_ref, shi_ref,
                     qg_ref, kg_ref, wg_ref, bg_ref, gmat_ref,
                     qcr_ref, iqr_ref, qkdo_ref, kcr_ref, ikr_ref, kcb_ref, vcb_ref, ikb_ref,
                     la_ref):
    cos, slo, shi = cos_ref[...], slo_ref[...], shi_ref[...]
    gmat = gmat_ref[...]
    qc = _head_rms(qc_ref[...], gmat) * (qg_ref[...] * (C_DH ** -0.5 * LOG2E))
    qcr_ref[...] = _rope(qc, cos, slo, shi)
    iqr_ref[...] = _rope(iq_ref[...], cos, slo, shi)
    lane = lax.broadcasted_iota(I32, (1, 4 * LANES), 1)
    qkdo_ref[...] = qkd_ref[...] * jnp.where(lane < 2 * LANES, D_DK ** -0.5, 1.0)
    kvc = kvc_ref[...]
    kc = _head_rms(kvc[:, :LANES], gmat[:LANES, :LANES]) * kg_ref[...]
    kcr = _rope(kc, cos, slo, shi)
    kcr_ref[...] = kcr
    kcb_ref[...] = kcr.astype(BF16)
    vcb_ref[...] = kvc[:, LANES:].astype(BF16)
    misc = misc_ref[...]
    ikr = _rope(misc, cos, slo, shi)
    lane1 = lax.broadcasted_iota(I32, (1, LANES), 1)
    ik2 = jnp.where(lane1 < HEAD, ikr, pltpu.roll(ikr, HEAD, 1))
    ikr_ref[...] = ik2
    ikb_ref[...] = ik2.astype(BF16)
    z = _dot(misc.astype(BF16), wg_ref[...]) + bg_ref[...]
    la_ref[...] = _log_sigmoid(z) * (1.0 / D_GATE_NORM)


def prep_odd(h, cos, slo, shi, qg, kg, wg, bg, gmat, *, tm):
    rows = h.shape[0]
    blk = lambda w, c: pl.BlockSpec((tm, w), lambda i, c=c: (i, c))
    row = lambda w: pl.BlockSpec((1, w), lambda i: (0, 0))
    return pl.pallas_call(
        _prep_odd_kernel,
        grid=(rows // tm,),
        in_specs=[blk(512, OD_QC // 512), blk(512, OD_IQ // 512), blk(512, OD_QKD // 512),
                  blk(256, OD_KVC // 256), blk(LANES, OD_MISC // LANES),
                  blk(LANES, 0), blk(LANES, 0), blk(LANES, 0),
                  row(512), row(LANES),
                  pl.BlockSpec((LANES, 256), lambda i: (0, 0)), row(256),
                  pl.BlockSpec((512, 512), lambda i: (0, 0))],
        out_specs=[blk(512, 0), blk(512, 0), blk(512, 0), blk(LANES, 0), blk(LANES, 0),
                   blk(LANES, 0), blk(LANES, 0), blk(LANES, 0), blk(256, 0)],
        out_shape=[jax.ShapeDtypeStruct((rows, 512), F32)] * 3
        + [jax.ShapeDtypeStruct((rows, LANES), F32)] * 2
        + [jax.ShapeDtypeStruct((rows, LANES), BF16)] * 3
        + [jax.ShapeDtypeStruct((rows, 256), F32)],
        compiler_params=_cparams(("parallel",)),
        name="prep_odd",
    )(h, h, h, h, h, cos, slo, shi, qg, kg, wg, bg, gmat)


def _scan_kernel(q_ref, k_ref, v_ref, g_ref, la_ref, gain_ref, s0_ref, tri_ref, bmask_ref,
                 o_ref, sfin_ref, st_ref, *, ts, nsteps):
    step = pl.program_id(1)

    @pl.when(step == 0)
    def _():
        st_ref[...] = s0_ref[0]

    q, k, v = q_ref[...], k_ref[...], v_ref[...]
    tri = tri_ref[...]
    a1, a2, a3 = _split3(la_ref[...])
    cum = _dot(tri, a1) + _dot(tri, a2) + _dot(tri, a3)
    qt = q * jnp.exp(cum)
    kt = k * jnp.exp(-cum)
    hm = _half_masks()
    bmask = bmask_ref[...]
    nsub = ts // SUB
    vb = v.astype(BF16)
    elast, upd = [], []
    for u in range(nsub):
        r0 = u * SUB
        last = cum[r0 + SUB - 1:r0 + SUB, :]
        k2 = (k[r0:r0 + SUB, :] * jnp.exp(last - cum[r0:r0 + SUB, :])).astype(BF16)
        elast.append(jnp.exp(last))
        upd.append([_dot_tn(vb[r0:r0 + SUB, 2 * p * LANES:2 * (p + 1) * LANES],
                            k2[:, p * LANES:(p + 1) * LANES]) * bmask for p in range(2)])
    seen = []
    for p in range(2):
        st = st_ref[p]
        row = []
        for u in range(nsub):
            row.append(st.astype(BF16))
            st = st * elast[u][:, p * LANES:(p + 1) * LANES] + upd[u][p]
        st_ref[p] = st
        seen.append(row)
    qtb = qt.astype(BF16)
    inter = jnp.concatenate(
        [jnp.concatenate([_dot_nt(qtb[u * SUB:(u + 1) * SUB, p * LANES:(p + 1) * LANES], seen[p][u])
                          for p in range(2)], axis=1) for u in range(nsub)], axis=0)
    ktb = kt.astype(BF16)
    same_chunk_causal = tri > 0
    intra = []
    for h in range(4):
        l0 = (h // 2) * LANES
        att = _dot_nt((qt[:, l0:l0 + LANES] * hm[h % 2]).astype(BF16), ktb[:, l0:l0 + LANES])
        att = jnp.where(same_chunk_causal, att, 0.0)
        intra.append(_dot(att.astype(BF16), vb[:, h * LANES:(h + 1) * LANES]))
    o = inter + jnp.concatenate(intra, axis=1)
    g = g_ref[...]
    gain = gain_ref[...]
    outs = []
    for h in range(4):
        oh = o[:, h * LANES:(h + 1) * LANES]
        ms = jnp.mean(oh * oh, axis=-1, keepdims=True)
        outs.append(oh * lax.rsqrt(ms + EPS) * gain[:, h * LANES:(h + 1) * LANES]
                    * _silu(g[:, h * LANES:(h + 1) * LANES]))
    o_ref[...] = jnp.concatenate(outs, axis=1)

    @pl.when(step == nsteps - 1)
    def _():
        sfin_ref[0] = st_ref[...]


def _scan_consts(ts):
    r = np.arange(ts)
    tri = ((r[:, None] // SUB == r[None, :] // SUB) & (r[None, :] <= r[:, None])).astype(np.float32)
    row = np.arange(2 * LANES)[:, None] // LANES
    col = np.arange(LANES)[None, :] // HEAD
    bmask = (row == col).astype(np.float32)
    return jnp.asarray(tri, BF16), jnp.asarray(bmask, F32)


def _ignore_last_input(fn, n_in):
    def wrapped(*refs):
        return fn(*refs[:n_in], *refs[n_in + 1:])
    return wrapped


def _shared_rows(into, rows, width):
    shape = jax.ShapeDtypeStruct((rows, width), F32)
    if into is None:
        return shape, [], []
    assert into.shape == (rows, width)
    return shape, [into], [pl.BlockSpec(memory_space=pl.ANY)]


def linear_scan(q_arr, q_col, k_arr, k_col, v_arr, v_col, g_arr, g_col, la, gain, s0,
                *, row0, nb, t, ts, la_shared, out_rows, into, name):
    nsteps = t // ts
    blk0 = row0 // ts
    tri, bmask = _scan_consts(ts)
    src = lambda w, c: pl.BlockSpec((ts, w), lambda b, s, c=c: (blk0 + b * nsteps + s, c))
    la_spec = pl.BlockSpec((ts, 256), lambda b, s: (0, 0)) if la_shared else src(256, 0)
    o_shape, extra, extra_specs = _shared_rows(into, out_rows, 512)
    n_in = 9
    kern = functools.partial(_scan_kernel, ts=ts, nsteps=nsteps)
    return pl.pallas_call(
        _ignore_last_input(kern, n_in) if extra else kern,
        grid=(nb, nsteps),
        in_specs=[src(256, q_col), src(256, k_col), src(512, v_col), src(512, g_col), la_spec,
                  pl.BlockSpec((1, 512), lambda b, s: (0, 0)),
                  pl.BlockSpec((1, 2, 256, LANES), lambda b, s: (b, 0, 0, 0)),
                  pl.BlockSpec((ts, ts), lambda b, s: (0, 0)),
                  pl.BlockSpec((256, LANES), lambda b, s: (0, 0))] + extra_specs,
        out_specs=[src(512, 0), pl.BlockSpec((1, 2, 256, LANES), lambda b, s: (b, 0, 0, 0))],
        out_shape=[o_shape, jax.ShapeDtypeStruct((nb, 2, 256, LANES), F32)],
        scratch_shapes=[pltpu.VMEM((2, 256, LANES), F32)],
        input_output_aliases={n_in: 0} if extra else {},
        compiler_params=_cparams(("parallel", "arbitrary")),
        name=name,
    )(q_arr, k_arr, v_arr, g_arr, la, gain, s0, tri, bmask, *extra)


def _state_to_pairs(s):
    nb = s.shape[0]
    st = jnp.swapaxes(s, -1, -2).reshape(nb, 2, 2, LANES, HEAD)
    z = jnp.zeros_like(st[:, :, 0])
    top = jnp.concatenate([st[:, :, 0], z], axis=-1)
    bot = jnp.concatenate([z, st[:, :, 1]], axis=-1)
    return jnp.concatenate([top, bot], axis=-2)


def _pairs_to_state(sp):
    nb = sp.shape[0]
    h0 = sp[:, :, :LANES, :HEAD]
    h1 = sp[:, :, LANES:, HEAD:]
    st = jnp.stack([h0, h1], axis=2).reshape(nb, 4, LANES, HEAD)
    return jnp.swapaxes(st, -1, -2)


def _cumsum_kernel(x_ref, tri_ref, o_ref, carry_ref):
    @pl.when(pl.program_id(1) == 0)
    def _():
        carry_ref[...] = jnp.zeros_like(carry_ref)

    tri = tri_ref[...]
    a1, a2, a3 = _split3(x_ref[...])
    c = _dot(tri, a1) + _dot(tri, a2) + _dot(tri, a3) + carry_ref[0:1, :]
    o_ref[...] = c * LOG2E
    carry_ref[...] = jnp.broadcast_to(c[-1:, :], carry_ref.shape)


def row_cumsum(x, *, row0, nb, n, tb):
    r = np.arange(tb)
    tri = jnp.asarray((r[None, :] <= r[:, None]).astype(np.float32), BF16)
    nblk = n // tb
    blk0 = row0 // tb
    return pl.pallas_call(
        _cumsum_kernel,
        grid=(nb, nblk),
        in_specs=[pl.BlockSpec((tb, LANES), lambda b, j: (blk0 + b * nblk + j, 0)),
                  pl.BlockSpec((tb, tb), lambda b, j: (0, 0))],
        out_specs=pl.BlockSpec((tb, LANES), lambda b, j: (b * nblk + j, 0)),
        out_shape=jax.ShapeDtypeStruct((nb * n, LANES), F32),
        scratch_shapes=[pltpu.VMEM((SUBLANES, LANES), F32)],
        compiler_params=_cparams(("parallel", "arbitrary")),
        name="logf_cumsum",
    )(x, tri)


def _col_reduce(x, op):
    tk, tq = x.shape
    if tk > FOLD and tk % FOLD == 0:
        x = op(x.reshape(tk // FOLD, FOLD, tq), axis=0)
    return op(x, axis=0, keepdims=True)


def _values_with_ones(vt_head):
    ones = jnp.ones((ONES_ROWS, vt_head.shape[1]), F32)
    return jnp.concatenate([vt_head, ones], axis=0).astype(BF16)


def _attn_update(s, smax, vt_ext, m_ref, acc_ref, idx):
    m_prev = m_ref[idx]
    m_new = jnp.maximum(m_prev, smax)
    alpha = jnp.exp2(m_prev - m_new)
    p = jnp.exp2(s - m_new)
    acc_ref[idx] = alpha * acc_ref[idx] + _dot(vt_ext, p.astype(BF16))
    m_ref[idx] = m_new


def _eye(n):
    r = lax.broadcasted_iota(I32, (n, n), 0)
    c = lax.broadcasted_iota(I32, (n, n), 1)
    return jnp.where(r == c, 1.0, 0.0).astype(BF16)


def _transpose_exact(x, eye):
    a1, a2, a3 = _split3(x)
    return _dot_nt(eye, a1) + _dot_nt(eye, a2) + _dot_nt(eye, a3)


def _write_heads(o_ref, acc_ref, tq):
    eye = _eye(tq)
    for c in range(4):
        a0, a1 = acc_ref[2 * c], acc_ref[2 * c + 1]
        ot = jnp.concatenate([a0[:HEAD] / a0[HEAD:HEAD + 1], a1[:HEAD] / a1[HEAD:HEAD + 1]], axis=0)
        o_ref[:, c * LANES:(c + 1) * LANES] = _transpose_exact(ot, eye)


def _fox_kernel(qi_ref, kj_ref, q_ref, k_ref, v_ref, c_ref, o_ref, m_ref, acc_ref,
                *, tq, tk, q_off):
    s_id = pl.program_id(1)
    qi = qi_ref[s_id]
    kj = kj_ref[s_id]
    j_last = (q_off + (qi + 1) * tq - 1) // tk

    @pl.when(kj == 0)
    def _():
        m_ref[...] = jnp.full_like(m_ref, NEG)
        acc_ref[...] = jnp.zeros_like(acc_ref)

    def block(masked):
        hm = _half_masks()
        q, k, v, c = q_ref[...], k_ref[...], v_ref[...], c_ref[...]
        if masked:
            kpos = kj * tk + lax.broadcasted_iota(I32, (tk, tq), 0)
            qpos = q_off + qi * tq + lax.broadcasted_iota(I32, (tk, tq), 1)
            visible = kpos <= qpos
        scores, maxima = [], []
        for h in range(B_HEADS):
            l0 = (h // 2) * LANES
            qm = (q[:, l0:l0 + LANES] * hm[h % 2]).astype(BF16)
            s = _dot_nt(k[:, l0:l0 + LANES].astype(BF16), qm) - c[:, h:h + 1]
            if masked:
                s = jnp.where(visible, s, NEG)
            scores.append(s)
            maxima.append(_col_reduce(s, jnp.max))
        for p in range(4):
            l0 = p * LANES
            vt = v[:, l0:l0 + LANES].T
            for hh in range(2):
                h = 2 * p + hh
                _attn_update(scores[h], maxima[h], _values_with_ones(vt[hh * HEAD:(hh + 1) * HEAD]),
                             m_ref, acc_ref, h)

    @pl.when(kj < j_last)
    def _():
        block(False)

    @pl.when(kj == j_last)
    def _():
        block(True)
        _write_heads(o_ref, acc_ref, tq)


def fox_attention(q_arr, k_arr, v_arr, v_col, c_arr, *, q_row0, nb, tq_total, tk_total, q_off,
                  tq, tk, out_rows, into, name):
    nq = tq_total // tq
    steps = [(i, j) for i in range(nq) for j in range((q_off + (i + 1) * tq - 1) // tk + 1)]
    qi = jnp.asarray([s[0] for s in steps], I32)
    kj = jnp.asarray([s[1] for s in steps], I32)
    qb0 = q_row0 // tq
    nkb = tk_total // tk
    o_shape, extra, extra_specs = _shared_rows(into, out_rows, 512)
    n_in = 6
    grid_spec = pltpu.PrefetchScalarGridSpec(
        num_scalar_prefetch=2,
        grid=(nb, len(steps)),
        in_specs=[
            pl.BlockSpec((tq, 512), lambda b, s, qi, kj: (qb0 + b * nq + qi[s], 0)),
            pl.BlockSpec((tk, 512), lambda b, s, qi, kj: (b * nkb + kj[s], 0)),
            pl.BlockSpec((tk, 512), lambda b, s, qi, kj: (b * nkb + kj[s], v_col)),
            pl.BlockSpec((tk, LANES), lambda b, s, qi, kj: (b * nkb + kj[s], 0)),
        ] + extra_specs,
        out_specs=pl.BlockSpec((tq, 512), lambda b, s, qi, kj: (qb0 + b * nq + qi[s], 0)),
        scratch_shapes=[pltpu.VMEM((8, 1, tq), F32), pltpu.VMEM((8, HEAD + ONES_ROWS, tq), F32)],
    )
    kern = functools.partial(_fox_kernel, tq=tq, tk=tk, q_off=q_off)
    return pl.pallas_call(
        _ignore_last_input(kern, n_in) if extra else kern,
        grid_spec=grid_spec,
        out_shape=o_shape,
        input_output_aliases={n_in: 0} if extra else {},
        compiler_params=_cparams(("parallel", "arbitrary")),
        name=name,
    )(qi, kj, q_arr, k_arr, v_arr, c_arr, *extra)


def _sortable_key(x):
    b = lax.bitcast_convert_type(x + 0.0, I32)
    return jnp.where(b < 0, b ^ jnp.int32(0x7FFFFFFF), b)


def _dsa_kernel(iq_ref, misc_ref, qc_ref, kc_ref, vc_ref, ik_ref, o_ref,
                keys_ref, half_ref, qm_ref, m_ref, acc_ref, thr_ref, jcut_ref,
                *, tq, tk, n_keys, q_off, ksel):
    i = pl.program_id(1)
    hm = _half_masks()
    qcol = lax.broadcasted_iota(I32, (1, tq), 1)
    lim = jnp.minimum(((q_off + i * tq + qcol) // CHUNK + 1) * CHUNK, n_keys)
    lim_max = jnp.minimum(((q_off + (i + 1) * tq - 1) // CHUNK + 1) * CHUNK, n_keys)
    nkb = (lim_max + tk - 1) // tk
    eye = _eye(LANES)

    def key_pos(kb):
        return kb * tk + lax.broadcasted_iota(I32, (tk, tq), 0)

    iq = iq_ref[...]
    for hd in range(IDX_HEADS):
        c0 = (hd // 2) * LANES
        qm_ref[hd] = (iq[:, c0:c0 + LANES] * hm[hd % 2]).astype(BF16)
    iwt = _transpose_exact(misc_ref[...], eye) * IDX_SCALE

    def score_body(kb, carry):
        k0 = pl.multiple_of(kb * tk, tk)
        ikb = ik_ref[pl.ds(k0, tk), :]
        sc = jnp.zeros((tk, tq), F32)
        for hd in range(IDX_HEADS):
            a = _dot_nt(ikb, qm_ref[hd])
            sc = sc + jnp.maximum(a, 0.0) * iwt[MISC_IW + hd:MISC_IW + hd + 1, :]
        key = jnp.where(key_pos(kb) < lim, _sortable_key(sc), INT_MIN)
        keys_ref[kb] = key
        half_ref[kb] = jnp.right_shift(key, 16).astype(I16)
        return carry

    lax.fori_loop(0, nkb, score_body, 0)

    def count(pred):
        def body(kb, acc):
            hit = jnp.where(pred(keys_ref[kb], key_pos(kb)), 1.0, 0.0)
            return acc + jnp.sum(hit.reshape(tk // COUNT_FOLD, COUNT_FOLD, tq), axis=0)
        acc = lax.fori_loop(0, nkb, body, jnp.zeros((COUNT_FOLD, tq), F32))
        return jnp.sum(acc, axis=0, keepdims=True)

    def count_half(cand):
        def body(kb, acc):
            hit = jnp.where(half_ref[kb] >= cand, jnp.int16(1), jnp.int16(0))
            parts = [hit[r * COUNT_FOLD:(r + 1) * COUNT_FOLD] for r in range(tk // COUNT_FOLD)]
            while len(parts) > 1:
                parts = [a + b for a, b in zip(parts[0::2], parts[1::2])] + parts[len(parts) & ~1:]
            return acc + parts[0]
        acc = lax.fori_loop(0, nkb, body, jnp.zeros((COUNT_FOLD, tq), I16))
        return jnp.sum(acc.astype(F32), axis=0, keepdims=True)

    def any_row(flag):
        return jnp.max(jnp.where(flag, 1.0, 0.0)) > 0.0

    thr_ref[...] = jnp.full((1, tq), INT_MIN + 1, I32)
    jcut_ref[...] = jnp.full((1, tq), 2 ** 30, I32)
    kf = float(ksel)

    @pl.when(lim_max > ksel)
    def _():
        few = lim <= ksel

        def unsettled(cres):
            return jnp.logical_and(cres != kf, jnp.logical_not(few))

        def search(first, last, group, to_half, state):
            assert (last - first) % group == 0

            def cond(state):
                t, _, _, go = state
                return jnp.logical_and(t < last, go)

            def body(state):
                t, res, cres, _ = state
                for g in range(group):
                    cand = res + jnp.left_shift(jnp.int32(1), 30 - g - t)
                    cnt = count_half(to_half(cand))
                    take = cnt >= kf
                    res = jnp.where(take, cand, res)
                    cres = jnp.where(take, cnt, cres)
                return t + group, res, cres, any_row(unsettled(cres))

            return lax.while_loop(cond, body, (jnp.int32(first),) + state)[1:]

        def high_half(x):
            return jnp.right_shift(x, 16).astype(I16)

        def low_half(x):
            return ((x & 0xFFFF) - 32768).astype(I16)

        c0 = count_half(jnp.zeros((1, tq), I16))
        res = jnp.where(c0 >= kf, 0, INT_MIN).astype(I32)
        cres = jnp.where(c0 >= kf, c0, 2.0 * n_keys)
        res, cres, go = search(0, 15, 3, high_half, (res, cres, any_row(unsettled(cres))))

        @pl.when(go)
        def _():
            rh = high_half(res)

            def relabel(kb, carry):
                hi = half_ref[kb]
                lo = low_half(keys_ref[kb])
                half_ref[kb] = jnp.where(hi > rh, jnp.int16(32767),
                                         jnp.where(hi < rh, jnp.int16(-32768), lo))
                return carry

            lax.fori_loop(0, nkb, relabel, 0)

        res, cres, open_rows = search(15, 31, 4, low_half, (res, cres, go))
        thr = jnp.maximum(res, INT_MIN + 1)
        thr_ref[...] = thr

        @pl.when(open_rows)
        def _():
            need = kf - count(lambda blk, kpos: blk > thr)
            n_eq = count(lambda blk, kpos: blk == thr)
            split = n_eq > need

            @pl.when(any_row(split))
            def _():
                def idx_body(t, lo):
                    cand = lo + jnp.left_shift(jnp.int32(1), 14 - t)
                    cnt = count(lambda blk, kpos: (blk == thr) & (kpos < cand))
                    return jnp.where(cnt < need, cand, lo)

                lo = lax.fori_loop(0, 15, idx_body, jnp.zeros((1, tq), I32))
                jcut_ref[...] = jnp.where(split, lo, 2 ** 30)

    m_ref[...] = jnp.full_like(m_ref, NEG)
    acc_ref[...] = jnp.zeros_like(acc_ref)
    qc = qc_ref[...]
    for hd in range(C_HEADS):
        c0 = (hd // 2) * LANES
        qm_ref[hd] = (qc[:, c0:c0 + LANES] * hm[hd % 2]).astype(BF16)
    thr = thr_ref[...]
    jcut = jcut_ref[...]

    def attn_body(kb, carry):
        k0 = pl.multiple_of(kb * tk, tk)
        kblk = kc_ref[pl.ds(k0, tk), :]
        vt = _dot_nt(eye, vc_ref[pl.ds(k0, tk), :])
        vts = [_values_with_ones(vt[hh * HEAD:(hh + 1) * HEAD]) for hh in range(2)]
        keys = keys_ref[kb]
        sel = (keys > thr) | ((keys == thr) & (key_pos(kb) <= jcut))
        scores = [jnp.where(sel, _dot_nt(kblk, qm_ref[hd]), NEG) for hd in range(C_HEADS)]
        maxima = [_col_reduce(s, jnp.max) for s in scores]
        for hd in range(C_HEADS):
            _attn_update(scores[hd], maxima[hd], vts[hd % 2], m_ref, acc_ref, hd)
        return carry

    lax.fori_loop(0, nkb, attn_body, 0)
    _write_heads(o_ref, acc_ref, tq)


def dsa_attention(iq_arr, misc_arr, misc_col, qc_arr, kc, vc, ik, *, q_row0, nb, tq_total,
                  n_keys, n_keys_pad, q_off, tq, tk, out_rows, into, name):
    nq = tq_total // tq
    qb0 = q_row0 // tq
    ksel = min(TOPK_MAX, n_keys // 4)
    qspec = lambda w, c: pl.BlockSpec((tq, w), lambda b, i, c=c: (qb0 + b * nq + i, c))
    kspec = pl.BlockSpec((n_keys_pad, LANES), lambda b, i: (b, 0))
    o_shape, extra, extra_specs = _shared_rows(into, out_rows, 512)
    n_in = 6
    kern = functools.partial(_dsa_kernel, tq=tq, tk=tk, n_keys=n_keys, q_off=q_off, ksel=ksel)
    return pl.pallas_call(
        _ignore_last_input(kern, n_in) if extra else kern,
        grid=(nb, nq),
        in_specs=[qspec(512, 0), qspec(LANES, misc_col), qspec(512, 0), kspec, kspec, kspec]
        + extra_specs,
        out_specs=qspec(512, 0),
        out_shape=o_shape,
        input_output_aliases={n_in: 0} if extra else {},
        scratch_shapes=[pltpu.VMEM((n_keys_pad // tk, tk, tq), I32),
                        pltpu.VMEM((n_keys_pad // tk, tk, tq), I16),
                        pltpu.VMEM((8, tq, LANES), BF16),
                        pltpu.VMEM((8, 1, tq), F32),
                        pltpu.VMEM((8, HEAD + ONES_ROWS, tq), F32),
                        pltpu.VMEM((1, tq), I32), pltpu.VMEM((1, tq), I32)],
        compiler_params=_cparams(("parallel", "arbitrary")),
        name=name,
    )(iq_arr, misc_arr, qc_arr, kc, vc, ik, *extra)


def _rope_tables(pos):
    half = HEAD // 2
    inv = jnp.power(ROPE_THETA, -jnp.arange(half, dtype=F32) / half)
    ang = pos.astype(F32)[:, None] * inv[None, :]
    cos, sin = jnp.cos(ang), jnp.sin(ang)
    z = jnp.zeros_like(sin)
    return (jnp.tile(cos, (1, 4)), jnp.tile(jnp.concatenate([-sin, z], 1), (1, 2)),
            jnp.tile(jnp.concatenate([z, sin], 1), (1, 2)))


def _head_mean_matrix():
    r = np.arange(512)
    return jnp.asarray((r[:, None] // HEAD == r[None, :] // HEAD).astype(np.float32), BF16)


def _pick_tile(n, prefs):
    for t in prefs:
        if n % t == 0:
            return t
    raise ValueError(f"no tile in {prefs} divides {n}")


def _pad_keys(x, n_pad):
    nb, n, w = x.shape
    if n_pad > n:
        x = jnp.concatenate([x, jnp.zeros((nb, n_pad - n, w), x.dtype)], axis=1)
    return x.reshape(nb * n_pad, w)


def kernel(x_prompt, x_sample, state_a, cache_b_k, cache_b_v, cache_b_logf, cache_c_k, cache_c_v,
           cache_c_idx, state_d, norm_mix, norm_ffn, even_w_in, even_b_f, even_q_norm, even_k_norm,
           even_w_out, odd_w_in, odd_w_gate_up, odd_b_gate, odd_q_norm, odd_k_norm, odd_o_norm,
           odd_w_out, ffn_w1, ffn_w2):
    bp, tp = x_prompt.shape[:2]
    nb, ts = x_sample.shape[:2]
    past = cache_b_k.shape[2]
    depth = norm_mix.shape[0]
    rp, rs = bp * tp, nb * ts
    rows = rp + rs
    assert bp == 1 and tp % 128 == 0 and ts % SUB == 0 and rp % ts == 0 and past % CHUNK == 0

    y = jnp.concatenate([x_prompt.reshape(rp, D_MODEL), x_sample.reshape(rs, D_MODEL)], axis=0)
    pos = jnp.concatenate([jnp.arange(tp, dtype=I32),
                           jnp.tile(past + jnp.arange(ts, dtype=I32), nb)])
    cos, slo, shi = _rope_tables(pos)
    gmat = _head_mean_matrix()

    tm = _pick_tile(rows, (512, 256, 128, 64, 32))
    tm_prep = _pick_tile(rows, (256, 128, 64, 32))
    ts_p = _pick_tile(tp, (128, 64, 32, 16))
    tq_fox = _pick_tile(tp, (256, 128))
    tk_fox = _pick_tile(tp, (512, 256, 128))
    tq_dsa = _pick_tile(tp, (256, 128))
    tk_dsa = _pick_tile(tp, (512, 256, 128))
    nk_s = past + ts
    nk_s_pad = -(-nk_s // 384) * 384
    tb_p = _pick_tile(tp, (512, 256, 128))

    lg = jnp.log1p(-jnp.exp2(-5.0 - jnp.arange(A_HEADS, dtype=F32)))
    la_ret = jnp.repeat(lg, A_DK)[None, :]
    ones_gain = jnp.ones((1, 512), F32)
    zero_state = jnp.zeros((bp, 2, 256, LANES), F32)

    a_p, a_s, bk_p, bk_s, bv_p, bv_s, bf_p, bf_s = [], [], [], [], [], [], [], []
    ck_p, ck_s, cv_p, cv_s, ci_p, ci_s, d_p, d_s = [], [], [], [], [], [], [], []

    for l in range(depth):
        i = l // 2
        if l % 2 == 0:
            w_in = jnp.concatenate(
                [even_w_in[i], jnp.zeros((D_MODEL, EV_WIDTH - even_w_in.shape[2]), F32)], 1).astype(BF16)
            h = norm_matmul(y, norm_mix[l], w_in, tm=tm, tn=EV_WIDTH, name="even_in_proj")
            bf = jnp.concatenate([even_b_f[i], jnp.zeros((LANES - B_HEADS,), F32)])[None, :]
            qka, qbn, kbn, lf = prep_even(
                h, cos, slo, shi, jnp.tile(even_q_norm[i], B_HEADS)[None, :],
                jnp.tile(even_k_norm[i], B_HEADS)[None, :], bf, gmat, tm=tm_prep)
            oa, sa_p = linear_scan(qka, 0, qka, 1, h, EV_VA // 512, h, EV_GA // 512,
                                   jnp.broadcast_to(la_ret, (ts_p, 256)), ones_gain, zero_state,
                                   row0=0, nb=bp, t=tp, ts=ts_p, la_shared=True, out_rows=rows,
                                   into=None, name="retention_prompt")
            oa, sa_s = linear_scan(qka, 0, qka, 1, h, EV_VA // 512, h, EV_GA // 512,
                                   jnp.broadcast_to(la_ret, (ts, 256)), ones_gain,
                                   _state_to_pairs(state_a[i]),
                                   row0=rp, nb=nb, t=ts, ts=ts, la_shared=True, out_rows=rows,
                                   into=oa, name="retention_decode")
            lf8 = lf[:, :B_HEADS]
            c_p = row_cumsum(lf, row0=0, nb=bp, n=tp, tb=tb_p)
            ob = fox_attention(qbn, kbn, h, EV_VB // 512, c_p,
                               q_row0=0, nb=bp, tq_total=tp, tk_total=tp, q_off=0,
                               tq=tq_fox, tk=tk_fox, out_rows=rows, into=None, name="fox_prompt")
            kb_new = kbn[rp:].reshape(nb, ts, 512)
            vb_new = h[rp:, EV_VB:EV_VB + 512].reshape(nb, ts, 512)
            lf_past = jnp.pad(cache_b_logf[i], ((0, 0), (0, 0), (0, LANES - B_HEADS)))
            lf_all = _pad_keys(jnp.concatenate([lf_past, lf[rp:].reshape(nb, ts, LANES)], 1), nk_s_pad)
            c_s = row_cumsum(lf_all, row0=0, nb=nb, n=nk_s_pad, tb=nk_s_pad // 3)
            k_all = _pad_keys(jnp.concatenate([cache_b_k[i].reshape(nb, past, 512), kb_new], 1), nk_s_pad)
            v_all = _pad_keys(jnp.concatenate([cache_b_v[i].reshape(nb, past, 512), vb_new], 1), nk_s_pad)
            ob = fox_attention(qbn, k_all, v_all, 0, c_s,
                               q_row0=rp, nb=nb, tq_total=ts, tk_total=nk_s_pad, q_off=past,
                               tq=ts, tk=nk_s_pad, out_rows=rows, into=ob, name="fox_decode")
            w_out = even_w_out[i].astype(BF16)
            y = matmul_residual(y, [oa, ob], [w_out[:512], w_out[512:]], tm=tm, name="even_out_proj")
            a_p.append(_pairs_to_state(sa_p))
            a_s.append(_pairs_to_state(sa_s))
            bk_p.append(kbn[:rp].reshape(bp, tp, B_HEADS, B_DH))
            bk_s.append(kb_new.reshape(nb, ts, B_HEADS, B_DH))
            bv_p.append(h[:rp, EV_VB:EV_VB + 512].reshape(bp, tp, B_HEADS, B_DH))
            bv_s.append(vb_new.reshape(nb, ts, B_HEADS, B_DH))
            bf_p.append(lf8[:rp].reshape(bp, tp, B_HEADS))
            bf_s.append(lf8[rp:].reshape(nb, ts, B_HEADS))
        else:
            w = odd_w_in[i]
            offs = np.cumsum([0, 512, 128, 128, 512, 64, 8, 256, 256, 512, 512, 16])
            qc_w, kc_w, vc_w, iq_w, ik_w, iw_w, qd_w, kd_w, vd_w, gd_w, gr_w = [
                w[:, int(a):int(b)] for a, b in zip(offs[:-1], offs[1:])]
            qc_w = qc_w.reshape(D_MODEL, 2, 4, C_DH).transpose(0, 2, 1, 3).reshape(D_MODEL, 512)
            w_in = jnp.concatenate(
                [qc_w, iq_w, vd_w, gd_w, qd_w, kd_w, kc_w, vc_w, ik_w, iw_w, gr_w,
                 jnp.zeros((D_MODEL, OD_WIDTH - OD_MISC - 88), F32)], 1).astype(BF16)
            h = norm_matmul(y, norm_mix[l], w_in, tm=tm, tn=OD_WIDTH, name="odd_in_proj")
            wg = jnp.zeros((LANES, 256), F32).at[MISC_GR:MISC_GR + D_GATE_RANK].set(
                odd_w_gate_up[i]).astype(BF16)
            qcr, iqr, qkd, kcr, ikr, kcb, vcb, ikb, la = prep_odd(
                h, cos, slo, shi, jnp.tile(odd_q_norm[i], C_HEADS)[None, :],
                jnp.tile(odd_k_norm[i], C_KV_HEADS)[None, :], wg, odd_b_gate[i][None, :], gmat,
                tm=tm_prep)
            oc = dsa_attention(iqr, h, OD_MISC // LANES, qcr, kcb, vcb, ikb,
                               q_row0=0, nb=bp, tq_total=tp, n_keys=tp, n_keys_pad=tp, q_off=0,
                               tq=tq_dsa, tk=tk_dsa, out_rows=rows, into=None, name="dsa_prompt")

            def with_past(cache, new):
                return _pad_keys(jnp.concatenate(
                    [cache.reshape(nb, past, -1).astype(BF16), new[rp:].reshape(nb, ts, -1)], 1), nk_s_pad)

            ik_past = jnp.concatenate([cache_c_idx[i], cache_c_idx[i]], axis=-1)
            oc = dsa_attention(iqr, h, OD_MISC // LANES, qcr,
                               with_past(cache_c_k[i], kcb), with_past(cache_c_v[i], vcb),
                               with_past(ik_past, ikb),
                               q_row0=rp, nb=nb, tq_total=ts, n_keys=nk_s, n_keys_pad=nk_s_pad,
                               q_off=past, tq=ts, tk=384, out_rows=rows, into=oc, name="dsa_decode")
            gain = jnp.tile(odd_o_norm[i], D_HEADS)[None, :]
            od, sd_p = linear_scan(qkd, 0, qkd, 1, h, OD_VD // 512, h, OD_GD // 512, la, gain,
                                   zero_state, row0=0, nb=bp, t=tp, ts=ts_p, la_shared=False,
                                   out_rows=rows, into=None, name="gla_prompt")
            od, sd_s = linear_scan(qkd, 0, qkd, 1, h, OD_VD // 512, h, OD_GD // 512, la, gain,
                                   _state_to_pairs(state_d[i]), row0=rp, nb=nb, t=ts, ts=ts,
                                   la_shared=False, out_rows=rows, into=od, name="gla_decode")
            w_out = odd_w_out[i]
            w_oc = w_out[:512].reshape(2, 4, C_DH, D_MODEL).transpose(1, 0, 2, 3).reshape(512, D_MODEL)
            y = matmul_residual(y, [oc, od], [w_oc.astype(BF16), w_out[512:].astype(BF16)], tm=tm,
                                name="odd_out_proj")
            d_p.append(_pairs_to_state(sd_p))
            d_s.append(_pairs_to_state(sd_s))
            ck_p.append(kcr[:rp].reshape(bp, tp, C_KV_HEADS, C_DH))
            ck_s.append(kcr[rp:].reshape(nb, ts, C_KV_HEADS, C_DH))
            cv_p.append(h[:rp, OD_KVC + LANES:OD_KVC + 2 * LANES].reshape(bp, tp, C_KV_HEADS, C_DH))
            cv_s.append(h[rp:, OD_KVC + LANES:OD_KVC + 2 * LANES].reshape(nb, ts, C_KV_HEADS, C_DH))
            ci_p.append(ikr[:rp, :IDX_DIM].reshape(bp, tp, IDX_DIM))
            ci_s.append(ikr[rp:, :IDX_DIM].reshape(nb, ts, IDX_DIM))
        hid = norm_matmul(y, norm_ffn[l], ffn_w1[l].astype(BF16), tm=tm, tn=D_FF, relu2=True,
                          out_dtype=BF16, name="mlp_up")
        y = matmul_residual(y, [hid], [ffn_w2[l].astype(BF16)], tm=tm, name="mlp_down")

    return (y[:rp].reshape(bp, tp, D_MODEL), y[rp:].reshape(nb, ts, D_MODEL),
            jnp.stack(a_p), jnp.stack(a_s), jnp.stack(bk_p), jnp.stack(bk_s),
            jnp.stack(bv_p), jnp.stack(bv_s), jnp.stack(bf_p), jnp.stack(bf_s),
            jnp.stack(ck_p), jnp.stack(ck_s), jnp.stack(cv_p), jnp.stack(cv_s),
            jnp.stack(ci_p), jnp.stack(ci_s), jnp.stack(d_p), jnp.stack(d_s))
```

```python
import functools

import numpy as np
import jax
import jax.numpy as jnp
from jax import lax
from jax.experimental import pallas as pl
from jax.experimental.pallas import tpu as pltpu

F32 = jnp.float32
BF16 = jnp.bfloat16
I32 = jnp.int32
I16 = jnp.int16

D_MODEL = 1024
CHUNK = 64
ROPE_THETA = 10000.0
EPS = 1e-6
A_HEADS, A_DK, A_DV = 4, 64, 128
B_HEADS, B_DH = 8, 64
C_HEADS, C_KV_HEADS, C_DH = 8, 2, 64
IDX_HEADS, IDX_DIM = 8, 64
IDX_SCALE = (IDX_HEADS * IDX_DIM) ** -0.5
TOPK_MAX = 256
D_HEADS, D_DK, D_DV = 4, 64, 128
D_GATE_RANK = 16
D_GATE_NORM = 16.0
D_FF = 4 * D_MODEL

LANES = 128
SUBLANES = 8
VMEM_LIMIT_BYTES = 56 * 1024 * 1024

HEAD = 64
SUB = 16
FOLD = 64
COUNT_FOLD = 32
ONES_ROWS = 16
NEG = -1e30
LOG2E = 1.4426950408889634
INT_MIN = -(2 ** 31)

EV_QK, EV_VA, EV_GA, EV_QB, EV_KB, EV_VB, EV_FB = 0, 512, 1024, 1536, 2048, 2560, 3072
EV_WIDTH = 3200
OD_QC, OD_IQ, OD_VD, OD_GD, OD_QKD, OD_KVC, OD_MISC = 0, 512, 1024, 1536, 2048, 2560, 2816
OD_WIDTH = 2944
MISC_IK, MISC_IW, MISC_GR = 0, 64, 72


def _cparams(sem):
    return pltpu.CompilerParams(dimension_semantics=sem, vmem_limit_bytes=VMEM_LIMIT_BYTES)


def _dot(a, b):
    return jnp.dot(a, b, preferred_element_type=F32)


def _dot_nt(a, b):
    return lax.dot_general(a, b, (((1,), (1,)), ((), ())), preferred_element_type=F32)


def _dot_tn(a, b):
    return lax.dot_general(a, b, (((0,), (0,)), ((), ())), preferred_element_type=F32)


def _split3(x):
    a1 = x.astype(BF16)
    r1 = x - a1.astype(F32)
    a2 = r1.astype(BF16)
    a3 = (r1 - a2.astype(F32)).astype(BF16)
    return a1, a2, a3


def _half_masks():
    lane = lax.broadcasted_iota(I32, (1, LANES), 1)
    lo = (lane < HEAD).astype(F32)
    return lo, 1.0 - lo


def _log_sigmoid(x):
    return jnp.minimum(x, 0.0) - jnp.log1p(jnp.exp(-jnp.abs(x)))


def _silu(x):
    return x / (1.0 + jnp.exp(-x))


def _norm_mm_kernel(x_ref, g_ref, w_ref, o_ref, *, relu2):
    x = x_ref[...]
    ms = jnp.mean(x * x, axis=-1, keepdims=True)
    xn = (x * lax.rsqrt(ms + EPS) * g_ref[...]).astype(BF16)
    y = _dot(xn, w_ref[...])
    if relu2:
        y = jnp.maximum(y, 0.0)
        y = y * y
    o_ref[...] = y.astype(o_ref.dtype)


def norm_matmul(x, g, w, *, tm, tn, relu2=False, out_dtype=F32, name):
    rows, k = x.shape
    n = w.shape[1]
    return pl.pallas_call(
        functools.partial(_norm_mm_kernel, relu2=relu2),
        grid=(rows // tm, n // tn),
        in_specs=[pl.BlockSpec((tm, k), lambda i, j: (i, 0)),
                  pl.BlockSpec((1, k), lambda i, j: (0, 0)),
                  pl.BlockSpec((k, tn), lambda i, j: (0, j))],
        out_specs=pl.BlockSpec((tm, tn), lambda i, j: (i, j)),
        out_shape=jax.ShapeDtypeStruct((rows, n), out_dtype),
        compiler_params=_cparams(("parallel", "parallel")),
        name=name,
    )(x, g.reshape(1, k), w)


def _mm_res_kernel(*refs, n_in):
    res_ref = refs[0]
    o_ref = refs[-1]
    acc = res_ref[...]
    for a_ref, w_ref in zip(refs[1:1 + n_in], refs[1 + n_in:1 + 2 * n_in]):
        acc = acc + _dot(a_ref[...].astype(BF16), w_ref[...])
    o_ref[...] = acc


def matmul_residual(res, a_list, w_list, *, tm, name):
    rows, n = res.shape
    n_in = len(a_list)
    in_specs = [pl.BlockSpec((tm, n), lambda i: (i, 0))]
    in_specs += [pl.BlockSpec((tm, a.shape[1]), lambda i: (i, 0)) for a in a_list]
    in_specs += [pl.BlockSpec(w.shape, lambda i: (0, 0)) for w in w_list]
    return pl.pallas_call(
        functools.partial(_mm_res_kernel, n_in=n_in),
        grid=(rows // tm,),
        in_specs=in_specs,
        out_specs=pl.BlockSpec((tm, n), lambda i: (i, 0)),
        out_shape=jax.ShapeDtypeStruct((rows, n), F32),
        compiler_params=_cparams(("parallel",)),
        name=name,
    )(res, *a_list, *w_list)


def _tile_lanes(t, width):
    n = width // LANES
    return t if n == 1 else jnp.concatenate([t] * n, axis=1)


def _rope(x, cos, sin_lo, sin_hi):
    w = x.shape[1]
    up = pltpu.roll(x, w - HEAD // 2, 1)
    dn = pltpu.roll(x, HEAD // 2, 1)
    return x * _tile_lanes(cos, w) + up * _tile_lanes(sin_lo, w) + dn * _tile_lanes(sin_hi, w)


def _head_rms(x, gmat):
    s = x * x
    hi = s.astype(BF16)
    lo = (s - hi.astype(F32)).astype(BF16)
    ms = (_dot(hi, gmat) + _dot(lo, gmat)) * (1.0 / HEAD)
    return x * lax.rsqrt(ms + EPS)


def _prep_even_kernel(qk_ref, qb_ref, kb_ref, fb_ref, cos_ref, slo_ref, shi_ref,
                      qg_ref, kg_ref, bf_ref, gmat_ref,
                      qka_ref, qbn_ref, kbn_ref, lf_ref):
    cos, slo, shi = cos_ref[...], slo_ref[...], shi_ref[...]
    lane = lax.broadcasted_iota(I32, (1, 4 * LANES), 1)
    kscale = jnp.where(lane < 2 * LANES, 1.0, A_DK ** -0.5)
    qka_ref[...] = _rope(qk_ref[...], cos, slo, shi) * kscale
    gmat = gmat_ref[...]
    qbn_ref[...] = _head_rms(qb_ref[...], gmat) * (qg_ref[...] * (B_DH ** -0.5 * LOG2E))
    kbn_ref[...] = _head_rms(kb_ref[...], gmat) * kg_ref[...]
    lf_ref[...] = _log_sigmoid(fb_ref[...] + bf_ref[...])


def prep_even(h, cos, slo, shi, qg, kg, bf, gmat, *, tm):
    rows = h.shape[0]
    blk = lambda w, c: pl.BlockSpec((tm, w), lambda i, c=c: (i, c))
    row = lambda w: pl.BlockSpec((1, w), lambda i: (0, 0))
    return pl.pallas_call(
        _prep_even_kernel,
        grid=(rows // tm,),
        in_specs=[blk(512, EV_QK // 512), blk(512, EV_QB // 512), blk(512, EV_KB // 512),
                  blk(LANES, EV_FB // LANES), blk(LANES, 0), blk(LANES, 0), blk(LANES, 0),
                  row(512), row(512), row(LANES),
                  pl.BlockSpec((512, 512), lambda i: (0, 0))],
        out_specs=[blk(512, 0), blk(512, 0), blk(512, 0), blk(LANES, 0)],
        out_shape=[jax.ShapeDtypeStruct((rows, 512), F32)] * 3
        + [jax.ShapeDtypeStruct((rows, LANES), F32)],
        compiler_params=_cparams(("parallel",)),
        name="prep_even",
    )(h, h, h, h, cos, slo, shi, qg, kg, bf, gmat)


def _prep_odd_kernel(qc_ref, iq_ref, qkd_ref, kvc_ref, misc_ref, cos_ref, slo_ref, shi_ref,
                     qg_ref, kg_ref, wg_ref, bg_ref, gmat_ref,
                     qcr_ref, iqr_ref, qkdo_ref, kcr_ref, ikr_ref, kcb_ref, vcb_ref, ikb_ref,
                     la_ref):
    cos, slo, shi = cos_ref[...], slo_ref[...], shi_ref[...]
    gmat = gmat_ref[...]
    qc = _head_rms(qc_ref[...], gmat) * (qg_ref[...] * (C_DH ** -0.5 * LOG2E))
    qcr_ref[...] = _rope(qc, cos, slo, shi)
    iqr_ref[...] = _rope(iq_ref[...], cos, slo, shi)
    lane = lax.broadcasted_iota(I32, (1, 4 * LANES), 1)
    qkdo_ref[...] = qkd_ref[...] * jnp.where(lane < 2 * LANES, D_DK ** -0.5, 1.0)
    kvc = kvc_ref[...]
    kc = _head_rms(kvc[:, :LANES], gmat[:LANES, :LANES]) * kg_ref[...]
    kcr = _rope(kc, cos, slo, shi)
    kcr_ref[...] = kcr
    kcb_ref[...] = kcr.astype(BF16)
    vcb_ref[...] = kvc[:, LANES:].astype(BF16)
    misc = misc_ref[...]
    ikr = _rope(misc, cos, slo, shi)
    lane1 = lax.broadcasted_iota(I32, (1, LANES), 1)
    ik2 = jnp.where(lane1 < HEAD, ikr, pltpu.roll(ikr, HEAD, 1))
    ikr_ref[...] = ik2
    ikb_ref[...] = ik2.astype(BF16)
    z = _dot(misc.astype(BF16), wg_ref[...]) + bg_ref[...]
    la_ref[...] = _log_sigmoid(z) * (1.0 / D_GATE_NORM)


def prep_odd(h, cos, slo, shi, qg, kg, wg, bg, gmat, *, tm):
    rows = h.shape[0]
    blk = lambda w, c: pl.BlockSpec((tm, w), lambda i, c=c: (i, c))
    row = lambda w: pl.BlockSpec((1, w), lambda i: (0, 0))
    return pl.pallas_call(
        _prep_odd_kernel,
        grid=(rows // tm,),
        in_specs=[blk(512, OD_QC // 512), blk(512, OD_IQ // 512), blk(512, OD_QKD // 512),
                  blk(256, OD_KVC // 256), blk(LANES, OD_MISC // LANES),
                  blk(LANES, 0), blk(LANES, 0), blk(LANES, 0),
                  row(512), row(LANES),
                  pl.BlockSpec((LANES, 256), lambda i: (0, 0)), row(256),
                  pl.BlockSpec((512, 512), lambda i: (0, 0))],
        out_specs=[blk(512, 0), blk(512, 0), blk(512, 0), blk(LANES, 0), blk(LANES, 0),
                   blk(LANES, 0), blk(LANES, 0), blk(LANES, 0), blk(256, 0)],
        out_shape=[jax.ShapeDtypeStruct((rows, 512), F32)] * 3
        + [jax.ShapeDtypeStruct((rows, LANES), F32)] * 2
        + [jax.ShapeDtypeStruct((rows, LANES), BF16)] * 3
        + [jax.ShapeDtypeStruct((rows, 256), F32)],
        compiler_params=_cparams(("parallel",)),
        name="prep_odd",
    )(h, h, h, h, h, cos, slo, shi, qg, kg, wg, bg, gmat)


def _scan_kernel(q_ref, k_ref, v_ref, g_ref, la_ref, gain_ref, s0_ref, tri_ref, bmask_ref,
                 o_ref, sfin_ref, st_ref, *, ts, nsteps):
    step = pl.program_id(1)

    @pl.when(step == 0)
    def _():
        st_ref[...] = s0_ref[0]

    q, k, v = q_ref[...], k_ref[...], v_ref[...]
    tri = tri_ref[...]
    a1, a2, a3 = _split3(la_ref[...])
    cum = _dot(tri, a1) + _dot(tri, a2) + _dot(tri, a3)
    qt = q * jnp.exp(cum)
    kt = k * jnp.exp(-cum)
    hm = _half_masks()
    bmask = bmask_ref[...]
    nsub = ts // SUB
    vb = v.astype(BF16)
    elast, upd = [], []
    for u in range(nsub):
        r0 = u * SUB
        last = cum[r0 + SUB - 1:r0 + SUB, :]
        k2 = (k[r0:r0 + SUB, :] * jnp.exp(last - cum[r0:r0 + SUB, :])).astype(BF16)
        elast.append(jnp.exp(last))
        upd.append([_dot_tn(vb[r0:r0 + SUB, 2 * p * LANES:2 * (p + 1) * LANES],
                            k2[:, p * LANES:(p + 1) * LANES]) * bmask for p in range(2)])
    seen = []
    for p in range(2):
        st = st_ref[p]
        row = []
        for u in range(nsub):
            row.append(st.astype(BF16))
            st = st * elast[u][:, p * LANES:(p + 1) * LANES] + upd[u][p]
        st_ref[p] = st
        seen.append(row)
    qtb = qt.astype(BF16)
    inter = jnp.concatenate(
        [jnp.concatenate([_dot_nt(qtb[u * SUB:(u + 1) * SUB, p * LANES:(p + 1) * LANES], seen[p][u])
                          for p in range(2)], axis=1) for u in range(nsub)], axis=0)
    ktb = kt.astype(BF16)
    same_chunk_causal = tri > 0
    intra = []
    for h in range(4):
        l0 = (h // 2) * LANES
        att = _dot_nt((qt[:, l0:l0 + LANES] * hm[h % 2]).astype(BF16), ktb[:, l0:l0 + LANES])
        att = jnp.where(same_chunk_causal, att, 0.0)
        intra.append(_dot(att.astype(BF16), vb[:, h * LANES:(h + 1) * LANES]))
    o = inter + jnp.concatenate(intra, axis=1)
    g = g_ref[...]
    gain = gain_ref[...]
    outs = []
    for h in range(4):
        oh = o[:, h * LANES:(h + 1) * LANES]
        ms = jnp.mean(oh * oh, axis=-1, keepdims=True)
        outs.append(oh * lax.rsqrt(ms + EPS) * gain[:, h * LANES:(h + 1) * LANES]
                    * _silu(g[:, h * LANES:(h + 1) * LANES]))
    o_ref[...] = jnp.concatenate(outs, axis=1)

    @pl.when(step == nsteps - 1)
    def _():
        sfin_ref[0] = st_ref[...]


def _scan_consts(ts):
    r = np.arange(ts)
    tri = ((r[:, None] // SUB == r[None, :] // SUB) & (r[None, :] <= r[:, None])).astype(np.float32)
    row = np.arange(2 * LANES)[:, None] // LANES
    col = np.arange(LANES)[None, :] // HEAD
    bmask = (row == col).astype(np.float32)
    return jnp.asarray(tri, BF16), jnp.asarray(bmask, F32)


def _ignore_last_input(fn, n_in):
    def wrapped(*refs):
        return fn(*refs[:n_in], *refs[n_in + 1:])
    return wrapped


def _shared_rows(into, rows, width):
    shape = jax.ShapeDtypeStruct((rows, width), F32)
    if into is None:
        return shape, [], []
    assert into.shape == (rows, width)
    return shape, [into], [pl.BlockSpec(memory_space=pl.ANY)]


def linear_scan(q_arr, q_col, k_arr, k_col, v_arr, v_col, g_arr, g_col, la, gain, s0,
                *, row0, nb, t, ts, la_shared, out_rows, into, name):
    nsteps = t // ts
    blk0 = row0 // ts
    tri, bmask = _scan_consts(ts)
    src = lambda w, c: pl.BlockSpec((ts, w), lambda b, s, c=c: (blk0 + b * nsteps + s, c))
    la_spec = pl.BlockSpec((ts, 256), lambda b, s: (0, 0)) if la_shared else src(256, 0)
    o_shape, extra, extra_specs = _shared_rows(into, out_rows, 512)
    n_in = 9
    kern = functools.partial(_scan_kernel, ts=ts, nsteps=nsteps)
    return pl.pallas_call(
        _ignore_last_input(kern, n_in) if extra else kern,
        grid=(nb, nsteps),
        in_specs=[src(256, q_col), src(256, k_col), src(512, v_col), src(512, g_col), la_spec,
                  pl.BlockSpec((1, 512), lambda b, s: (0, 0)),
                  pl.BlockSpec((1, 2, 256, LANES), lambda b, s: (b, 0, 0, 0)),
                  pl.BlockSpec((ts, ts), lambda b, s: (0, 0)),
                  pl.BlockSpec((256, LANES), lambda b, s: (0, 0))] + extra_specs,
        out_specs=[src(512, 0), pl.BlockSpec((1, 2, 256, LANES), lambda b, s: (b, 0, 0, 0))],
        out_shape=[o_shape, jax.ShapeDtypeStruct((nb, 2, 256, LANES), F32)],
        scratch_shapes=[pltpu.VMEM((2, 256, LANES), F32)],
        input_output_aliases={n_in: 0} if extra else {},
        compiler_params=_cparams(("parallel", "arbitrary")),
        name=name,
    )(q_arr, k_arr, v_arr, g_arr, la, gain, s0, tri, bmask, *extra)


def _state_to_pairs(s):
    nb = s.shape[0]
    st = jnp.swapaxes(s, -1, -2).reshape(nb, 2, 2, LANES, HEAD)
    z = jnp.zeros_like(st[:, :, 0])
    top = jnp.concatenate([st[:, :, 0], z], axis=-1)
    bot = jnp.concatenate([z, st[:, :, 1]], axis=-1)
    return jnp.concatenate([top, bot], axis=-2)


def _pairs_to_state(sp):
    nb = sp.shape[0]
    h0 = sp[:, :, :LANES, :HEAD]
    h1 = sp[:, :, LANES:, HEAD:]
    st = jnp.stack([h0, h1], axis=2).reshape(nb, 4, LANES, HEAD)
    return jnp.swapaxes(st, -1, -2)


def _cumsum_kernel(x_ref, tri_ref, o_ref, carry_ref):
    @pl.when(pl.program_id(1) == 0)
    def _():
        carry_ref[...] = jnp.zeros_like(carry_ref)

    tri = tri_ref[...]
    a1, a2, a3 = _split3(x_ref[...])
    c = _dot(tri, a1) + _dot(tri, a2) + _dot(tri, a3) + carry_ref[0:1, :]
    o_ref[...] = c * LOG2E
    carry_ref[...] = jnp.broadcast_to(c[-1:, :], carry_ref.shape)


def row_cumsum(x, *, row0, nb, n, tb):
    r = np.arange(tb)
    tri = jnp.asarray((r[None, :] <= r[:, None]).astype(np.float32), BF16)
    nblk = n // tb
    blk0 = row0 // tb
    return pl.pallas_call(
        _cumsum_kernel,
        grid=(nb, nblk),
        in_specs=[pl.BlockSpec((tb, LANES), lambda b, j: (blk0 + b * nblk + j, 0)),
                  pl.BlockSpec((tb, tb), lambda b, j: (0, 0))],
        out_specs=pl.BlockSpec((tb, LANES), lambda b, j: (b * nblk + j, 0)),
        out_shape=jax.ShapeDtypeStruct((nb * n, LANES), F32),
        scratch_shapes=[pltpu.VMEM((SUBLANES, LANES), F32)],
        compiler_params=_cparams(("parallel", "arbitrary")),
        name="logf_cumsum",
    )(x, tri)


def _col_reduce(x, op):
    tk, tq = x.shape
    if tk > FOLD and tk % FOLD == 0:
        x = op(x.reshape(tk // FOLD, FOLD, tq), axis=0)
    return op(x, axis=0, keepdims=True)


def _values_with_ones(vt_head):
    ones = jnp.ones((ONES_ROWS, vt_head.shape[1]), F32)
    return jnp.concatenate([vt_head, ones], axis=0).astype(BF16)


def _attn_update(s, smax, vt_ext, m_ref, acc_ref, idx):
    m_prev = m_ref[idx]
    m_new = jnp.maximum(m_prev, smax)
    alpha = jnp.exp2(m_prev - m_new)
    p = jnp.exp2(s - m_new)
    acc_ref[idx] = alpha * acc_ref[idx] + _dot(vt_ext, p.astype(BF16))
    m_ref[idx] = m_new


def _eye(n):
    r = lax.broadcasted_iota(I32, (n, n), 0)
    c = lax.broadcasted_iota(I32, (n, n), 1)
    return jnp.where(r == c, 1.0, 0.0).astype(BF16)


def _transpose_exact(x, eye):
    a1, a2, a3 = _split3(x)
    return _dot_nt(eye, a1) + _dot_nt(eye, a2) + _dot_nt(eye, a3)


def _write_heads(o_ref, acc_ref, tq):
    eye = _eye(tq)
    for c in range(4):
        a0, a1 = acc_ref[2 * c], acc_ref[2 * c + 1]
        ot = jnp.concatenate([a0[:HEAD] / a0[HEAD:HEAD + 1], a1[:HEAD] / a1[HEAD:HEAD + 1]], axis=0)
        o_ref[:, c * LANES:(c + 1) * LANES] = _transpose_exact(ot, eye)


def _fox_kernel(qi_ref, kj_ref, q_ref, k_ref, v_ref, c_ref, o_ref, m_ref, acc_ref,
                *, tq, tk, q_off):
    s_id = pl.program_id(1)
    qi = qi_ref[s_id]
    kj = kj_ref[s_id]
    j_last = (q_off + (qi + 1) * tq - 1) // tk

    @pl.when(kj == 0)
    def _():
        m_ref[...] = jnp.full_like(m_ref, NEG)
        acc_ref[...] = jnp.zeros_like(acc_ref)

    def block(masked):
        visible = None
        if masked:
            kpos = kj * tk + lax.broadcasted_iota(I32, (tk, tq), 0)
            qpos = q_off + qi * tq + lax.broadcasted_iota(I32, (tk, tq), 1)
            visible = kpos <= qpos
        v = v_ref[...]
        vts = [v[:, p * LANES:(p + 1) * LANES].T for p in range(4)]
        _fox_block(q_ref[...], k_ref[...], vts, c_ref[...], visible, m_ref, acc_ref)

    @pl.when(kj < j_last)
    def _():
        block(False)

    @pl.when(kj == j_last)
    def _():
        block(True)
        _write_heads(o_ref, acc_ref, tq)


def _fox_block(q, k, vts, c, visible, m_ref, acc_ref):
    hm = _half_masks()
    scores, maxima = [], []
    for h in range(B_HEADS):
        l0 = (h // 2) * LANES
        qm = (q[:, l0:l0 + LANES] * hm[h % 2]).astype(BF16)
        s = _dot_nt(k[:, l0:l0 + LANES].astype(BF16), qm) - c[:, h:h + 1]
        if visible is not None:
            s = jnp.where(visible, s, NEG)
        scores.append(s)
        maxima.append(_col_reduce(s, jnp.max))
    for h in range(B_HEADS):
        hh = h % 2
        _attn_update(scores[h], maxima[h], _values_with_ones(vts[h // 2][hh * HEAD:(hh + 1) * HEAD]),
                     m_ref, acc_ref, h)


def _fox_decode_kernel(q_ref, kp_ref, vp_ref, lp_ref, kn_ref, vn_ref, ln_ref, tri_ref, o_ref,
                       m_ref, acc_ref, *, ts, past, tb):
    m_ref[...] = jnp.full_like(m_ref, NEG)
    acc_ref[...] = jnp.zeros_like(acc_ref)
    eye = _eye(LANES)
    tri = tri_ref[...]

    def cumsum(x, tri_blk, carry):
        a1, a2, a3 = _split3(x)
        return _dot(tri_blk, a1) + _dot(tri_blk, a2) + _dot(tri_blk, a3) + carry

    def transposed_values(v):
        vb = v.astype(BF16)
        return [_dot_nt(eye, vb[:, p * LANES:(p + 1) * LANES]) for p in range(4)]

    lp = lp_ref[...]
    carry = jnp.zeros((1, lp.shape[1]), F32)
    c_blocks = []
    for j in range(past // tb):
        cb = cumsum(lp[j * tb:(j + 1) * tb], tri, carry)
        carry = cb[-1:, :]
        c_blocks.append(cb)
    c_past = jnp.concatenate(c_blocks, axis=0) * LOG2E
    c_new = cumsum(ln_ref[:, :B_HEADS], tri[:ts, :ts], carry) * LOG2E
    q = q_ref[...]
    _fox_block(q, kp_ref[...], transposed_values(vp_ref[...]), c_past, None, m_ref, acc_ref)
    kpos = lax.broadcasted_iota(I32, (ts, ts), 0)
    qpos = lax.broadcasted_iota(I32, (ts, ts), 1)
    _fox_block(q, kn_ref[...], transposed_values(vn_ref[...]), c_new, kpos <= qpos, m_ref, acc_ref)
    _write_heads(o_ref, acc_ref, ts)


def fox_decode(q_arr, k_past, v_past, lf_past, k_new, v_new, v_col, lf_new, *, row0, nb, ts, past,
               out_rows, into, name):
    tb = _pick_tile(past, (256, 128, 64))
    r = np.arange(tb)
    tri = jnp.asarray((r[None, :] <= r[:, None]).astype(np.float32), BF16)
    blk0 = row0 // ts
    new = lambda w, c: pl.BlockSpec((ts, w), lambda b, c=c: (blk0 + b, c))
    old = lambda w: pl.BlockSpec((past, w), lambda b: (b, 0))
    o_shape, extra, extra_specs = _shared_rows(into, out_rows, 512)
    n_in = 8
    kern = functools.partial(_fox_decode_kernel, ts=ts, past=past, tb=tb)
    return pl.pallas_call(
        _ignore_last_input(kern, n_in) if extra else kern,
        grid=(nb,),
        in_specs=[new(512, 0), old(512), old(512), old(B_HEADS), new(512, 0), new(512, v_col),
                  new(LANES, 0), pl.BlockSpec((tb, tb), lambda b: (0, 0))] + extra_specs,
        out_specs=new(512, 0),
        out_shape=o_shape,
        scratch_shapes=[pltpu.VMEM((8, 1, ts), F32), pltpu.VMEM((8, HEAD + ONES_ROWS, ts), F32)],
        input_output_aliases={n_in: 0} if extra else {},
        compiler_params=_cparams(("parallel",)),
        name=name,
    )(q_arr, k_past, v_past, lf_past, k_new, v_new, lf_new, tri, *extra)


def fox_attention(q_arr, k_arr, v_arr, v_col, c_arr, *, q_row0, nb, tq_total, tk_total, q_off,
                  tq, tk, out_rows, into, name):
    nq = tq_total // tq
    steps = [(i, j) for i in range(nq) for j in range((q_off + (i + 1) * tq - 1) // tk + 1)]
    qi = jnp.asarray([s[0] for s in steps], I32)
    kj = jnp.asarray([s[1] for s in steps], I32)
    qb0 = q_row0 // tq
    nkb = tk_total // tk
    o_shape, extra, extra_specs = _shared_rows(into, out_rows, 512)
    n_in = 6
    grid_spec = pltpu.PrefetchScalarGridSpec(
        num_scalar_prefetch=2,
        grid=(nb, len(steps)),
        in_specs=[
            pl.BlockSpec((tq, 512), lambda b, s, qi, kj: (qb0 + b * nq + qi[s], 0)),
            pl.BlockSpec((tk, 512), lambda b, s, qi, kj: (b * nkb + kj[s], 0)),
            pl.BlockSpec((tk, 512), lambda b, s, qi, kj: (b * nkb + kj[s], v_col)),
            pl.BlockSpec((tk, LANES), lambda b, s, qi, kj: (b * nkb + kj[s], 0)),
        ] + extra_specs,
        out_specs=pl.BlockSpec((tq, 512), lambda b, s, qi, kj: (qb0 + b * nq + qi[s], 0)),
        scratch_shapes=[pltpu.VMEM((8, 1, tq), F32), pltpu.VMEM((8, HEAD + ONES_ROWS, tq), F32)],
    )
    kern = functools.partial(_fox_kernel, tq=tq, tk=tk, q_off=q_off)
    return pl.pallas_call(
        _ignore_last_input(kern, n_in) if extra else kern,
        grid_spec=grid_spec,
        out_shape=o_shape,
        input_output_aliases={n_in: 0} if extra else {},
        compiler_params=_cparams(("parallel", "arbitrary")),
        name=name,
    )(qi, kj, q_arr, k_arr, v_arr, c_arr, *extra)


def _sortable_key(x):
    b = lax.bitcast_convert_type(x + 0.0, I32)
    return jnp.where(b < 0, b ^ jnp.int32(0x7FFFFFFF), b)


def _dsa_kernel(iq_ref, misc_ref, qc_ref, kc_ref, vc_ref, ik_ref, o_ref,
                keys_ref, half_ref, qm_ref, m_ref, acc_ref, thr_ref, jcut_ref,
                *, tq, tk, n_keys, q_off, ksel):
    i = pl.program_id(1)
    hm = _half_masks()
    qcol = lax.broadcasted_iota(I32, (1, tq), 1)
    lim = jnp.minimum(((q_off + i * tq + qcol) // CHUNK + 1) * CHUNK, n_keys)
    lim_max = jnp.minimum(((q_off + (i + 1) * tq - 1) // CHUNK + 1) * CHUNK, n_keys)
    nkb = (lim_max + tk - 1) // tk
    eye = _eye(LANES)

    def key_pos(kb):
        return kb * tk + lax.broadcasted_iota(I32, (tk, tq), 0)

    iq = iq_ref[...]
    for hd in range(IDX_HEADS):
        c0 = (hd // 2) * LANES
        qm_ref[hd] = (iq[:, c0:c0 + LANES] * hm[hd % 2]).astype(BF16)
    iwt = _transpose_exact(misc_ref[...], eye) * IDX_SCALE

    def score_body(kb, carry):
        k0 = pl.multiple_of(kb * tk, tk)
        ikb = ik_ref[pl.ds(k0, tk), :]
        sc = jnp.zeros((tk, tq), F32)
        for hd in range(IDX_HEADS):
            a = _dot_nt(ikb, qm_ref[hd])
            sc = sc + jnp.maximum(a, 0.0) * iwt[MISC_IW + hd:MISC_IW + hd + 1, :]
        key = jnp.where(key_pos(kb) < lim, _sortable_key(sc), INT_MIN)
        keys_ref[kb] = key
        half_ref[kb] = jnp.right_shift(key, 16).astype(I16)
        return carry

    lax.fori_loop(0, nkb, score_body, 0)

    def count(pred):
        def body(kb, acc):
            hit = jnp.where(pred(keys_ref[kb], key_pos(kb)), 1.0, 0.0)
            return acc + jnp.sum(hit.reshape(tk // COUNT_FOLD, COUNT_FOLD, tq), axis=0)
        acc = lax.fori_loop(0, nkb, body, jnp.zeros((COUNT_FOLD, tq), F32))
        return jnp.sum(acc, axis=0, keepdims=True)

    def count_half(cand):
        def body(kb, acc):
            hit = jnp.where(half_ref[kb] >= cand, jnp.int16(1), jnp.int16(0))
            parts = [hit[r * COUNT_FOLD:(r + 1) * COUNT_FOLD] for r in range(tk // COUNT_FOLD)]
            while len(parts) > 1:
                parts = [a + b for a, b in zip(parts[0::2], parts[1::2])] + parts[len(parts) & ~1:]
            return acc + parts[0]
        acc = lax.fori_loop(0, nkb, body, jnp.zeros((COUNT_FOLD, tq), I16))
        return jnp.sum(acc.astype(F32), axis=0, keepdims=True)

    def any_row(flag):
        return jnp.max(jnp.where(flag, 1.0, 0.0)) > 0.0

    thr_ref[...] = jnp.full((1, tq), INT_MIN + 1, I32)
    jcut_ref[...] = jnp.full((1, tq), 2 ** 30, I32)
    kf = float(ksel)

    @pl.when(lim_max > ksel)
    def _():
        few = lim <= ksel

        def unsettled(cres):
            return jnp.logical_and(cres != kf, jnp.logical_not(few))

        def search(first, last, group, to_half, state):
            assert (last - first) % group == 0

            def cond(state):
                t, _, _, go = state
                return jnp.logical_and(t < last, go)

            def body(state):
                t, res, cres, _ = state
                for g in range(group):
                    cand = res + jnp.left_shift(jnp.int32(1), 30 - g - t)
                    cnt = count_half(to_half(cand))
                    take = cnt >= kf
                    res = jnp.where(take, cand, res)
                    cres = jnp.where(take, cnt, cres)
                return t + group, res, cres, any_row(unsettled(cres))

            return lax.while_loop(cond, body, (jnp.int32(first),) + state)[1:]

        def high_half(x):
            return jnp.right_shift(x, 16).astype(I16)

        def low_half(x):
            return ((x & 0xFFFF) - 32768).astype(I16)

        c0 = count_half(jnp.zeros((1, tq), I16))
        res = jnp.where(c0 >= kf, 0, INT_MIN).astype(I32)
        cres = jnp.where(c0 >= kf, c0, 2.0 * n_keys)
        res, cres, go = search(0, 15, 3, high_half, (res, cres, any_row(unsettled(cres))))

        @pl.when(go)
        def _():
            rh = high_half(res)

            def relabel(kb, carry):
                hi = half_ref[kb]
                lo = low_half(keys_ref[kb])
                half_ref[kb] = jnp.where(hi > rh, jnp.int16(32767),
                                         jnp.where(hi < rh, jnp.int16(-32768), lo))
                return carry

            lax.fori_loop(0, nkb, relabel, 0)

        res, cres, open_rows = search(15, 31, 4, low_half, (res, cres, go))
        thr = jnp.maximum(res, INT_MIN + 1)
        thr_ref[...] = thr

        @pl.when(open_rows)
        def _():
            need = kf - count(lambda blk, kpos: blk > thr)
            n_eq = count(lambda blk, kpos: blk == thr)
            split = n_eq > need

            @pl.when(any_row(split))
            def _():
                def idx_body(t, lo):
                    cand = lo + jnp.left_shift(jnp.int32(1), 14 - t)
                    cnt = count(lambda blk, kpos: (blk == thr) & (kpos < cand))
                    return jnp.where(cnt < need, cand, lo)

                lo = lax.fori_loop(0, 15, idx_body, jnp.zeros((1, tq), I32))
                jcut_ref[...] = jnp.where(split, lo, 2 ** 30)

    m_ref[...] = jnp.full_like(m_ref, NEG)
    acc_ref[...] = jnp.zeros_like(acc_ref)
    qc = qc_ref[...]
    for hd in range(C_HEADS):
        c0 = (hd // 2) * LANES
        qm_ref[hd] = (qc[:, c0:c0 + LANES] * hm[hd % 2]).astype(BF16)
    thr = thr_ref[...]
    jcut = jcut_ref[...]

    def attn_body(kb, carry):
        k0 = pl.multiple_of(kb * tk, tk)
        kblk = kc_ref[pl.ds(k0, tk), :]
        vt = _dot_nt(eye, vc_ref[pl.ds(k0, tk), :])
        vts = [_values_with_ones(vt[hh * HEAD:(hh + 1) * HEAD]) for hh in range(2)]
        keys = keys_ref[kb]
        sel = (keys > thr) | ((keys == thr) & (key_pos(kb) <= jcut))
        scores = [jnp.where(sel, _dot_nt(kblk, qm_ref[hd]), NEG) for hd in range(C_HEADS)]
        maxima = [_col_reduce(s, jnp.max) for s in scores]
        for hd in range(C_HEADS):
            _attn_update(scores[hd], maxima[hd], vts[hd % 2], m_ref, acc_ref, hd)
        return carry

    lax.fori_loop(0, nkb, attn_body, 0)
    _write_heads(o_ref, acc_ref, tq)


def dsa_attention(iq_arr, misc_arr, misc_col, qc_arr, kc, vc, ik, *, q_row0, nb, tq_total,
                  n_keys, n_keys_pad, q_off, tq, tk, out_rows, into, name):
    nq = tq_total // tq
    qb0 = q_row0 // tq
    ksel = min(TOPK_MAX, n_keys // 4)
    qspec = lambda w, c: pl.BlockSpec((tq, w), lambda b, i, c=c: (qb0 + b * nq + i, c))
    kspec = pl.BlockSpec((n_keys_pad, LANES), lambda b, i: (b, 0))
    o_shape, extra, extra_specs = _shared_rows(into, out_rows, 512)
    n_in = 6
    kern = functools.partial(_dsa_kernel, tq=tq, tk=tk, n_keys=n_keys, q_off=q_off, ksel=ksel)
    return pl.pallas_call(
        _ignore_last_input(kern, n_in) if extra else kern,
        grid=(nb, nq),
        in_specs=[qspec(512, 0), qspec(LANES, misc_col), qspec(512, 0), kspec, kspec, kspec]
        + extra_specs,
        out_specs=qspec(512, 0),
        out_shape=o_shape,
        input_output_aliases={n_in: 0} if extra else {},
        scratch_shapes=[pltpu.VMEM((n_keys_pad // tk, tk, tq), I32),
                        pltpu.VMEM((n_keys_pad // tk, tk, tq), I16),
                        pltpu.VMEM((8, tq, LANES), BF16),
                        pltpu.VMEM((8, 1, tq), F32),
                        pltpu.VMEM((8, HEAD + ONES_ROWS, tq), F32),
                        pltpu.VMEM((1, tq), I32), pltpu.VMEM((1, tq), I32)],
        compiler_params=_cparams(("parallel", "arbitrary")),
        name=name,
    )(iq_arr, misc_arr, qc_arr, kc, vc, ik, *extra)


def _rope_tables(pos):
    half = HEAD // 2
    inv = jnp.power(ROPE_THETA, -jnp.arange(half, dtype=F32) / half)
    ang = pos.astype(F32)[:, None] * inv[None, :]
    cos, sin = jnp.cos(ang), jnp.sin(ang)
    z = jnp.zeros_like(sin)
    return (jnp.tile(cos, (1, 4)), jnp.tile(jnp.concatenate([-sin, z], 1), (1, 2)),
            jnp.tile(jnp.concatenate([z, sin], 1), (1, 2)))


def _head_mean_matrix():
    r = np.arange(512)
    return jnp.asarray((r[:, None] // HEAD == r[None, :] // HEAD).astype(np.float32), BF16)


def _pick_tile(n, prefs):
    for t in prefs:
        if n % t == 0:
            return t
    raise ValueError(f"no tile in {prefs} divides {n}")


def _pad_keys(x, n_pad):
    nb, n, w = x.shape
    if n_pad > n:
        x = jnp.concatenate([x, jnp.zeros((nb, n_pad - n, w), x.dtype)], axis=1)
    return x.reshape(nb * n_pad, w)


def kernel(x_prompt, x_sample, state_a, cache_b_k, cache_b_v, cache_b_logf, cache_c_k, cache_c_v,
           cache_c_idx, state_d, norm_mix, norm_ffn, even_w_in, even_b_f, even_q_norm, even_k_norm,
           even_w_out, odd_w_in, odd_w_gate_up, odd_b_gate, odd_q_norm, odd_k_norm, odd_o_norm,
           odd_w_out, ffn_w1, ffn_w2):
    bp, tp = x_prompt.shape[:2]
    nb, ts = x_sample.shape[:2]
    past = cache_b_k.shape[2]
    depth = norm_mix.shape[0]
    rp, rs = bp * tp, nb * ts
    rows = rp + rs
    assert bp == 1 and tp % 128 == 0 and ts % SUB == 0 and rp % ts == 0 and past % CHUNK == 0

    y = jnp.concatenate([x_prompt.reshape(rp, D_MODEL), x_sample.reshape(rs, D_MODEL)], axis=0)
    pos = jnp.concatenate([jnp.arange(tp, dtype=I32),
                           jnp.tile(past + jnp.arange(ts, dtype=I32), nb)])
    cos, slo, shi = _rope_tables(pos)
    gmat = _head_mean_matrix()

    tm = _pick_tile(rows, (512, 256, 128, 64, 32))
    tm_prep = _pick_tile(rows, (256, 128, 64, 32))
    ts_p = _pick_tile(tp, (128, 64, 32, 16))
    tq_fox = _pick_tile(tp, (512, 256, 128))
    tk_fox = _pick_tile(tp, (512, 256, 128))
    tq_dsa = _pick_tile(tp, (256, 128))
    tk_dsa = _pick_tile(tp, (512, 256, 128))
    nk_s = past + ts
    nk_s_pad = -(-nk_s // 384) * 384
    tb_p = _pick_tile(tp, (512, 256, 128))

    lg = jnp.log1p(-jnp.exp2(-5.0 - jnp.arange(A_HEADS, dtype=F32)))
    la_ret = jnp.repeat(lg, A_DK)[None, :]
    ones_gain = jnp.ones((1, 512), F32)
    zero_state = jnp.zeros((bp, 2, 256, LANES), F32)

    a_p, a_s, bk_p, bk_s, bv_p, bv_s, bf_p, bf_s = [], [], [], [], [], [], [], []
    ck_p, ck_s, cv_p, cv_s, ci_p, ci_s, d_p, d_s = [], [], [], [], [], [], [], []

    for l in range(depth):
        i = l // 2
        if l % 2 == 0:
            w_in = jnp.concatenate(
                [even_w_in[i], jnp.zeros((D_MODEL, EV_WIDTH - even_w_in.shape[2]), F32)], 1).astype(BF16)
            h = norm_matmul(y, norm_mix[l], w_in, tm=tm, tn=EV_WIDTH, name="even_in_proj")
            bf = jnp.concatenate([even_b_f[i], jnp.zeros((LANES - B_HEADS,), F32)])[None, :]
            qka, qbn, kbn, lf = prep_even(
                h, cos, slo, shi, jnp.tile(even_q_norm[i], B_HEADS)[None, :],
                jnp.tile(even_k_norm[i], B_HEADS)[None, :], bf, gmat, tm=tm_prep)
            oa, sa_p = linear_scan(qka, 0, qka, 1, h, EV_VA // 512, h, EV_GA // 512,
                                   jnp.broadcast_to(la_ret, (ts_p, 256)), ones_gain, zero_state,
                                   row0=0, nb=bp, t=tp, ts=ts_p, la_shared=True, out_rows=rows,
                                   into=None, name="retention_prompt")
            oa, sa_s = linear_scan(qka, 0, qka, 1, h, EV_VA // 512, h, EV_GA // 512,
                                   jnp.broadcast_to(la_ret, (ts, 256)), ones_gain,
                                   _state_to_pairs(state_a[i]),
                                   row0=rp, nb=nb, t=ts, ts=ts, la_shared=True, out_rows=rows,
                                   into=oa, name="retention_decode")
            lf8 = lf[:, :B_HEADS]
            c_p = row_cumsum(lf, row0=0, nb=bp, n=tp, tb=tb_p)
            ob = fox_attention(qbn, kbn, h, EV_VB // 512, c_p,
                               q_row0=0, nb=bp, tq_total=tp, tk_total=tp, q_off=0,
                               tq=tq_fox, tk=tk_fox, out_rows=rows, into=None, name="fox_prompt")
            kb_new = kbn[rp:].reshape(nb, ts, 512)
            vb_new = h[rp:, EV_VB:EV_VB + 512].reshape(nb, ts, 512)
            ob = fox_decode(qbn, cache_b_k[i].reshape(nb * past, 512),
                            cache_b_v[i].reshape(nb * past, 512),
                            cache_b_logf[i].reshape(nb * past, B_HEADS), kbn, h, EV_VB // 512, lf,
                            row0=rp, nb=nb, ts=ts, past=past, out_rows=rows, into=ob,
                            name="fox_decode")
            w_out = even_w_out[i].astype(BF16)
            y = matmul_residual(y, [oa, ob], [w_out[:512], w_out[512:]], tm=tm, name="even_out_proj")
            a_p.append(_pairs_to_state(sa_p))
            a_s.append(_pairs_to_state(sa_s))
            bk_p.append(kbn[:rp].reshape(bp, tp, B_HEADS, B_DH))
            bk_s.append(kb_new.reshape(nb, ts, B_HEADS, B_DH))
            bv_p.append(h[:rp, EV_VB:EV_VB + 512].reshape(bp, tp, B_HEADS, B_DH))
            bv_s.append(vb_new.reshape(nb, ts, B_HEADS, B_DH))
            bf_p.append(lf8[:rp].reshape(bp, tp, B_HEADS))
            bf_s.append(lf8[rp:].reshape(nb, ts, B_HEADS))
        else:
            w = odd_w_in[i]
            offs = np.cumsum([0, 512, 128, 128, 512, 64, 8, 256, 256, 512, 512, 16])
            qc_w, kc_w, vc_w, iq_w, ik_w, iw_w, qd_w, kd_w, vd_w, gd_w, gr_w = [
                w[:, int(a):int(b)] for a, b in zip(offs[:-1], offs[1:])]
            qc_w = qc_w.reshape(D_MODEL, 2, 4, C_DH).transpose(0, 2, 1, 3).reshape(D_MODEL, 512)
            w_in = jnp.concatenate(
                [qc_w, iq_w, vd_w, gd_w, qd_w, kd_w, kc_w, vc_w, ik_w, iw_w, gr_w,
                 jnp.zeros((D_MODEL, OD_WIDTH - OD_MISC - 88), F32)], 1).astype(BF16)
            h = norm_matmul(y, norm_mix[l], w_in, tm=tm, tn=OD_WIDTH, name="odd_in_proj")
            wg = jnp.zeros((LANES, 256), F32).at[MISC_GR:MISC_GR + D_GATE_RANK].set(
                odd_w_gate_up[i]).astype(BF16)
            qcr, iqr, qkd, kcr, ikr, kcb, vcb, ikb, la = prep_odd(
                h, cos, slo, shi, jnp.tile(odd_q_norm[i], C_HEADS)[None, :],
                jnp.tile(odd_k_norm[i], C_KV_HEADS)[None, :], wg, odd_b_gate[i][None, :], gmat,
                tm=tm_prep)
            oc = dsa_attention(iqr, h, OD_MISC // LANES, qcr, kcb, vcb, ikb,
                               q_row0=0, nb=bp, tq_total=tp, n_keys=tp, n_keys_pad=tp, q_off=0,
                               tq=tq_dsa, tk=tk_dsa, out_rows=rows, into=None, name="dsa_prompt")

            def with_past(cache, new):
                return _pad_keys(jnp.concatenate(
                    [cache.reshape(nb, past, -1).astype(BF16), new[rp:].reshape(nb, ts, -1)], 1), nk_s_pad)

            ik_past = jnp.concatenate([cache_c_idx[i], cache_c_idx[i]], axis=-1)
            oc = dsa_attention(iqr, h, OD_MISC // LANES, qcr,
                               with_past(cache_c_k[i], kcb), with_past(cache_c_v[i], vcb),
                               with_past(ik_past, ikb),
                               q_row0=rp, nb=nb, tq_total=ts, n_keys=nk_s, n_keys_pad=nk_s_pad,
                               q_off=past, tq=ts, tk=384, out_rows=rows, into=oc, name="dsa_decode")
            gain = jnp.tile(odd_o_norm[i], D_HEADS)[None, :]
            od, sd_p = linear_scan(qkd, 0, qkd, 1, h, OD_VD // 512, h, OD_GD // 512, la, gain,
                                   zero_state, row0=0, nb=bp, t=tp, ts=ts_p, la_shared=False,
                                   out_rows=rows, into=None, name="gla_prompt")
            od, sd_s = linear_scan(qkd, 0, qkd, 1, h, OD_VD // 512, h, OD_GD // 512, la, gain,
                                   _state_to_pairs(state_d[i]), row0=rp, nb=nb, t=ts, ts=ts,
                                   la_shared=False, out_rows=rows, into=od, name="gla_decode")
            w_out = odd_w_out[i]
            w_oc = w_out[:512].reshape(2, 4, C_DH, D_MODEL).transpose(1, 0, 2, 3).reshape(512, D_MODEL)
            y = matmul_residual(y, [oc, od], [w_oc.astype(BF16), w_out[512:].astype(BF16)], tm=tm,
                                name="odd_out_proj")
            d_p.append(_pairs_to_state(sd_p))
            d_s.append(_pairs_to_state(sd_s))
            ck_p.append(kcr[:rp].reshape(bp, tp, C_KV_HEADS, C_DH))
            ck_s.append(kcr[rp:].reshape(nb, ts, C_KV_HEADS, C_DH))
            cv_p.append(h[:rp, OD_KVC + LANES:OD_KVC + 2 * LANES].reshape(bp, tp, C_KV_HEADS, C_DH))
            cv_s.append(h[rp:, OD_KVC + LANES:OD_KVC + 2 * LANES].reshape(nb, ts, C_KV_HEADS, C_DH))
            ci_p.append(ikr[:rp, :IDX_DIM].reshape(bp, tp, IDX_DIM))
            ci_s.append(ikr[rp:, :IDX_DIM].reshape(nb, ts, IDX_DIM))
        hid = norm_matmul(y, norm_ffn[l], ffn_w1[l].astype(BF16), tm=tm, tn=D_FF, relu2=True,
                          out_dtype=BF16, name="mlp_up")
        y = matmul_residual(y, [hid], [ffn_w2[l].astype(BF16)], tm=tm, name="mlp_down")

    return (y[:rp].reshape(bp, tp, D_MODEL), y[rp:].reshape(nb, ts, D_MODEL),
            jnp.stack(a_p), jnp.stack(a_s), jnp.stack(bk_p), jnp.stack(bk_s),
            jnp.stack(bv_p), jnp.stack(bv_s), jnp.stack(bf_p), jnp.stack(bf_s),
            jnp.stack(ck_p), jnp.stack(ck_s), jnp.stack(cv_p), jnp.stack(cv_s),
            jnp.stack(ci_p), jnp.stack(ci_s), jnp.stack(d_p), jnp.stack(d_s))
```

```python
import functools

import numpy as np
import jax
import jax.numpy as jnp
from jax import lax
from jax.experimental import pallas as pl
from jax.experimental.pallas import tpu as pltpu

F32 = jnp.float32
BF16 = jnp.bfloat16
I32 = jnp.int32
I16 = jnp.int16

D_MODEL = 1024
CHUNK = 64
ROPE_THETA = 10000.0
EPS = 1e-6
A_HEADS, A_DK, A_DV = 4, 64, 128
B_HEADS, B_DH = 8, 64
C_HEADS, C_KV_HEADS, C_DH = 8, 2, 64
IDX_HEADS, IDX_DIM = 8, 64
IDX_SCALE = (IDX_HEADS * IDX_DIM) ** -0.5
TOPK_MAX = 256
D_HEADS, D_DK, D_DV = 4, 64, 128
D_GATE_RANK = 16
D_GATE_NORM = 16.0
D_FF = 4 * D_MODEL

LANES = 128
SUBLANES = 8
VMEM_LIMIT_BYTES = 56 * 1024 * 1024

HEAD = 64
SUB = 16
FOLD = 64
COUNT_FOLD = 32
ONES_ROWS = 16
NEG = -1e30
LOG2E = 1.4426950408889634
BOUND_SLACK = 1.03
DENOM_FLOOR = 2.0 ** -100
FOX_BOUND_RANGE = 90.0
INT_MIN = -(2 ** 31)

EV_QK, EV_VA, EV_GA, EV_QB, EV_KB, EV_VB, EV_FB = 0, 512, 1024, 1536, 2048, 2560, 3072
EV_WIDTH = 3200
OD_QC, OD_IQ, OD_VD, OD_GD, OD_QKD, OD_KVC, OD_MISC = 0, 512, 1024, 1536, 2048, 2560, 2816
OD_WIDTH = 2944
MISC_IK, MISC_IW, MISC_GR = 0, 64, 72


def _cparams(sem):
    return pltpu.CompilerParams(dimension_semantics=sem, vmem_limit_bytes=VMEM_LIMIT_BYTES)


def _dot(a, b):
    return jnp.dot(a, b, preferred_element_type=F32)


def _dot_nt(a, b):
    return lax.dot_general(a, b, (((1,), (1,)), ((), ())), preferred_element_type=F32)


def _dot_tn(a, b):
    return lax.dot_general(a, b, (((0,), (0,)), ((), ())), preferred_element_type=F32)


def _split3(x):
    a1 = x.astype(BF16)
    r1 = x - a1.astype(F32)
    a2 = r1.astype(BF16)
    a3 = (r1 - a2.astype(F32)).astype(BF16)
    return a1, a2, a3


def _half_masks():
    lane = lax.broadcasted_iota(I32, (1, LANES), 1)
    lo = (lane < HEAD).astype(F32)
    return lo, 1.0 - lo


def _log_sigmoid(x):
    return jnp.minimum(x, 0.0) - jnp.log1p(jnp.exp(-jnp.abs(x)))


def _silu(x):
    return x / (1.0 + jnp.exp(-x))


def _norm_mm_kernel(x_ref, g_ref, w_ref, o_ref, *, relu2):
    x = x_ref[...]
    ms = jnp.mean(x * x, axis=-1, keepdims=True)
    xn = (x * lax.rsqrt(ms + EPS) * g_ref[...]).astype(BF16)
    y = _dot(xn, w_ref[...])
    if relu2:
        y = jnp.maximum(y, 0.0)
        y = y * y
    o_ref[...] = y.astype(o_ref.dtype)


def norm_matmul(x, g, w, *, tm, tn, relu2=False, out_dtype=F32, name):
    rows, k = x.shape
    n = w.shape[1]
    return pl.pallas_call(
        functools.partial(_norm_mm_kernel, relu2=relu2),
        grid=(rows // tm, n // tn),
        in_specs=[pl.BlockSpec((tm, k), lambda i, j: (i, 0)),
                  pl.BlockSpec((1, k), lambda i, j: (0, 0)),
                  pl.BlockSpec((k, tn), lambda i, j: (0, j))],
        out_specs=pl.BlockSpec((tm, tn), lambda i, j: (i, j)),
        out_shape=jax.ShapeDtypeStruct((rows, n), out_dtype),
        compiler_params=_cparams(("parallel", "parallel")),
        name=name,
    )(x, g.reshape(1, k), w)


def _mm_res_kernel(*refs, n_in):
    res_ref = refs[0]
    o_ref = refs[-1]
    acc = res_ref[...]
    for a_ref, w_ref in zip(refs[1:1 + n_in], refs[1 + n_in:1 + 2 * n_in]):
        acc = acc + _dot(a_ref[...].astype(BF16), w_ref[...])
    o_ref[...] = acc


def matmul_residual(res, a_list, w_list, *, tm, name):
    rows, n = res.shape
    n_in = len(a_list)
    in_specs = [pl.BlockSpec((tm, n), lambda i: (i, 0))]
    in_specs += [pl.BlockSpec((tm, a.shape[1]), lambda i: (i, 0)) for a in a_list]
    in_specs += [pl.BlockSpec(w.shape, lambda i: (0, 0)) for w in w_list]
    return pl.pallas_call(
        functools.partial(_mm_res_kernel, n_in=n_in),
        grid=(rows // tm,),
        in_specs=in_specs,
        out_specs=pl.BlockSpec((tm, n), lambda i: (i, 0)),
        out_shape=jax.ShapeDtypeStruct((rows, n), F32),
        compiler_params=_cparams(("parallel",)),
        name=name,
    )(res, *a_list, *w_list)


def _tile_lanes(t, width):
    n = width // LANES
    return t if n == 1 else jnp.concatenate([t] * n, axis=1)


def _rope(x, cos, sin_lo, sin_hi):
    w = x.shape[1]
    up = pltpu.roll(x, w - HEAD // 2, 1)
    dn = pltpu.roll(x, HEAD // 2, 1)
    return x * _tile_lanes(cos, w) + up * _tile_lanes(sin_lo, w) + dn * _tile_lanes(sin_hi, w)


def _head_rms(x, gmat):
    s = x * x
    hi = s.astype(BF16)
    lo = (s - hi.astype(F32)).astype(BF16)
    ms = (_dot(hi, gmat) + _dot(lo, gmat)) * (1.0 / HEAD)
    return x * lax.rsqrt(ms + EPS)


def _prep_even_kernel(qk_ref, qb_ref, kb_ref, fb_ref, cos_ref, slo_ref, shi_ref,
                      qg_ref, kg_ref, bf_ref, gmat_ref,
                      qka_ref, qbn_ref, kbn_ref, lf_ref, qq_ref, kk_ref, qk_ref_out):
    cos, slo, shi = cos_ref[...], slo_ref[...], shi_ref[...]
    lane = lax.broadcasted_iota(I32, (1, 4 * LANES), 1)
    kscale = jnp.where(lane < 2 * LANES, 1.0, A_DK ** -0.5)
    qka_ref[...] = _rope(qk_ref[...], cos, slo, shi) * kscale
    gmat = gmat_ref[...]
    qbn = _head_rms(qb_ref[...], gmat) * (qg_ref[...] * (B_DH ** -0.5 * LOG2E))
    kbn = _head_rms(kb_ref[...], gmat) * kg_ref[...]
    qbn_ref[...] = qbn
    kbn_ref[...] = kbn
    lf_ref[...] = _log_sigmoid(fb_ref[...] + bf_ref[...])
    hsum = _head_sum_matrix()
    qq_ref[...] = _dot((qbn * qbn).astype(BF16), hsum)
    kk_ref[...] = _dot((kbn * kbn).astype(BF16), hsum)
    qk_ref_out[...] = _dot((qbn * kbn).astype(BF16), hsum)


def prep_even(h, cos, slo, shi, qg, kg, bf, gmat, *, tm):
    rows = h.shape[0]
    blk = lambda w, c: pl.BlockSpec((tm, w), lambda i, c=c: (i, c))
    row = lambda w: pl.BlockSpec((1, w), lambda i: (0, 0))
    return pl.pallas_call(
        _prep_even_kernel,
        grid=(rows // tm,),
        in_specs=[blk(512, EV_QK // 512), blk(512, EV_QB // 512), blk(512, EV_KB // 512),
                  blk(LANES, EV_FB // LANES), blk(LANES, 0), blk(LANES, 0), blk(LANES, 0),
                  row(512), row(512), row(LANES),
                  pl.BlockSpec((512, 512), lambda i: (0, 0))],
        out_specs=[blk(512, 0), blk(512, 0), blk(512, 0)] + [blk(LANES, 0)] * 4,
        out_shape=[jax.ShapeDtypeStruct((rows, 512), F32)] * 3
        + [jax.ShapeDtypeStruct((rows, LANES), F32)] * 4,
        compiler_params=_cparams(("parallel",)),
        name="prep_even",
    )(h, h, h, h, cos, slo, shi, qg, kg, bf, gmat)


def _prep_odd_kernel(qc_ref, iq_ref, qkd_ref, kvc_ref, misc_ref, cos_ref, slo_ref, shi_ref,
                     qg_ref, kg_ref, wg_ref, bg_ref, gmat_ref,
                     qcr_ref, iqr_ref, qkdo_ref, kcr_ref, ikr_ref, kcb_ref, vcb_ref, ikb_ref,
                     la_ref):
    cos, slo, shi = cos_ref[...], slo_ref[...], shi_ref[...]
    gmat = gmat_ref[...]
    qc = _head_rms(qc_ref[...], gmat) * (qg_ref[...] * (C_DH ** -0.5 * LOG2E))
    qcr_ref[...] = _rope(qc, cos, slo, shi)
    iqr_ref[...] = _rope(iq_ref[...], cos, slo, shi)
    lane = lax.broadcasted_iota(I32, (1, 4 * LANES), 1)
    qkdo_ref[...] = qkd_ref[...] * jnp.where(lane < 2 * LANES, D_DK ** -0.5, 1.0)
    kvc = kvc_ref[...]
    kc = _head_rms(kvc[:, :LANES], gmat[:LANES, :LANES]) * kg_ref[...]
    kcr = _rope(kc, cos, slo, shi)
    kcr_ref[...] = kcr
    kcb_ref[...] = kcr.astype(BF16)
    vcb_ref[...] = kvc[:, LANES:].astype(BF16)
    misc = misc_ref[...]
    ikr = _rope(misc, cos, slo, shi)
    lane1 = lax.broadcasted_iota(I32, (1, LANES), 1)
    ik2 = jnp.where(lane1 < HEAD, ikr, pltpu.roll(ikr, HEAD, 1))
    ikr_ref[...] = ik2
    ikb_ref[...] = ik2.astype(BF16)
    z = _dot(misc.astype(BF16), wg_ref[...]) + bg_ref[...]
    la_ref[...] = _log_sigmoid(z) * (1.0 / D_GATE_NORM)


def prep_odd(h, cos, slo, shi, qg, kg, wg, bg, gmat, *, tm):
    rows = h.shape[0]
    blk = lambda w, c: pl.BlockSpec((tm, w), lambda i, c=c: (i, c))
    row = lambda w: pl.BlockSpec((1, w), lambda i: (0, 0))
    return pl.pallas_call(
        _prep_odd_kernel,
        grid=(rows // tm,),
        in_specs=[blk(512, OD_QC // 512), blk(512, OD_IQ // 512), blk(512, OD_QKD // 512),
                  blk(256, OD_KVC // 256), blk(LANES, OD_MISC // LANES),
                  blk(LANES, 0), blk(LANES, 0), blk(LANES, 0),
                  row(512), row(LANES),
                  pl.BlockSpec((LANES, 256), lambda i: (0, 0)), row(256),
                  pl.BlockSpec((512, 512), lambda i: (0, 0))],
        out_specs=[blk(512, 0), blk(512, 0), blk(512, 0), blk(LANES, 0), blk(LANES, 0),
                   blk(LANES, 0), blk(LANES, 0), blk(LANES, 0), blk(256, 0)],
        out_shape=[jax.ShapeDtypeStruct((rows, 512), F32)] * 3
        + [jax.ShapeDtypeStruct((rows, LANES), F32)] * 2
        + [jax.ShapeDtypeStruct((rows, LANES), BF16)] * 3
        + [jax.ShapeDtypeStruct((rows, 256), F32)],
        compiler_params=_cparams(("parallel",)),
        name="prep_odd",
    )(h, h, h, h, h, cos, slo, shi, qg, kg, wg, bg, gmat)


def _scan_kernel(q_ref, k_ref, v_ref, g_ref, la_ref, gain_ref, s0_ref, tri_ref, bmask_ref,
                 o_ref, sfin_ref, st_ref, *, ts, nsteps):
    step = pl.program_id(1)

    @pl.when(step == 0)
    def _():
        st_ref[...] = s0_ref[0]

    q, k, v = q_ref[...], k_ref[...], v_ref[...]
    tri = tri_ref[...]
    a1, a2, a3 = _split3(la_ref[...])
    cum = _dot(tri, a1) + _dot(tri, a2) + _dot(tri, a3)
    qt = q * jnp.exp(cum)
    kt = k * jnp.exp(-cum)
    hm = _half_masks()
    bmask = bmask_ref[...]
    nsub = ts // SUB
    vb = v.astype(BF16)
    elast, upd = [], []
    for u in range(nsub):
        r0 = u * SUB
        last = cum[r0 + SUB - 1:r0 + SUB, :]
        k2 = (k[r0:r0 + SUB, :] * jnp.exp(last - cum[r0:r0 + SUB, :])).astype(BF16)
        elast.append(jnp.exp(last))
        upd.append([_dot_tn(vb[r0:r0 + SUB, 2 * p * LANES:2 * (p + 1) * LANES],
                            k2[:, p * LANES:(p + 1) * LANES]) * bmask for p in range(2)])
    seen = []
    for p in range(2):
        st = st_ref[p]
        row = []
        for u in range(nsub):
            row.append(st.astype(BF16))
            st = st * elast[u][:, p * LANES:(p + 1) * LANES] + upd[u][p]
        st_ref[p] = st
        seen.append(row)
    qtb = qt.astype(BF16)
    inter = jnp.concatenate(
        [jnp.concatenate([_dot_nt(qtb[u * SUB:(u + 1) * SUB, p * LANES:(p + 1) * LANES], seen[p][u])
                          for p in range(2)], axis=1) for u in range(nsub)], axis=0)
    ktb = kt.astype(BF16)
    same_chunk_causal = tri > 0
    intra = []
    for h in range(4):
        l0 = (h // 2) * LANES
        att = _dot_nt((qt[:, l0:l0 + LANES] * hm[h % 2]).astype(BF16), ktb[:, l0:l0 + LANES])
        att = jnp.where(same_chunk_causal, att, 0.0)
        intra.append(_dot(att.astype(BF16), vb[:, h * LANES:(h + 1) * LANES]))
    o = inter + jnp.concatenate(intra, axis=1)
    g = g_ref[...]
    gain = gain_ref[...]
    outs = []
    for h in range(4):
        oh = o[:, h * LANES:(h + 1) * LANES]
        ms = jnp.mean(oh * oh, axis=-1, keepdims=True)
        outs.append(oh * lax.rsqrt(ms + EPS) * gain[:, h * LANES:(h + 1) * LANES]
                    * _silu(g[:, h * LANES:(h + 1) * LANES]))
    o_ref[...] = jnp.concatenate(outs, axis=1)

    @pl.when(step == nsteps - 1)
    def _():
        sfin_ref[0] = st_ref[...]


def _scan_consts(ts):
    r = np.arange(ts)
    tri = ((r[:, None] // SUB == r[None, :] // SUB) & (r[None, :] <= r[:, None])).astype(np.float32)
    row = np.arange(2 * LANES)[:, None] // LANES
    col = np.arange(LANES)[None, :] // HEAD
    bmask = (row == col).astype(np.float32)
    return jnp.asarray(tri, BF16), jnp.asarray(bmask, F32)


def _ignore_last_input(fn, n_in):
    def wrapped(*refs):
        return fn(*refs[:n_in], *refs[n_in + 1:])
    return wrapped


def _shared_rows(into, rows, width):
    shape = jax.ShapeDtypeStruct((rows, width), F32)
    if into is None:
        return shape, [], []
    assert into.shape == (rows, width)
    return shape, [into], [pl.BlockSpec(memory_space=pl.ANY)]


def linear_scan(q_arr, q_col, k_arr, k_col, v_arr, v_col, g_arr, g_col, la, gain, s0,
                *, row0, nb, t, ts, la_shared, out_rows, into, name):
    nsteps = t // ts
    blk0 = row0 // ts
    tri, bmask = _scan_consts(ts)
    src = lambda w, c: pl.BlockSpec((ts, w), lambda b, s, c=c: (blk0 + b * nsteps + s, c))
    la_spec = pl.BlockSpec((ts, 256), lambda b, s: (0, 0)) if la_shared else src(256, 0)
    o_shape, extra, extra_specs = _shared_rows(into, out_rows, 512)
    n_in = 9
    kern = functools.partial(_scan_kernel, ts=ts, nsteps=nsteps)
    return pl.pallas_call(
        _ignore_last_input(kern, n_in) if extra else kern,
        grid=(nb, nsteps),
        in_specs=[src(256, q_col), src(256, k_col), src(512, v_col), src(512, g_col), la_spec,
                  pl.BlockSpec((1, 512), lambda b, s: (0, 0)),
                  pl.BlockSpec((1, 2, 256, LANES), lambda b, s: (b, 0, 0, 0)),
                  pl.BlockSpec((ts, ts), lambda b, s: (0, 0)),
                  pl.BlockSpec((256, LANES), lambda b, s: (0, 0))] + extra_specs,
        out_specs=[src(512, 0), pl.BlockSpec((1, 2, 256, LANES), lambda b, s: (b, 0, 0, 0))],
        out_shape=[o_shape, jax.ShapeDtypeStruct((nb, 2, 256, LANES), F32)],
        scratch_shapes=[pltpu.VMEM((2, 256, LANES), F32)],
        input_output_aliases={n_in: 0} if extra else {},
        compiler_params=_cparams(("parallel", "arbitrary")),
        name=name,
    )(q_arr, k_arr, v_arr, g_arr, la, gain, s0, tri, bmask, *extra)


def _state_to_pairs(s):
    nb = s.shape[0]
    st = jnp.swapaxes(s, -1, -2).reshape(nb, 2, 2, LANES, HEAD)
    z = jnp.zeros_like(st[:, :, 0])
    top = jnp.concatenate([st[:, :, 0], z], axis=-1)
    bot = jnp.concatenate([z, st[:, :, 1]], axis=-1)
    return jnp.concatenate([top, bot], axis=-2)


def _pairs_to_state(sp):
    nb = sp.shape[0]
    h0 = sp[:, :, :LANES, :HEAD]
    h1 = sp[:, :, LANES:, HEAD:]
    st = jnp.stack([h0, h1], axis=2).reshape(nb, 4, LANES, HEAD)
    return jnp.swapaxes(st, -1, -2)


def _cumsum_kernel(x_ref, tri_ref, o_ref, carry_ref):
    @pl.when(pl.program_id(1) == 0)
    def _():
        carry_ref[...] = jnp.zeros_like(carry_ref)

    tri = tri_ref[...]
    a1, a2, a3 = _split3(x_ref[...])
    c = _dot(tri, a1) + _dot(tri, a2) + _dot(tri, a3) + carry_ref[0:1, :]
    o_ref[...] = c * LOG2E
    carry_ref[...] = jnp.broadcast_to(c[-1:, :], carry_ref.shape)


def row_cumsum(x, *, row0, nb, n, tb):
    r = np.arange(tb)
    tri = jnp.asarray((r[None, :] <= r[:, None]).astype(np.float32), BF16)
    nblk = n // tb
    blk0 = row0 // tb
    return pl.pallas_call(
        _cumsum_kernel,
        grid=(nb, nblk),
        in_specs=[pl.BlockSpec((tb, LANES), lambda b, j: (blk0 + b * nblk + j, 0)),
                  pl.BlockSpec((tb, tb), lambda b, j: (0, 0))],
        out_specs=pl.BlockSpec((tb, LANES), lambda b, j: (b * nblk + j, 0)),
        out_shape=jax.ShapeDtypeStruct((nb * n, LANES), F32),
        scratch_shapes=[pltpu.VMEM((SUBLANES, LANES), F32)],
        compiler_params=_cparams(("parallel", "arbitrary")),
        name="logf_cumsum",
    )(x, tri)


def _col_reduce(x, op):
    tk, tq = x.shape
    if tk > FOLD and tk % FOLD == 0:
        x = op(x.reshape(tk // FOLD, FOLD, tq), axis=0)
    return op(x, axis=0, keepdims=True)


def _values_with_ones(vt_head):
    ones = jnp.ones((ONES_ROWS, vt_head.shape[1]), F32)
    return jnp.concatenate([vt_head, ones], axis=0).astype(BF16)


def _attn_update(s, smax, vt_ext, m_ref, acc_ref, idx):
    m_prev = m_ref[idx]
    m_new = jnp.maximum(m_prev, smax)
    alpha = jnp.exp2(m_prev - m_new)
    p = jnp.exp2(s - m_new)
    acc_ref[idx] = alpha * acc_ref[idx] + _dot(vt_ext, p.astype(BF16))
    m_ref[idx] = m_new


def _eye(n):
    r = lax.broadcasted_iota(I32, (n, n), 0)
    c = lax.broadcasted_iota(I32, (n, n), 1)
    return jnp.where(r == c, 1.0, 0.0).astype(BF16)


def _transpose_exact(x, eye):
    a1, a2, a3 = _split3(x)
    return _dot_nt(eye, a1) + _dot_nt(eye, a2) + _dot_nt(eye, a3)


def _write_heads(o_ref, acc_ref, tq):
    eye = _eye(tq)
    for c in range(4):
        a0, a1 = acc_ref[2 * c], acc_ref[2 * c + 1]
        ot = jnp.concatenate([a0[:HEAD] / a0[HEAD:HEAD + 1], a1[:HEAD] / a1[HEAD:HEAD + 1]], axis=0)
        o_ref[:, c * LANES:(c + 1) * LANES] = _transpose_exact(ot, eye)


def _fox_kernel(qi_ref, kj_ref, q_ref, k_ref, v_ref, c_ref, o_ref, m_ref, acc_ref,
                *, tq, tk, q_off):
    s_id = pl.program_id(1)
    qi = qi_ref[s_id]
    kj = kj_ref[s_id]
    j_last = (q_off + (qi + 1) * tq - 1) // tk

    @pl.when(kj == 0)
    def _():
        m_ref[...] = jnp.full_like(m_ref, NEG)
        acc_ref[...] = jnp.zeros_like(acc_ref)

    def block(masked):
        visible = None
        if masked:
            kpos = kj * tk + lax.broadcasted_iota(I32, (tk, tq), 0)
            qpos = q_off + qi * tq + lax.broadcasted_iota(I32, (tk, tq), 1)
            visible = kpos <= qpos
        qs, ks = _pair_masked_operands(q_ref[...], k_ref[...])
        v = v_ref[...]
        vts = []
        for p in range(4):
            vt = v[:, p * LANES:(p + 1) * LANES].T
            vts += [vt[:HEAD], vt[HEAD:]]
        _fox_block(qs, ks, vts, c_ref[...], visible, m_ref, acc_ref)

    @pl.when(kj < j_last)
    def _():
        block(False)

    @pl.when(kj == j_last)
    def _():
        block(True)
        _write_heads(o_ref, acc_ref, tq)


def _fox_bounded_kernel(qi_ref, kj_ref, q_ref, k_ref, v_ref, c_ref, bnd_ref, o_ref, m_ref, acc_ref,
                        *, tq, tk, q_off):
    s_id = pl.program_id(1)
    qi = qi_ref[s_id]
    kj = kj_ref[s_id]
    j_last = (q_off + (qi + 1) * tq - 1) // tk

    @pl.when(kj == 0)
    def _():
        acc_ref[...] = jnp.zeros_like(acc_ref)
        bt = _transpose_exact(bnd_ref[...], _eye(LANES))
        for h in range(B_HEADS):
            m_ref[h] = bt[h:h + 1, :]

    def block(masked):
        qs, ks = _pair_masked_operands(q_ref[...], k_ref[...])
        c = c_ref[...]
        if masked:
            kpos = kj * tk + lax.broadcasted_iota(I32, (tk, tq), 0)
            qpos = q_off + qi * tq + lax.broadcasted_iota(I32, (tk, tq), 1)
            visible = kpos <= qpos
        probs = []
        for h in range(B_HEADS):
            e = _dot_nt(ks[h], qs[h]) - c[:, h:h + 1] - m_ref[h]
            if masked:
                e = jnp.where(visible, e, NEG)
            probs.append(jnp.exp2(e).astype(BF16))
        v = v_ref[...]
        for p in range(4):
            vt = v[:, p * LANES:(p + 1) * LANES].T
            for hh in range(2):
                h = 2 * p + hh
                acc_ref[h] = acc_ref[h] + _dot(_values_with_ones(vt[hh * HEAD:(hh + 1) * HEAD]), probs[h])

    @pl.when(kj < j_last)
    def _():
        block(False)

    @pl.when(kj == j_last)
    def _():
        block(True)
        _write_heads(o_ref, acc_ref, tq)


def _pair_masked_operands(q, k):
    hm = _half_masks()
    qs, ks = [], []
    for p in range(4):
        kp = k[:, p * LANES:(p + 1) * LANES].astype(BF16)
        for hh in range(2):
            qs.append((q[:, p * LANES:(p + 1) * LANES] * hm[hh]).astype(BF16))
            ks.append(kp)
    return qs, ks


def _fox_block(qs, ks, vts, c, visible, m_ref, acc_ref):
    scores, maxima = [], []
    for h in range(B_HEADS):
        s = _dot_nt(ks[h], qs[h]) - c[:, h:h + 1]
        if visible is not None:
            s = jnp.where(visible, s, NEG)
        scores.append(s)
        maxima.append(_col_reduce(s, jnp.max))
    for h in range(B_HEADS):
        _attn_update(scores[h], maxima[h], _values_with_ones(vts[h]), m_ref, acc_ref, h)


def _fox_decode_kernel(q_ref, kp_ref, vp_ref, lp_ref, kn_ref, vn_ref, ln_ref, tri_ref, o_ref,
                       m_ref, acc_ref, *, ts, past, tb):
    m_ref[...] = jnp.full_like(m_ref, NEG)
    acc_ref[...] = jnp.zeros_like(acc_ref)
    eye = _eye(LANES)
    tri = tri_ref[...]

    def cumsum(x, tri_blk, carry):
        a1, a2, a3 = _split3(x)
        return _dot(tri_blk, a1) + _dot(tri_blk, a2) + _dot(tri_blk, a3) + carry

    def transposed(v):
        return _dot_nt(eye[:v.shape[1], :v.shape[1]], v.astype(BF16))

    lp = lp_ref[...]
    carry = jnp.zeros((1, lp.shape[1]), F32)
    c_blocks = []
    for j in range(past // tb):
        cb = cumsum(lp[j * tb:(j + 1) * tb], tri, carry)
        carry = cb[-1:, :]
        c_blocks.append(cb)
    c_past = jnp.concatenate(c_blocks, axis=0) * LOG2E
    c_new = cumsum(ln_ref[:, :B_HEADS], tri[:ts, :ts], carry) * LOG2E
    q = q_ref[...]
    def attend(k, v, c, visible):
        qs, ks = _pair_masked_operands(q, k)
        vts = []
        for p in range(4):
            vt = transposed(v[:, p * LANES:(p + 1) * LANES])
            vts += [vt[:HEAD], vt[HEAD:]]
        _fox_block(qs, ks, vts, c, visible, m_ref, acc_ref)

    attend(kp_ref[...], vp_ref[...], c_past, None)
    kpos = lax.broadcasted_iota(I32, (ts, ts), 0)
    qpos = lax.broadcasted_iota(I32, (ts, ts), 1)
    attend(kn_ref[...], vn_ref[...], c_new, kpos <= qpos)
    _write_heads(o_ref, acc_ref, ts)


def fox_decode(q_arr, k_past, v_past, lf_past, k_new, v_new, v_col, lf_new, *, row0, nb, ts, past,
               out_rows, into, name):
    tb = _pick_tile(past, (256, 128, 64))
    r = np.arange(tb)
    tri = jnp.asarray((r[None, :] <= r[:, None]).astype(np.float32), BF16)
    blk0 = row0 // ts
    new = lambda w, c: pl.BlockSpec((ts, w), lambda b, c=c: (blk0 + b, c))
    old = pl.BlockSpec((past, 512), lambda b: (b, 0))
    o_shape, extra, extra_specs = _shared_rows(into, out_rows, 512)
    n_in = 8
    kern = functools.partial(_fox_decode_kernel, ts=ts, past=past, tb=tb)
    return pl.pallas_call(
        _ignore_last_input(kern, n_in) if extra else kern,
        grid=(nb,),
        in_specs=[new(512, 0), old, old, pl.BlockSpec((past, B_HEADS), lambda b: (b, 0)),
                  new(512, 0), new(512, v_col),
                  new(LANES, 0), pl.BlockSpec((tb, tb), lambda b: (0, 0))] + extra_specs,
        out_specs=new(512, 0),
        out_shape=o_shape,
        scratch_shapes=[pltpu.VMEM((8, 1, ts), F32), pltpu.VMEM((8, HEAD + ONES_ROWS, ts), F32)],
        input_output_aliases={n_in: 0} if extra else {},
        compiler_params=_cparams(("parallel",)),
        name=name,
    )(q_arr, k_past, v_past, lf_past, k_new, v_new, lf_new, tri, *extra)


def fox_attention(q_arr, k_arr, v_arr, v_col, c_arr, bound, *, q_row0, nb, tq_total, tk_total, q_off,
                  tq, tk, out_rows, into, name):
    nq = tq_total // tq
    steps = [(i, j) for i in range(nq) for j in range((q_off + (i + 1) * tq - 1) // tk + 1)]
    qi = jnp.asarray([s[0] for s in steps], I32)
    kj = jnp.asarray([s[1] for s in steps], I32)
    qb0 = q_row0 // tq
    nkb = tk_total // tk
    o_shape, extra, extra_specs = _shared_rows(into, out_rows, 512)
    qrows = lambda w: pl.BlockSpec((tq, w), lambda b, s, qi, kj: (qb0 + b * nq + qi[s], 0))
    operands = [q_arr, k_arr, v_arr, c_arr]
    in_specs = [qrows(512),
                pl.BlockSpec((tk, 512), lambda b, s, qi, kj: (b * nkb + kj[s], 0)),
                pl.BlockSpec((tk, 512), lambda b, s, qi, kj: (b * nkb + kj[s], v_col)),
                pl.BlockSpec((tk, LANES), lambda b, s, qi, kj: (b * nkb + kj[s], 0))]
    body = _fox_kernel
    if bound is not None:
        operands.append(bound)
        in_specs.append(qrows(LANES))
        body = _fox_bounded_kernel
    n_in = 2 + len(operands)
    grid_spec = pltpu.PrefetchScalarGridSpec(
        num_scalar_prefetch=2,
        grid=(nb, len(steps)),
        in_specs=in_specs + extra_specs,
        out_specs=qrows(512),
        scratch_shapes=[pltpu.VMEM((8, 1, tq), F32), pltpu.VMEM((8, HEAD + ONES_ROWS, tq), F32)],
    )
    kern = functools.partial(body, tq=tq, tk=tk, q_off=q_off)
    return pl.pallas_call(
        _ignore_last_input(kern, n_in) if extra else kern,
        grid_spec=grid_spec,
        out_shape=o_shape,
        input_output_aliases={n_in: 0} if extra else {},
        compiler_params=_cparams(("parallel", "arbitrary")),
        name=name,
    )(qi, kj, *operands, *extra)


def _head_sum_matrix():
    r = lax.broadcasted_iota(I32, (4 * LANES, LANES), 0)
    c = lax.broadcasted_iota(I32, (4 * LANES, LANES), 1)
    return jnp.where(r // HEAD == c, 1.0, 0.0).astype(BF16)


def _sortable_key(x):
    b = lax.bitcast_convert_type(x + 0.0, I32)
    return jnp.where(b < 0, b ^ jnp.int32(0x7FFFFFFF), b)


def _dsa_kernel(iq_ref, misc_ref, qc_ref, kc_ref, vc_ref, ik_ref, o_ref,
                keys_ref, half_ref, qm_ref, m_ref, acc_ref, thr_ref, jcut_ref,
                *, tq, tk, n_keys, q_off, ksel):
    i = pl.program_id(1)
    hm = _half_masks()
    qcol = lax.broadcasted_iota(I32, (1, tq), 1)
    lim = jnp.minimum(((q_off + i * tq + qcol) // CHUNK + 1) * CHUNK, n_keys)
    lim_max = jnp.minimum(((q_off + (i + 1) * tq - 1) // CHUNK + 1) * CHUNK, n_keys)
    nkb = (lim_max + tk - 1) // tk
    eye = _eye(LANES)

    def key_pos(kb):
        return kb * tk + lax.broadcasted_iota(I32, (tk, tq), 0)

    iq = iq_ref[...]
    for hd in range(IDX_HEADS):
        c0 = (hd // 2) * LANES
        qm_ref[hd] = (iq[:, c0:c0 + LANES] * hm[hd % 2]).astype(BF16)
    iwt = _transpose_exact(misc_ref[...], eye) * IDX_SCALE

    def score_body(kb, kk_max):
        k0 = pl.multiple_of(kb * tk, tk)
        ikb = ik_ref[pl.ds(k0, tk), :]
        sc = jnp.zeros((tk, tq), F32)
        for hd in range(IDX_HEADS):
            a = _dot_nt(ikb, qm_ref[hd])
            sc = sc + jnp.maximum(a, 0.0) * iwt[MISC_IW + hd:MISC_IW + hd + 1, :]
        key = jnp.where(key_pos(kb) < lim, _sortable_key(sc), INT_MIN)
        keys_ref[kb] = key
        half_ref[kb] = jnp.right_shift(key, 16).astype(I16)
        kf32 = kc_ref[pl.ds(k0, tk), :].astype(F32)
        kk = _dot((kf32 * kf32).astype(BF16), half_ones)
        return jnp.maximum(kk_max, _col_reduce(kk, jnp.max))

    lane_r = lax.broadcasted_iota(I32, (LANES, LANES), 0)
    lane_c = lax.broadcasted_iota(I32, (LANES, LANES), 1)
    half_ones = jnp.where(lane_r // HEAD == lane_c // HEAD, 1.0, 0.0).astype(BF16)
    kk_max = lax.fori_loop(0, nkb, score_body, jnp.zeros((1, LANES), F32))

    def count(pred):
        def body(kb, acc):
            hit = jnp.where(pred(keys_ref[kb], key_pos(kb)), 1.0, 0.0)
            return acc + jnp.sum(hit.reshape(tk // COUNT_FOLD, COUNT_FOLD, tq), axis=0)
        acc = lax.fori_loop(0, nkb, body, jnp.zeros((COUNT_FOLD, tq), F32))
        return jnp.sum(acc, axis=0, keepdims=True)

    def count_half(cand):
        def body(kb, acc):
            hit = jnp.where(half_ref[kb] >= cand, jnp.int16(1), jnp.int16(0))
            parts = [hit[r * COUNT_FOLD:(r + 1) * COUNT_FOLD] for r in range(tk // COUNT_FOLD)]
            while len(parts) > 1:
                parts = [a + b for a, b in zip(parts[0::2], parts[1::2])] + parts[len(parts) & ~1:]
            return acc + parts[0]
        acc = lax.fori_loop(0, nkb, body, jnp.zeros((COUNT_FOLD, tq), I16))
        return jnp.sum(acc.astype(F32), axis=0, keepdims=True)

    def any_row(flag):
        return jnp.max(jnp.where(flag, 1.0, 0.0)) > 0.0

    thr_ref[...] = jnp.full((1, tq), INT_MIN + 1, I32)
    jcut_ref[...] = jnp.full((1, tq), 2 ** 30, I32)
    kf = float(ksel)

    @pl.when(lim_max > ksel)
    def _():
        few = lim <= ksel

        def unsettled(cres):
            return jnp.logical_and(cres != kf, jnp.logical_not(few))

        def search(first, last, group, to_half, state):
            assert (last - first) % group == 0

            def cond(state):
                t, _, _, go = state
                return jnp.logical_and(t < last, go)

            def body(state):
                t, res, cres, _ = state
                for g in range(group):
                    cand = res + jnp.left_shift(jnp.int32(1), 30 - g - t)
                    cnt = count_half(to_half(cand))
                    take = cnt >= kf
                    res = jnp.where(take, cand, res)
                    cres = jnp.where(take, cnt, cres)
                return t + group, res, cres, any_row(unsettled(cres))

            return lax.while_loop(cond, body, (jnp.int32(first),) + state)[1:]

        def high_half(x):
            return jnp.right_shift(x, 16).astype(I16)

        def low_half(x):
            return ((x & 0xFFFF) - 32768).astype(I16)

        c0 = count_half(jnp.zeros((1, tq), I16))
        res = jnp.where(c0 >= kf, 0, INT_MIN).astype(I32)
        cres = jnp.where(c0 >= kf, c0, 2.0 * n_keys)
        res, cres, go = search(0, 15, 3, high_half, (res, cres, any_row(unsettled(cres))))

        @pl.when(go)
        def _():
            rh = high_half(res)

            def relabel(kb, carry):
                hi = half_ref[kb]
                lo = low_half(keys_ref[kb])
                half_ref[kb] = jnp.where(hi > rh, jnp.int16(32767),
                                         jnp.where(hi < rh, jnp.int16(-32768), lo))
                return carry

            lax.fori_loop(0, nkb, relabel, 0)

        res, cres, open_rows = search(15, 31, 4, low_half, (res, cres, go))
        thr = jnp.maximum(res, INT_MIN + 1)
        thr_ref[...] = thr

        @pl.when(open_rows)
        def _():
            need = kf - count(lambda blk, kpos: blk > thr)
            n_eq = count(lambda blk, kpos: blk == thr)
            split = n_eq > need

            @pl.when(any_row(split))
            def _():
                def idx_body(t, lo):
                    cand = lo + jnp.left_shift(jnp.int32(1), 14 - t)
                    cnt = count(lambda blk, kpos: (blk == thr) & (kpos < cand))
                    return jnp.where(cnt < need, cand, lo)

                lo = lax.fori_loop(0, 15, idx_body, jnp.zeros((1, tq), I32))
                jcut_ref[...] = jnp.where(split, lo, 2 ** 30)

    qc = qc_ref[...]
    for hd in range(C_HEADS):
        c0 = (hd // 2) * LANES
        qm_ref[hd] = (qc[:, c0:c0 + LANES] * hm[hd % 2]).astype(BF16)
    thr = thr_ref[...]
    jcut = jcut_ref[...]

    def block_operands(kb):
        k0 = pl.multiple_of(kb * tk, tk)
        kblk = kc_ref[pl.ds(k0, tk), :]
        vt = _dot_nt(eye, vc_ref[pl.ds(k0, tk), :])
        vts = [_values_with_ones(vt[hh * HEAD:(hh + 1) * HEAD]) for hh in range(2)]
        keys = keys_ref[kb]
        sel = (keys > thr) | ((keys == thr) & (key_pos(kb) <= jcut))
        return kblk, vts, sel

    qq = _dot((qc * qc).astype(BF16), _head_sum_matrix())
    qq_t = _dot_nt(eye, qq.astype(BF16))
    bound = [jnp.sqrt(qq_t[hd:hd + 1, :] * kk_max[:, (hd % 2) * HEAD:(hd % 2) * HEAD + 1]) * BOUND_SLACK
             for hd in range(C_HEADS)]
    acc_ref[...] = jnp.zeros_like(acc_ref)

    def fast_body(kb, carry):
        kblk, vts, sel = block_operands(kb)
        keep = jnp.where(sel, 1.0, 0.0).astype(BF16)
        probs = [jnp.exp2(_dot_nt(kblk, qm_ref[hd]) - bound[hd]).astype(BF16) * keep
                 for hd in range(C_HEADS)]
        for hd in range(C_HEADS):
            acc_ref[hd] = acc_ref[hd] + _dot(vts[hd % 2], probs[hd])
        return carry

    lax.fori_loop(0, nkb, fast_body, 0)
    denom_min = acc_ref[0][HEAD:HEAD + 1, :]
    for hd in range(1, C_HEADS):
        denom_min = jnp.minimum(denom_min, acc_ref[hd][HEAD:HEAD + 1, :])

    @pl.when(jnp.min(denom_min) < DENOM_FLOOR)
    def _():
        m_ref[...] = jnp.full_like(m_ref, NEG)
        acc_ref[...] = jnp.zeros_like(acc_ref)

        def exact_body(kb, carry):
            kblk, vts, sel = block_operands(kb)
            scores = [jnp.where(sel, _dot_nt(kblk, qm_ref[hd]), NEG) for hd in range(C_HEADS)]
            maxima = [_col_reduce(s, jnp.max) for s in scores]
            for hd in range(C_HEADS):
                _attn_update(scores[hd], maxima[hd], vts[hd % 2], m_ref, acc_ref, hd)
            return carry

        lax.fori_loop(0, nkb, exact_body, 0)

    _write_heads(o_ref, acc_ref, tq)


def dsa_attention(iq_arr, misc_arr, misc_col, qc_arr, kc, vc, ik, *, q_row0, nb, tq_total,
                  n_keys, n_keys_pad, q_off, tq, tk, out_rows, into, name):
    nq = tq_total // tq
    qb0 = q_row0 // tq
    ksel = min(TOPK_MAX, n_keys // 4)
    qspec = lambda w, c: pl.BlockSpec((tq, w), lambda b, i, c=c: (qb0 + b * nq + i, c))
    kspec = pl.BlockSpec((n_keys_pad, LANES), lambda b, i: (b, 0))
    o_shape, extra, extra_specs = _shared_rows(into, out_rows, 512)
    n_in = 6
    kern = functools.partial(_dsa_kernel, tq=tq, tk=tk, n_keys=n_keys, q_off=q_off, ksel=ksel)
    return pl.pallas_call(
        _ignore_last_input(kern, n_in) if extra else kern,
        grid=(nb, nq),
        in_specs=[qspec(512, 0), qspec(LANES, misc_col), qspec(512, 0), kspec, kspec, kspec]
        + extra_specs,
        out_specs=qspec(512, 0),
        out_shape=o_shape,
        input_output_aliases={n_in: 0} if extra else {},
        scratch_shapes=[pltpu.VMEM((n_keys_pad // tk, tk, tq), I32),
                        pltpu.VMEM((n_keys_pad // tk, tk, tq), I16),
                        pltpu.VMEM((8, tq, LANES), BF16),
                        pltpu.VMEM((8, 1, tq), F32),
                        pltpu.VMEM((8, HEAD + ONES_ROWS, tq), F32),
                        pltpu.VMEM((1, tq), I32), pltpu.VMEM((1, tq), I32)],
        compiler_params=_cparams(("parallel", "arbitrary")),
        name=name,
    )(iq_arr, misc_arr, qc_arr, kc, vc, ik, *extra)


def _rope_tables(pos):
    half = HEAD // 2
    inv = jnp.power(ROPE_THETA, -jnp.arange(half, dtype=F32) / half)
    ang = pos.astype(F32)[:, None] * inv[None, :]
    cos, sin = jnp.cos(ang), jnp.sin(ang)
    z = jnp.zeros_like(sin)
    return (jnp.tile(cos, (1, 4)), jnp.tile(jnp.concatenate([-sin, z], 1), (1, 2)),
            jnp.tile(jnp.concatenate([z, sin], 1), (1, 2)))


def _head_mean_matrix():
    r = np.arange(512)
    return jnp.asarray((r[:, None] // HEAD == r[None, :] // HEAD).astype(np.float32), BF16)


def _pick_tile(n, prefs):
    for t in prefs:
        if n % t == 0:
            return t
    raise ValueError(f"no tile in {prefs} divides {n}")


def _pad_keys(x, n_pad):
    nb, n, w = x.shape
    if n_pad > n:
        x = jnp.concatenate([x, jnp.zeros((nb, n_pad - n, w), x.dtype)], axis=1)
    return x.reshape(nb * n_pad, w)


def kernel(x_prompt, x_sample, state_a, cache_b_k, cache_b_v, cache_b_logf, cache_c_k, cache_c_v,
           cache_c_idx, state_d, norm_mix, norm_ffn, even_w_in, even_b_f, even_q_norm, even_k_norm,
           even_w_out, odd_w_in, odd_w_gate_up, odd_b_gate, odd_q_norm, odd_k_norm, odd_o_norm,
           odd_w_out, ffn_w1, ffn_w2):
    bp, tp = x_prompt.shape[:2]
    nb, ts = x_sample.shape[:2]
    past = cache_b_k.shape[2]
    depth = norm_mix.shape[0]
    rp, rs = bp * tp, nb * ts
    rows = rp + rs
    assert bp == 1 and tp % 128 == 0 and ts % SUB == 0 and rp % ts == 0 and past % CHUNK == 0

    y = jnp.concatenate([x_prompt.reshape(rp, D_MODEL), x_sample.reshape(rs, D_MODEL)], axis=0)
    pos = jnp.concatenate([jnp.arange(tp, dtype=I32),
                           jnp.tile(past + jnp.arange(ts, dtype=I32), nb)])
    cos, slo, shi = _rope_tables(pos)
    gmat = _head_mean_matrix()

    tm = _pick_tile(rows, (512, 256, 128, 64, 32))
    tm_prep = _pick_tile(rows, (256, 128, 64, 32))
    ts_p = _pick_tile(tp, (128, 64, 32, 16))
    tq_fox = _pick_tile(tp, (512, 256, 128))
    tk_fox = _pick_tile(tp, (512, 256, 128))
    tq_dsa = _pick_tile(tp, (256, 128))
    tk_dsa = _pick_tile(tp, (512, 256, 128))
    nk_s = past + ts
    nk_s_pad = -(-nk_s // 384) * 384
    tb_p = _pick_tile(tp, (512, 256, 128))

    lg = jnp.log1p(-jnp.exp2(-5.0 - jnp.arange(A_HEADS, dtype=F32)))
    la_ret = jnp.repeat(lg, A_DK)[None, :]
    ones_gain = jnp.ones((1, 512), F32)
    zero_state = jnp.zeros((bp, 2, 256, LANES), F32)

    a_p, a_s, bk_p, bk_s, bv_p, bv_s, bf_p, bf_s = [], [], [], [], [], [], [], []
    ck_p, ck_s, cv_p, cv_s, ci_p, ci_s, d_p, d_s = [], [], [], [], [], [], [], []

    for l in range(depth):
        i = l // 2
        if l % 2 == 0:
            w_in = jnp.concatenate(
                [even_w_in[i], jnp.zeros((D_MODEL, EV_WIDTH - even_w_in.shape[2]), F32)], 1).astype(BF16)
            h = norm_matmul(y, norm_mix[l], w_in, tm=tm, tn=EV_WIDTH, name="even_in_proj")
            bf = jnp.concatenate([even_b_f[i], jnp.zeros((LANES - B_HEADS,), F32)])[None, :]
            qka, qbn, kbn, lf, qq, kk, qk = prep_even(
                h, cos, slo, shi, jnp.tile(even_q_norm[i], B_HEADS)[None, :],
                jnp.tile(even_k_norm[i], B_HEADS)[None, :], bf, gmat, tm=tm_prep)
            oa, sa_p = linear_scan(qka, 0, qka, 1, h, EV_VA // 512, h, EV_GA // 512,
                                   jnp.broadcast_to(la_ret, (ts_p, 256)), ones_gain, zero_state,
                                   row0=0, nb=bp, t=tp, ts=ts_p, la_shared=True, out_rows=rows,
                                   into=None, name="retention_prompt")
            oa, sa_s = linear_scan(qka, 0, qka, 1, h, EV_VA // 512, h, EV_GA // 512,
                                   jnp.broadcast_to(la_ret, (ts, 256)), ones_gain,
                                   _state_to_pairs(state_a[i]),
                                   row0=rp, nb=nb, t=ts, ts=ts, la_shared=True, out_rows=rows,
                                   into=oa, name="retention_decode")
            lf8 = lf[:, :B_HEADS]
            c_p = row_cumsum(lf, row0=0, nb=bp, n=tp, tb=tb_p)
            qk_bound = jnp.sqrt(qq[:rp] * jnp.max(kk[:rp], axis=0, keepdims=True)) * BOUND_SLACK
            tight = jnp.max((qk_bound - qk[:rp])[:, :B_HEADS]) < FOX_BOUND_RANGE
            fox_prompt = functools.partial(
                fox_attention, qbn, kbn, h, EV_VB // 512, c_p, q_row0=0, nb=bp, tq_total=tp,
                tk_total=tp, q_off=0, tq=tq_fox, tk=tk_fox, out_rows=rows, into=None)
            ob = lax.cond(tight,
                          lambda: fox_prompt(qk_bound - c_p, name="fox_prompt_bounded"),
                          lambda: fox_prompt(None, name="fox_prompt"))
            kb_new = kbn[rp:].reshape(nb, ts, 512)
            vb_new = h[rp:, EV_VB:EV_VB + 512].reshape(nb, ts, 512)
            ob = fox_decode(qbn, cache_b_k[i].reshape(nb * past, 512),
                            cache_b_v[i].reshape(nb * past, 512),
                            cache_b_logf[i].reshape(nb * past, B_HEADS), kbn, h, EV_VB // 512, lf,
                            row0=rp, nb=nb, ts=ts, past=past, out_rows=rows, into=ob,
                            name="fox_decode")
            w_out = even_w_out[i].astype(BF16)
            y = matmul_residual(y, [oa, ob], [w_out[:512], w_out[512:]], tm=tm, name="even_out_proj")
            a_p.append(_pairs_to_state(sa_p))
            a_s.append(_pairs_to_state(sa_s))
            bk_p.append(kbn[:rp].reshape(bp, tp, B_HEADS, B_DH))
            bk_s.append(kb_new.reshape(nb, ts, B_HEADS, B_DH))
            bv_p.append(h[:rp, EV_VB:EV_VB + 512].reshape(bp, tp, B_HEADS, B_DH))
            bv_s.append(vb_new.reshape(nb, ts, B_HEADS, B_DH))
            bf_p.append(lf8[:rp].reshape(bp, tp, B_HEADS))
            bf_s.append(lf8[rp:].reshape(nb, ts, B_HEADS))
        else:
            w = odd_w_in[i]
            offs = np.cumsum([0, 512, 128, 128, 512, 64, 8, 256, 256, 512, 512, 16])
            qc_w, kc_w, vc_w, iq_w, ik_w, iw_w, qd_w, kd_w, vd_w, gd_w, gr_w = [
                w[:, int(a):int(b)] for a, b in zip(offs[:-1], offs[1:])]
            qc_w = qc_w.reshape(D_MODEL, 2, 4, C_DH).transpose(0, 2, 1, 3).reshape(D_MODEL, 512)
            w_in = jnp.concatenate(
                [qc_w, iq_w, vd_w, gd_w, qd_w, kd_w, kc_w, vc_w, ik_w, iw_w, gr_w,
                 jnp.zeros((D_MODEL, OD_WIDTH - OD_MISC - 88), F32)], 1).astype(BF16)
            h = norm_matmul(y, norm_mix[l], w_in, tm=tm, tn=OD_WIDTH, name="odd_in_proj")
            wg = jnp.zeros((LANES, 256), F32).at[MISC_GR:MISC_GR + D_GATE_RANK].set(
                odd_w_gate_up[i]).astype(BF16)
            qcr, iqr, qkd, kcr, ikr, kcb, vcb, ikb, la = prep_odd(
                h, cos, slo, shi, jnp.tile(odd_q_norm[i], C_HEADS)[None, :],
                jnp.tile(odd_k_norm[i], C_KV_HEADS)[None, :], wg, odd_b_gate[i][None, :], gmat,
                tm=tm_prep)
            oc = dsa_attention(iqr, h, OD_MISC // LANES, qcr, kcb, vcb, ikb,
                               q_row0=0, nb=bp, tq_total=tp, n_keys=tp, n_keys_pad=tp, q_off=0,
                               tq=tq_dsa, tk=tk_dsa, out_rows=rows, into=None, name="dsa_prompt")

            def with_past(cache, new):
                return _pad_keys(jnp.concatenate(
                    [cache.reshape(nb, past, -1).astype(BF16), new[rp:].reshape(nb, ts, -1)], 1), nk_s_pad)

            ik_past = jnp.concatenate([cache_c_idx[i], cache_c_idx[i]], axis=-1)
            oc = dsa_attention(iqr, h, OD_MISC // LANES, qcr,
                               with_past(cache_c_k[i], kcb), with_past(cache_c_v[i], vcb),
                               with_past(ik_past, ikb),
                               q_row0=rp, nb=nb, tq_total=ts, n_keys=nk_s, n_keys_pad=nk_s_pad,
                               q_off=past, tq=ts, tk=384, out_rows=rows, into=oc, name="dsa_decode")
            gain = jnp.tile(odd_o_norm[i], D_HEADS)[None, :]
            od, sd_p = linear_scan(qkd, 0, qkd, 1, h, OD_VD // 512, h, OD_GD // 512, la, gain,
                                   zero_state, row0=0, nb=bp, t=tp, ts=ts_p, la_shared=False,
                                   out_rows=rows, into=None, name="gla_prompt")
            od, sd_s = linear_scan(qkd, 0, qkd, 1, h, OD_VD // 512, h, OD_GD // 512, la, gain,
                                   _state_to_pairs(state_d[i]), row0=rp, nb=nb, t=ts, ts=ts,
                                   la_shared=False, out_rows=rows, into=od, name="gla_decode")
            w_out = odd_w_out[i]
            w_oc = w_out[:512].reshape(2, 4, C_DH, D_MODEL).transpose(1, 0, 2, 3).reshape(512, D_MODEL)
            y = matmul_residual(y, [oc, od], [w_oc.astype(BF16), w_out[512:].astype(BF16)], tm=tm,
                                name="odd_out_proj")
            d_p.append(_pairs_to_state(sd_p))
            d_s.append(_pairs_to_state(sd_s))
            ck_p.append(kcr[:rp].reshape(bp, tp, C_KV_HEADS, C_DH))
            ck_s.append(kcr[rp:].reshape(nb, ts, C_KV_HEADS, C_DH))
            cv_p.append(h[:rp, OD_KVC + LANES:OD_KVC + 2 * LANES].reshape(bp, tp, C_KV_HEADS, C_DH))
            cv_s.append(h[rp:, OD_KVC + LANES:OD_KVC + 2 * LANES].reshape(nb, ts, C_KV_HEADS, C_DH))
            ci_p.append(ikr[:rp, :IDX_DIM].reshape(bp, tp, IDX_DIM))
            ci_s.append(ikr[rp:, :IDX_DIM].reshape(nb, ts, IDX_DIM))
        hid = norm_matmul(y, norm_ffn[l], ffn_w1[l].astype(BF16), tm=tm, tn=D_FF, relu2=True,
                          out_dtype=BF16, name="mlp_up")
        y = matmul_residual(y, [hid], [ffn_w2[l].astype(BF16)], tm=tm, name="mlp_down")

    return (y[:rp].reshape(bp, tp, D_MODEL), y[rp:].reshape(nb, ts, D_MODEL),
            jnp.stack(a_p), jnp.stack(a_s), jnp.stack(bk_p), jnp.stack(bk_s),
            jnp.stack(bv_p), jnp.stack(bv_s), jnp.stack(bf_p), jnp.stack(bf_s),
            jnp.stack(ck_p), jnp.stack(ck_s), jnp.stack(cv_p), jnp.stack(cv_s),
            jnp.stack(ci_p), jnp.stack(ci_s), jnp.stack(d_p), jnp.stack(d_s))
```

```python
import functools

import numpy as np
import jax
import jax.numpy as jnp
from jax import lax
from jax.experimental import pallas as pl
from jax.experimental.pallas import tpu as pltpu

F32 = jnp.float32
BF16 = jnp.bfloat16
I32 = jnp.int32
I16 = jnp.int16

D_MODEL = 1024
CHUNK = 64
ROPE_THETA = 10000.0
EPS = 1e-6
A_HEADS, A_DK, A_DV = 4, 64, 128
B_HEADS, B_DH = 8, 64
C_HEADS, C_KV_HEADS, C_DH = 8, 2, 64
IDX_HEADS, IDX_DIM = 8, 64
IDX_SCALE = (IDX_HEADS * IDX_DIM) ** -0.5
TOPK_MAX = 256
D_HEADS, D_DK, D_DV = 4, 64, 128
D_GATE_RANK = 16
D_GATE_NORM = 16.0
D_FF = 4 * D_MODEL

LANES = 128
SUBLANES = 8
VMEM_LIMIT_BYTES = 56 * 1024 * 1024

HEAD = 64
SUB = 16
FOLD = 64
COUNT_FOLD = 32
SAMPLE_ROWS = 2 * COUNT_FOLD
SAMPLE_SPAN = 24
ONES_ROWS = 16
NEG = -1e30
LOG2E = 1.4426950408889634
BOUND_SLACK = 1.03
DENOM_FLOOR = 2.0 ** -100
FOX_BOUND_RANGE = 90.0
INT_MIN = -(2 ** 31)

EV_QK, EV_VA, EV_GA, EV_QB, EV_KB, EV_VB, EV_FB = 0, 512, 1024, 1536, 2048, 2560, 3072
EV_WIDTH = 3200
OD_QC, OD_IQ, OD_VD, OD_GD, OD_QKD, OD_KVC, OD_MISC = 0, 512, 1024, 1536, 2048, 2560, 2816
OD_WIDTH = 2944
MISC_IK, MISC_IW, MISC_GR = 0, 64, 72


def _cparams(sem):
    return pltpu.CompilerParams(dimension_semantics=sem, vmem_limit_bytes=VMEM_LIMIT_BYTES)


def _dot(a, b):
    return jnp.dot(a, b, preferred_element_type=F32)


def _dot_nt(a, b):
    return lax.dot_general(a, b, (((1,), (1,)), ((), ())), preferred_element_type=F32)


def _dot_tn(a, b):
    return lax.dot_general(a, b, (((0,), (0,)), ((), ())), preferred_element_type=F32)


def _split3(x):
    a1 = x.astype(BF16)
    r1 = x - a1.astype(F32)
    a2 = r1.astype(BF16)
    a3 = (r1 - a2.astype(F32)).astype(BF16)
    return a1, a2, a3


def _half_masks():
    lane = lax.broadcasted_iota(I32, (1, LANES), 1)
    lo = (lane < HEAD).astype(F32)
    return lo, 1.0 - lo


def _log_sigmoid(x):
    return jnp.minimum(x, 0.0) - jnp.log1p(jnp.exp(-jnp.abs(x)))


def _silu(x):
    return x / (1.0 + jnp.exp(-x))


def _norm_mm_kernel(x_ref, g_ref, w_ref, o_ref, *, relu2):
    x = x_ref[...]
    ms = jnp.mean(x * x, axis=-1, keepdims=True)
    xn = (x * lax.rsqrt(ms + EPS) * g_ref[...]).astype(BF16)
    y = _dot(xn, w_ref[...])
    if relu2:
        y = jnp.maximum(y, 0.0)
        y = y * y
    o_ref[...] = y.astype(o_ref.dtype)


def norm_matmul(x, g, w, *, tm, tn, relu2=False, out_dtype=F32, name):
    rows, k = x.shape
    n = w.shape[1]
    return pl.pallas_call(
        functools.partial(_norm_mm_kernel, relu2=relu2),
        grid=(rows // tm, n // tn),
        in_specs=[pl.BlockSpec((tm, k), lambda i, j: (i, 0)),
                  pl.BlockSpec((1, k), lambda i, j: (0, 0)),
                  pl.BlockSpec((k, tn), lambda i, j: (0, j))],
        out_specs=pl.BlockSpec((tm, tn), lambda i, j: (i, j)),
        out_shape=jax.ShapeDtypeStruct((rows, n), out_dtype),
        compiler_params=_cparams(("parallel", "parallel")),
        name=name,
    )(x, g.reshape(1, k), w)


def _mm_res_kernel(*refs, n_in):
    res_ref = refs[0]
    o_ref = refs[-1]
    acc = res_ref[...]
    for a_ref, w_ref in zip(refs[1:1 + n_in], refs[1 + n_in:1 + 2 * n_in]):
        acc = acc + _dot(a_ref[...].astype(BF16), w_ref[...])
    o_ref[...] = acc


def matmul_residual(res, a_list, w_list, *, tm, name):
    rows, n = res.shape
    n_in = len(a_list)
    in_specs = [pl.BlockSpec((tm, n), lambda i: (i, 0))]
    in_specs += [pl.BlockSpec((tm, a.shape[1]), lambda i: (i, 0)) for a in a_list]
    in_specs += [pl.BlockSpec(w.shape, lambda i: (0, 0)) for w in w_list]
    return pl.pallas_call(
        functools.partial(_mm_res_kernel, n_in=n_in),
        grid=(rows // tm,),
        in_specs=in_specs,
        out_specs=pl.BlockSpec((tm, n), lambda i: (i, 0)),
        out_shape=jax.ShapeDtypeStruct((rows, n), F32),
        compiler_params=_cparams(("parallel",)),
        name=name,
    )(res, *a_list, *w_list)


def _tile_lanes(t, width):
    n = width // LANES
    return t if n == 1 else jnp.concatenate([t] * n, axis=1)


def _rope(x, cos, sin_lo, sin_hi):
    w = x.shape[1]
    up = pltpu.roll(x, w - HEAD // 2, 1)
    dn = pltpu.roll(x, HEAD // 2, 1)
    return x * _tile_lanes(cos, w) + up * _tile_lanes(sin_lo, w) + dn * _tile_lanes(sin_hi, w)


def _head_rms(x, gmat):
    s = x * x
    hi = s.astype(BF16)
    lo = (s - hi.astype(F32)).astype(BF16)
    ms = (_dot(hi, gmat) + _dot(lo, gmat)) * (1.0 / HEAD)
    return x * lax.rsqrt(ms + EPS)


def _prep_even_kernel(qk_ref, qb_ref, kb_ref, fb_ref, cos_ref, slo_ref, shi_ref,
                      qg_ref, kg_ref, bf_ref, gmat_ref,
                      qka_ref, qbn_ref, kbn_ref, lf_ref, qq_ref, kk_ref, qk_ref_out):
    cos, slo, shi = cos_ref[...], slo_ref[...], shi_ref[...]
    lane = lax.broadcasted_iota(I32, (1, 4 * LANES), 1)
    kscale = jnp.where(lane < 2 * LANES, 1.0, A_DK ** -0.5)
    qka_ref[...] = _rope(qk_ref[...], cos, slo, shi) * kscale
    gmat = gmat_ref[...]
    qbn = _head_rms(qb_ref[...], gmat) * (qg_ref[...] * (B_DH ** -0.5 * LOG2E))
    kbn = _head_rms(kb_ref[...], gmat) * kg_ref[...]
    qbn_ref[...] = qbn
    kbn_ref[...] = kbn
    lf_ref[...] = _log_sigmoid(fb_ref[...] + bf_ref[...])
    hsum = _head_sum_matrix()
    qq_ref[...] = _dot((qbn * qbn).astype(BF16), hsum)
    kk_ref[...] = _dot((kbn * kbn).astype(BF16), hsum)
    qk_ref_out[...] = _dot((qbn * kbn).astype(BF16), hsum)


def prep_even(h, cos, slo, shi, qg, kg, bf, gmat, *, tm):
    rows = h.shape[0]
    blk = lambda w, c: pl.BlockSpec((tm, w), lambda i, c=c: (i, c))
    row = lambda w: pl.BlockSpec((1, w), lambda i: (0, 0))
    return pl.pallas_call(
        _prep_even_kernel,
        grid=(rows // tm,),
        in_specs=[blk(512, EV_QK // 512), blk(512, EV_QB // 512), blk(512, EV_KB // 512),
                  blk(LANES, EV_FB // LANES), blk(LANES, 0), blk(LANES, 0), blk(LANES, 0),
                  row(512), row(512), row(LANES),
                  pl.BlockSpec((512, 512), lambda i: (0, 0))],
        out_specs=[blk(512, 0), blk(512, 0), blk(512, 0)] + [blk(LANES, 0)] * 4,
        out_shape=[jax.ShapeDtypeStruct((rows, 512), F32)] * 3
        + [jax.ShapeDtypeStruct((rows, LANES), F32)] * 4,
        compiler_params=_cparams(("parallel",)),
        name="prep_even",
    )(h, h, h, h, cos, slo, shi, qg, kg, bf, gmat)


def _prep_odd_kernel(qc_ref, iq_ref, qkd_ref, kvc_ref, misc_ref, cos_ref, slo_ref, shi_ref,
                     qg_ref, kg_ref, wg_ref, bg_ref, gmat_ref,
                     qcr_ref, iqr_ref, qkdo_ref, kcr_ref, ikr_ref, kcb_ref, vcb_ref, ikb_ref,
                     la_ref):
    cos, slo, shi = cos_ref[...], slo_ref[...], shi_ref[...]
    gmat = gmat_ref[...]
    qc = _head_rms(qc_ref[...], gmat) * (qg_ref[...] * (C_DH ** -0.5 * LOG2E))
    qcr_ref[...] = _rope(qc, cos, slo, shi)
    iqr_ref[...] = _rope(iq_ref[...], cos, slo, shi)
    lane = lax.broadcasted_iota(I32, (1, 4 * LANES), 1)
    qkdo_ref[...] = qkd_ref[...] * jnp.where(lane < 2 * LANES, D_DK ** -0.5, 1.0)
    kvc = kvc_ref[...]
    kc = _head_rms(kvc[:, :LANES], gmat[:LANES, :LANES]) * kg_ref[...]
    kcr = _rope(kc, cos, slo, shi)
    kcr_ref[...] = kcr
    kcb_ref[...] = kcr.astype(BF16)
    vcb_ref[...] = kvc[:, LANES:].astype(BF16)
    misc = misc_ref[...]
    ikr = _rope(misc, cos, slo, shi)
    lane1 = lax.broadcasted_iota(I32, (1, LANES), 1)
    ik2 = jnp.where(lane1 < HEAD, ikr, pltpu.roll(ikr, HEAD, 1))
    ikr_ref[...] = ik2
    ikb_ref[...] = ik2.astype(BF16)
    z = _dot(misc.astype(BF16), wg_ref[...]) + bg_ref[...]
    la_ref[...] = _log_sigmoid(z) * (1.0 / D_GATE_NORM)


def prep_odd(h, cos, slo, shi, qg, kg, wg, bg, gmat, *, tm):
    rows = h.shape[0]
    blk = lambda w, c: pl.BlockSpec((tm, w), lambda i, c=c: (i, c))
    row = lambda w: pl.BlockSpec((1, w), lambda i: (0, 0))
    return pl.pallas_call(
        _prep_odd_kernel,
        grid=(rows // tm,),
        in_specs=[blk(512, OD_QC // 512), blk(512, OD_IQ // 512), blk(512, OD_QKD // 512),
                  blk(256, OD_KVC // 256), blk(LANES, OD_MISC // LANES),
                  blk(LANES, 0), blk(LANES, 0), blk(LANES, 0),
                  row(512), row(LANES),
                  pl.BlockSpec((LANES, 256), lambda i: (0, 0)), row(256),
                  pl.BlockSpec((512, 512), lambda i: (0, 0))],
        out_specs=[blk(512, 0), blk(512, 0), blk(512, 0), blk(LANES, 0), blk(LANES, 0),
                   blk(LANES, 0), blk(LANES, 0), blk(LANES, 0), blk(256, 0)],
        out_shape=[jax.ShapeDtypeStruct((rows, 512), F32)] * 3
        + [jax.ShapeDtypeStruct((rows, LANES), F32)] * 2
        + [jax.ShapeDtypeStruct((rows, LANES), BF16)] * 3
        + [jax.ShapeDtypeStruct((rows, 256), F32)],
        compiler_params=_cparams(("parallel",)),
        name="prep_odd",
    )(h, h, h, h, h, cos, slo, shi, qg, kg, wg, bg, gmat)


def _scan_kernel(q_ref, k_ref, v_ref, g_ref, la_ref, gain_ref, s0_ref, tri_ref, bmask_ref,
                 o_ref, sfin_ref, st_ref, *, ts, nsteps):
    step = pl.program_id(1)

    @pl.when(step == 0)
    def _():
        st_ref[...] = s0_ref[0]

    q, k, v = q_ref[...], k_ref[...], v_ref[...]
    tri = tri_ref[...]
    a1, a2, a3 = _split3(la_ref[...])
    cum = _dot(tri, a1) + _dot(tri, a2) + _dot(tri, a3)
    qt = q * jnp.exp(cum)
    kt = k * jnp.exp(-cum)
    hm = _half_masks()
    bmask = bmask_ref[...]
    nsub = ts // SUB
    vb = v.astype(BF16)
    elast, upd = [], []
    for u in range(nsub):
        r0 = u * SUB
        last = cum[r0 + SUB - 1:r0 + SUB, :]
        k2 = (k[r0:r0 + SUB, :] * jnp.exp(last - cum[r0:r0 + SUB, :])).astype(BF16)
        elast.append(jnp.exp(last))
        upd.append([_dot_tn(vb[r0:r0 + SUB, 2 * p * LANES:2 * (p + 1) * LANES],
                            k2[:, p * LANES:(p + 1) * LANES]) * bmask for p in range(2)])
    seen = []
    for p in range(2):
        st = st_ref[p]
        row = []
        for u in range(nsub):
            row.append(st.astype(BF16))
            st = st * elast[u][:, p * LANES:(p + 1) * LANES] + upd[u][p]
        st_ref[p] = st
        seen.append(row)
    qtb = qt.astype(BF16)
    inter = jnp.concatenate(
        [jnp.concatenate([_dot_nt(qtb[u * SUB:(u + 1) * SUB, p * LANES:(p + 1) * LANES], seen[p][u])
                          for p in range(2)], axis=1) for u in range(nsub)], axis=0)
    ktb = kt.astype(BF16)
    same_chunk_causal = tri > 0
    intra = []
    for h in range(4):
        l0 = (h // 2) * LANES
        att = _dot_nt((qt[:, l0:l0 + LANES] * hm[h % 2]).astype(BF16), ktb[:, l0:l0 + LANES])
        att = jnp.where(same_chunk_causal, att, 0.0)
        intra.append(_dot(att.astype(BF16), vb[:, h * LANES:(h + 1) * LANES]))
    o = inter + jnp.concatenate(intra, axis=1)
    g = g_ref[...]
    gain = gain_ref[...]
    outs = []
    for h in range(4):
        oh = o[:, h * LANES:(h + 1) * LANES]
        ms = jnp.mean(oh * oh, axis=-1, keepdims=True)
        outs.append(oh * lax.rsqrt(ms + EPS) * gain[:, h * LANES:(h + 1) * LANES]
                    * _silu(g[:, h * LANES:(h + 1) * LANES]))
    o_ref[...] = jnp.concatenate(outs, axis=1)

    @pl.when(step == nsteps - 1)
    def _():
        sfin_ref[0] = st_ref[...]


def _scan_consts(ts):
    r = np.arange(ts)
    tri = ((r[:, None] // SUB == r[None, :] // SUB) & (r[None, :] <= r[:, None])).astype(np.float32)
    row = np.arange(2 * LANES)[:, None] // LANES
    col = np.arange(LANES)[None, :] // HEAD
    bmask = (row == col).astype(np.float32)
    return jnp.asarray(tri, BF16), jnp.asarray(bmask, F32)


def _ignore_last_input(fn, n_in):
    def wrapped(*refs):
        return fn(*refs[:n_in], *refs[n_in + 1:])
    return wrapped


def _shared_rows(into, rows, width):
    shape = jax.ShapeDtypeStruct((rows, width), F32)
    if into is None:
        return shape, [], []
    assert into.shape == (rows, width)
    return shape, [into], [pl.BlockSpec(memory_space=pl.ANY)]


def linear_scan(q_arr, q_col, k_arr, k_col, v_arr, v_col, g_arr, g_col, la, gain, s0,
                *, row0, nb, t, ts, la_shared, out_rows, into, name):
    nsteps = t // ts
    blk0 = row0 // ts
    tri, bmask = _scan_consts(ts)
    src = lambda w, c: pl.BlockSpec((ts, w), lambda b, s, c=c: (blk0 + b * nsteps + s, c))
    la_spec = pl.BlockSpec((ts, 256), lambda b, s: (0, 0)) if la_shared else src(256, 0)
    o_shape, extra, extra_specs = _shared_rows(into, out_rows, 512)
    n_in = 9
    kern = functools.partial(_scan_kernel, ts=ts, nsteps=nsteps)
    return pl.pallas_call(
        _ignore_last_input(kern, n_in) if extra else kern,
        grid=(nb, nsteps),
        in_specs=[src(256, q_col), src(256, k_col), src(512, v_col), src(512, g_col), la_spec,
                  pl.BlockSpec((1, 512), lambda b, s: (0, 0)),
                  pl.BlockSpec((1, 2, 256, LANES), lambda b, s: (b, 0, 0, 0)),
                  pl.BlockSpec((ts, ts), lambda b, s: (0, 0)),
                  pl.BlockSpec((256, LANES), lambda b, s: (0, 0))] + extra_specs,
        out_specs=[src(512, 0), pl.BlockSpec((1, 2, 256, LANES), lambda b, s: (b, 0, 0, 0))],
        out_shape=[o_shape, jax.ShapeDtypeStruct((nb, 2, 256, LANES), F32)],
        scratch_shapes=[pltpu.VMEM((2, 256, LANES), F32)],
        input_output_aliases={n_in: 0} if extra else {},
        compiler_params=_cparams(("parallel", "arbitrary")),
        name=name,
    )(q_arr, k_arr, v_arr, g_arr, la, gain, s0, tri, bmask, *extra)


def _state_to_pairs(s):
    nb = s.shape[0]
    st = jnp.swapaxes(s, -1, -2).reshape(nb, 2, 2, LANES, HEAD)
    z = jnp.zeros_like(st[:, :, 0])
    top = jnp.concatenate([st[:, :, 0], z], axis=-1)
    bot = jnp.concatenate([z, st[:, :, 1]], axis=-1)
    return jnp.concatenate([top, bot], axis=-2)


def _pairs_to_state(sp):
    nb = sp.shape[0]
    h0 = sp[:, :, :LANES, :HEAD]
    h1 = sp[:, :, LANES:, HEAD:]
    st = jnp.stack([h0, h1], axis=2).reshape(nb, 4, LANES, HEAD)
    return jnp.swapaxes(st, -1, -2)


def _cumsum_kernel(x_ref, tri_ref, o_ref, carry_ref):
    @pl.when(pl.program_id(1) == 0)
    def _():
        carry_ref[...] = jnp.zeros_like(carry_ref)

    tri = tri_ref[...]
    a1, a2, a3 = _split3(x_ref[...])
    c = _dot(tri, a1) + _dot(tri, a2) + _dot(tri, a3) + carry_ref[0:1, :]
    o_ref[...] = c * LOG2E
    carry_ref[...] = jnp.broadcast_to(c[-1:, :], carry_ref.shape)


def row_cumsum(x, *, row0, nb, n, tb):
    r = np.arange(tb)
    tri = jnp.asarray((r[None, :] <= r[:, None]).astype(np.float32), BF16)
    nblk = n // tb
    blk0 = row0 // tb
    return pl.pallas_call(
        _cumsum_kernel,
        grid=(nb, nblk),
        in_specs=[pl.BlockSpec((tb, LANES), lambda b, j: (blk0 + b * nblk + j, 0)),
                  pl.BlockSpec((tb, tb), lambda b, j: (0, 0))],
        out_specs=pl.BlockSpec((tb, LANES), lambda b, j: (b * nblk + j, 0)),
        out_shape=jax.ShapeDtypeStruct((nb * n, LANES), F32),
        scratch_shapes=[pltpu.VMEM((SUBLANES, LANES), F32)],
        compiler_params=_cparams(("parallel", "arbitrary")),
        name="logf_cumsum",
    )(x, tri)


def _col_reduce(x, op):
    tk, tq = x.shape
    if tk > FOLD and tk % FOLD == 0:
        x = op(x.reshape(tk // FOLD, FOLD, tq), axis=0)
    return op(x, axis=0, keepdims=True)


def _values_with_ones(vt_head):
    ones = jnp.ones((ONES_ROWS, vt_head.shape[1]), F32)
    return jnp.concatenate([vt_head, ones], axis=0).astype(BF16)


def _attn_update(s, smax, vt_ext, m_ref, acc_ref, idx):
    m_prev = m_ref[idx]
    m_new = jnp.maximum(m_prev, smax)
    alpha = jnp.exp2(m_prev - m_new)
    p = jnp.exp2(s - m_new)
    acc_ref[idx] = alpha * acc_ref[idx] + _dot(vt_ext, p.astype(BF16))
    m_ref[idx] = m_new


def _eye(n):
    r = lax.broadcasted_iota(I32, (n, n), 0)
    c = lax.broadcasted_iota(I32, (n, n), 1)
    return jnp.where(r == c, 1.0, 0.0).astype(BF16)


def _transpose_exact(x, eye):
    a1, a2, a3 = _split3(x)
    return _dot_nt(eye, a1) + _dot_nt(eye, a2) + _dot_nt(eye, a3)


def _write_heads(o_ref, acc_ref, tq):
    eye = _eye(tq)
    for c in range(4):
        a0, a1 = acc_ref[2 * c], acc_ref[2 * c + 1]
        ot = jnp.concatenate([a0[:HEAD] / a0[HEAD:HEAD + 1], a1[:HEAD] / a1[HEAD:HEAD + 1]], axis=0)
        o_ref[:, c * LANES:(c + 1) * LANES] = _transpose_exact(ot, eye)


def _fox_kernel(qi_ref, kj_ref, q_ref, k_ref, v_ref, c_ref, o_ref, m_ref, acc_ref,
                *, tq, tk, q_off):
    s_id = pl.program_id(1)
    qi = qi_ref[s_id]
    kj = kj_ref[s_id]
    j_last = (q_off + (qi + 1) * tq - 1) // tk

    @pl.when(kj == 0)
    def _():
        m_ref[...] = jnp.full_like(m_ref, NEG)
        acc_ref[...] = jnp.zeros_like(acc_ref)

    def block(masked):
        visible = None
        if masked:
            kpos = kj * tk + lax.broadcasted_iota(I32, (tk, tq), 0)
            qpos = q_off + qi * tq + lax.broadcasted_iota(I32, (tk, tq), 1)
            visible = kpos <= qpos
        qs, ks = _pair_masked_operands(q_ref[...], k_ref[...])
        v = v_ref[...]
        vts = []
        for p in range(4):
            vt = v[:, p * LANES:(p + 1) * LANES].T
            vts += [vt[:HEAD], vt[HEAD:]]
        _fox_block(qs, ks, vts, c_ref[...], visible, m_ref, acc_ref)

    @pl.when(kj < j_last)
    def _():
        block(False)

    @pl.when(kj == j_last)
    def _():
        block(True)
        _write_heads(o_ref, acc_ref, tq)


def _fox_bounded_kernel(qi_ref, kj_ref, q_ref, k_ref, v_ref, c_ref, bnd_ref, o_ref, m_ref, acc_ref,
                        *, tq, tk, q_off):
    s_id = pl.program_id(1)
    qi = qi_ref[s_id]
    kj = kj_ref[s_id]
    j_last = (q_off + (qi + 1) * tq - 1) // tk

    @pl.when(kj == 0)
    def _():
        acc_ref[...] = jnp.zeros_like(acc_ref)
        bt = _transpose_exact(bnd_ref[...], _eye(LANES))
        for h in range(B_HEADS):
            m_ref[h] = bt[h:h + 1, :]

    def block(masked):
        qs, ks = _pair_masked_operands(q_ref[...], k_ref[...])
        c = c_ref[...]
        if masked:
            kpos = kj * tk + lax.broadcasted_iota(I32, (tk, tq), 0)
            qpos = q_off + qi * tq + lax.broadcasted_iota(I32, (tk, tq), 1)
            visible = kpos <= qpos
        probs = []
        for h in range(B_HEADS):
            e = _dot_nt(ks[h], qs[h]) - c[:, h:h + 1] - m_ref[h]
            if masked:
                e = jnp.where(visible, e, NEG)
            probs.append(jnp.exp2(e).astype(BF16))
        v = v_ref[...]
        for p in range(4):
            vt = v[:, p * LANES:(p + 1) * LANES].T
            for hh in range(2):
                h = 2 * p + hh
                acc_ref[h] = acc_ref[h] + _dot(_values_with_ones(vt[hh * HEAD:(hh + 1) * HEAD]), probs[h])

    @pl.when(kj < j_last)
    def _():
        block(False)

    @pl.when(kj == j_last)
    def _():
        block(True)
        _write_heads(o_ref, acc_ref, tq)


def _pair_masked_operands(q, k):
    hm = _half_masks()
    qs, ks = [], []
    for p in range(4):
        kp = k[:, p * LANES:(p + 1) * LANES].astype(BF16)
        for hh in range(2):
            qs.append((q[:, p * LANES:(p + 1) * LANES] * hm[hh]).astype(BF16))
            ks.append(kp)
    return qs, ks


def _fox_block(qs, ks, vts, c, visible, m_ref, acc_ref):
    scores, maxima = [], []
    for h in range(B_HEADS):
        s = _dot_nt(ks[h], qs[h]) - c[:, h:h + 1]
        if visible is not None:
            s = jnp.where(visible, s, NEG)
        scores.append(s)
        maxima.append(_col_reduce(s, jnp.max))
    for h in range(B_HEADS):
        _attn_update(scores[h], maxima[h], _values_with_ones(vts[h]), m_ref, acc_ref, h)


def _fox_decode_kernel(q_ref, kp_ref, vp_ref, lp_ref, kn_ref, vn_ref, ln_ref, tri_ref, o_ref,
                       m_ref, acc_ref, *, ts, past, tb):
    m_ref[...] = jnp.full_like(m_ref, NEG)
    acc_ref[...] = jnp.zeros_like(acc_ref)
    eye = _eye(LANES)
    tri = tri_ref[...]

    def cumsum(x, tri_blk, carry):
        a1, a2, a3 = _split3(x)
        return _dot(tri_blk, a1) + _dot(tri_blk, a2) + _dot(tri_blk, a3) + carry

    def transposed(v):
        return _dot_nt(eye[:v.shape[1], :v.shape[1]], v.astype(BF16))

    lp = lp_ref[...]
    carry = jnp.zeros((1, lp.shape[1]), F32)
    c_blocks = []
    for j in range(past // tb):
        cb = cumsum(lp[j * tb:(j + 1) * tb], tri, carry)
        carry = cb[-1:, :]
        c_blocks.append(cb)
    c_past = jnp.concatenate(c_blocks, axis=0) * LOG2E
    c_new = cumsum(ln_ref[:, :B_HEADS], tri[:ts, :ts], carry) * LOG2E
    q = q_ref[...]
    def attend(k, v, c, visible):
        qs, ks = _pair_masked_operands(q, k)
        vts = []
        for p in range(4):
            vt = transposed(v[:, p * LANES:(p + 1) * LANES])
            vts += [vt[:HEAD], vt[HEAD:]]
        _fox_block(qs, ks, vts, c, visible, m_ref, acc_ref)

    attend(kp_ref[...], vp_ref[...], c_past, None)
    kpos = lax.broadcasted_iota(I32, (ts, ts), 0)
    qpos = lax.broadcasted_iota(I32, (ts, ts), 1)
    attend(kn_ref[...], vn_ref[...], c_new, kpos <= qpos)
    _write_heads(o_ref, acc_ref, ts)


def fox_decode(q_arr, k_past, v_past, lf_past, k_new, v_new, v_col, lf_new, *, row0, nb, ts, past,
               out_rows, into, name):
    tb = _pick_tile(past, (256, 128, 64))
    r = np.arange(tb)
    tri = jnp.asarray((r[None, :] <= r[:, None]).astype(np.float32), BF16)
    blk0 = row0 // ts
    new = lambda w, c: pl.BlockSpec((ts, w), lambda b, c=c: (blk0 + b, c))
    old = pl.BlockSpec((past, 512), lambda b: (b, 0))
    o_shape, extra, extra_specs = _shared_rows(into, out_rows, 512)
    n_in = 8
    kern = functools.partial(_fox_decode_kernel, ts=ts, past=past, tb=tb)
    return pl.pallas_call(
        _ignore_last_input(kern, n_in) if extra else kern,
        grid=(nb,),
        in_specs=[new(512, 0), old, old, pl.BlockSpec((past, B_HEADS), lambda b: (b, 0)),
                  new(512, 0), new(512, v_col),
                  new(LANES, 0), pl.BlockSpec((tb, tb), lambda b: (0, 0))] + extra_specs,
        out_specs=new(512, 0),
        out_shape=o_shape,
        scratch_shapes=[pltpu.VMEM((8, 1, ts), F32), pltpu.VMEM((8, HEAD + ONES_ROWS, ts), F32)],
        input_output_aliases={n_in: 0} if extra else {},
        compiler_params=_cparams(("parallel",)),
        name=name,
    )(q_arr, k_past, v_past, lf_past, k_new, v_new, lf_new, tri, *extra)


def fox_attention(q_arr, k_arr, v_arr, v_col, c_arr, bound, *, q_row0, nb, tq_total, tk_total, q_off,
                  tq, tk, out_rows, into, name):
    nq = tq_total // tq
    steps = [(i, j) for i in range(nq) for j in range((q_off + (i + 1) * tq - 1) // tk + 1)]
    qi = jnp.asarray([s[0] for s in steps], I32)
    kj = jnp.asarray([s[1] for s in steps], I32)
    qb0 = q_row0 // tq
    nkb = tk_total // tk
    o_shape, extra, extra_specs = _shared_rows(into, out_rows, 512)
    qrows = lambda w: pl.BlockSpec((tq, w), lambda b, s, qi, kj: (qb0 + b * nq + qi[s], 0))
    operands = [q_arr, k_arr, v_arr, c_arr]
    in_specs = [qrows(512),
                pl.BlockSpec((tk, 512), lambda b, s, qi, kj: (b * nkb + kj[s], 0)),
                pl.BlockSpec((tk, 512), lambda b, s, qi, kj: (b * nkb + kj[s], v_col)),
                pl.BlockSpec((tk, LANES), lambda b, s, qi, kj: (b * nkb + kj[s], 0))]
    body = _fox_kernel
    if bound is not None:
        operands.append(bound)
        in_specs.append(qrows(LANES))
        body = _fox_bounded_kernel
    n_in = 2 + len(operands)
    grid_spec = pltpu.PrefetchScalarGridSpec(
        num_scalar_prefetch=2,
        grid=(nb, len(steps)),
        in_specs=in_specs + extra_specs,
        out_specs=qrows(512),
        scratch_shapes=[pltpu.VMEM((8, 1, tq), F32), pltpu.VMEM((8, HEAD + ONES_ROWS, tq), F32)],
    )
    kern = functools.partial(body, tq=tq, tk=tk, q_off=q_off)
    return pl.pallas_call(
        _ignore_last_input(kern, n_in) if extra else kern,
        grid_spec=grid_spec,
        out_shape=o_shape,
        input_output_aliases={n_in: 0} if extra else {},
        compiler_params=_cparams(("parallel", "arbitrary")),
        name=name,
    )(qi, kj, *operands, *extra)


def _head_sum_matrix():
    r = lax.broadcasted_iota(I32, (4 * LANES, LANES), 0)
    c = lax.broadcasted_iota(I32, (4 * LANES, LANES), 1)
    return jnp.where(r // HEAD == c, 1.0, 0.0).astype(BF16)


def _sortable_key(x):
    b = lax.bitcast_convert_type(x + 0.0, I32)
    return jnp.where(b < 0, b ^ jnp.int32(0x7FFFFFFF), b)


def _dsa_kernel(iq_ref, misc_ref, qc_ref, kc_ref, vc_ref, ik_ref, o_ref,
                keys_ref, half_ref, qm_ref, m_ref, acc_ref, thr_ref, jcut_ref, cnt_ref,
                *, tq, tk, n_keys, q_off, ksel):
    i = pl.program_id(1)
    hm = _half_masks()
    qcol = lax.broadcasted_iota(I32, (1, tq), 1)
    lim = jnp.minimum(((q_off + i * tq + qcol) // CHUNK + 1) * CHUNK, n_keys)
    lim_max = jnp.minimum(((q_off + (i + 1) * tq - 1) // CHUNK + 1) * CHUNK, n_keys)
    nkb = (lim_max + tk - 1) // tk
    eye = _eye(LANES)

    def key_pos(kb):
        return kb * tk + lax.broadcasted_iota(I32, (tk, tq), 0)

    iq = iq_ref[...]
    for hd in range(IDX_HEADS):
        c0 = (hd // 2) * LANES
        qm_ref[hd] = (iq[:, c0:c0 + LANES] * hm[hd % 2]).astype(BF16)
    iwt = _transpose_exact(misc_ref[...], eye) * IDX_SCALE

    def score_body(kb, kk_max):
        k0 = pl.multiple_of(kb * tk, tk)
        ikb = ik_ref[pl.ds(k0, tk), :]
        sc = jnp.zeros((tk, tq), F32)
        for hd in range(IDX_HEADS):
            a = _dot_nt(ikb, qm_ref[hd])
            sc = sc + jnp.maximum(a, 0.0) * iwt[MISC_IW + hd:MISC_IW + hd + 1, :]
        key = jnp.where(key_pos(kb) < lim, _sortable_key(sc), INT_MIN)
        keys_ref[kb] = key
        half_ref[kb] = jnp.right_shift(key, 16).astype(I16)
        kf32 = kc_ref[pl.ds(k0, tk), :].astype(F32)
        kk = _dot((kf32 * kf32).astype(BF16), half_ones)
        return jnp.maximum(kk_max, _col_reduce(kk, jnp.max))

    lane_r = lax.broadcasted_iota(I32, (LANES, LANES), 0)
    lane_c = lax.broadcasted_iota(I32, (LANES, LANES), 1)
    half_ones = jnp.where(lane_r // HEAD == lane_c // HEAD, 1.0, 0.0).astype(BF16)
    kk_max = lax.fori_loop(0, nkb, score_body, jnp.zeros((1, LANES), F32))

    def count(pred):
        def body(kb, acc):
            hit = jnp.where(pred(keys_ref[kb], key_pos(kb)), 1.0, 0.0)
            return acc + jnp.sum(hit.reshape(tk // COUNT_FOLD, COUNT_FOLD, tq), axis=0)
        acc = lax.fori_loop(0, nkb, body, jnp.zeros((COUNT_FOLD, tq), F32))
        return jnp.sum(acc, axis=0, keepdims=True)

    def count_half(cand):
        def body(kb, acc):
            hit = jnp.where(half_ref[kb] >= cand, jnp.int16(1), jnp.int16(0))
            parts = [hit[r * COUNT_FOLD:(r + 1) * COUNT_FOLD] for r in range(tk // COUNT_FOLD)]
            while len(parts) > 1:
                parts = [a + b for a, b in zip(parts[0::2], parts[1::2])] + parts[len(parts) & ~1:]
            return acc + parts[0]
        acc = lax.fori_loop(0, nkb, body, jnp.zeros((COUNT_FOLD, tq), I16))
        return jnp.sum(acc.astype(F32), axis=0, keepdims=True)

    def any_row(flag):
        return jnp.max(jnp.where(flag, 1.0, 0.0)) > 0.0

    thr_ref[...] = jnp.full((1, tq), INT_MIN + 1, I32)
    jcut_ref[...] = jnp.full((1, tq), 2 ** 30, I32)
    kf = float(ksel)

    @pl.when(lim_max > ksel)
    def _():
        few = lim <= ksel

        def unsettled(cres):
            return jnp.logical_and(cres != kf, jnp.logical_not(few))

        def search(first, last, group, to_half, state):
            assert (last - first) % group == 0

            def cond(state):
                t, _, _, go = state
                return jnp.logical_and(t < last, go)

            def body(state):
                t, res, cres, _ = state
                for g in range(group):
                    cand = res + jnp.left_shift(jnp.int32(1), 30 - g - t)
                    cnt = count_half(to_half(cand))
                    take = cnt >= kf
                    res = jnp.where(take, cand, res)
                    cres = jnp.where(take, cnt, cres)
                return t + group, res, cres, any_row(unsettled(cres))

            return lax.while_loop(cond, body, (jnp.int32(first),) + state)[1:]

        def high_half(x):
            return jnp.right_shift(x, 16).astype(I16)

        def low_half(x):
            return ((x & 0xFFFF) - 32768).astype(I16)

        def count_sample(cand):
            def body(kb, acc):
                hit = jnp.where(half_ref[kb, 0:SAMPLE_ROWS, :] >= cand, jnp.int16(1), jnp.int16(0))
                return acc + hit[:COUNT_FOLD] + hit[COUNT_FOLD:]
            acc = lax.fori_loop(0, nkb, body, jnp.zeros((COUNT_FOLD, tq), I16))
            return jnp.sum(acc.astype(F32), axis=0, keepdims=True)

        ks = float(max(1, -(-ksel * SAMPLE_ROWS // tk)))
        s0 = count_sample(jnp.zeros((1, tq), I16))
        guess0 = jnp.where(s0 >= ks, 0, -32768).astype(I32)

        def guess_body(t, guess):
            cand = guess + jnp.left_shift(jnp.int32(1), 14 - t)
            return jnp.where(count_sample(cand.astype(I16)) >= ks, cand, guess)

        guess = lax.fori_loop(0, 15, guess_body, guess0)
        lo_h = jnp.maximum(guess - SAMPLE_SPAN, -32767)
        hi_h = jnp.minimum(guess + SAMPLE_SPAN, 32767)
        c_lo = count_half(lo_h.astype(I16))
        c_hi = count_half(hi_h.astype(I16))
        bracketed = jnp.logical_not(any_row(jnp.logical_or(c_lo < kf, c_hi >= kf)))

        @pl.when(bracketed)
        def _():
            def mid_body(t, state):
                lo, hi, cnt_lo = state
                mid = jnp.right_shift(lo + hi, 1)
                cnt = count_half(mid.astype(I16))
                take = cnt >= kf
                return (jnp.where(take, mid, lo), jnp.where(take, hi, mid),
                        jnp.where(take, cnt, cnt_lo))

            steps = int(np.ceil(np.log2(2 * SAMPLE_SPAN)))
            lo, _, cnt_lo = lax.fori_loop(0, steps, mid_body, (lo_h, hi_h, c_lo))
            thr_ref[...] = jnp.left_shift(lo, 16)
            cnt_ref[...] = cnt_lo

        @pl.when(jnp.logical_not(bracketed))
        def _():
            c0 = count_half(jnp.zeros((1, tq), I16))
            res = jnp.where(c0 >= kf, 0, INT_MIN).astype(I32)
            cres = jnp.where(c0 >= kf, c0, 2.0 * n_keys)
            res, cres, _ = search(0, 15, 3, high_half, (res, cres, any_row(unsettled(cres))))
            thr_ref[...] = res
            cnt_ref[...] = cres

        res = thr_ref[...]
        cres = cnt_ref[...]
        go = any_row(unsettled(cres))

        @pl.when(go)
        def _():
            rh = high_half(res)

            def relabel(kb, carry):
                hi = half_ref[kb]
                lo = low_half(keys_ref[kb])
                half_ref[kb] = jnp.where(hi > rh, jnp.int16(32767),
                                         jnp.where(hi < rh, jnp.int16(-32768), lo))
                return carry

            lax.fori_loop(0, nkb, relabel, 0)

        res, cres, open_rows = search(15, 31, 4, low_half, (res, cres, go))
        thr = jnp.maximum(res, INT_MIN + 1)
        thr_ref[...] = thr

        @pl.when(open_rows)
        def _():
            need = kf - count(lambda blk, kpos: blk > thr)
            n_eq = count(lambda blk, kpos: blk == thr)
            split = n_eq > need

            @pl.when(any_row(split))
            def _():
                def idx_body(t, lo):
                    cand = lo + jnp.left_shift(jnp.int32(1), 14 - t)
                    cnt = count(lambda blk, kpos: (blk == thr) & (kpos < cand))
                    return jnp.where(cnt < need, cand, lo)

                lo = lax.fori_loop(0, 15, idx_body, jnp.zeros((1, tq), I32))
                jcut_ref[...] = jnp.where(split, lo, 2 ** 30)

    qc = qc_ref[...]
    for hd in range(C_HEADS):
        c0 = (hd // 2) * LANES
        qm_ref[hd] = (qc[:, c0:c0 + LANES] * hm[hd % 2]).astype(BF16)
    thr = thr_ref[...]
    jcut = jcut_ref[...]

    def block_operands(kb):
        k0 = pl.multiple_of(kb * tk, tk)
        kblk = kc_ref[pl.ds(k0, tk), :]
        vt = _dot_nt(eye, vc_ref[pl.ds(k0, tk), :])
        vts = [_values_with_ones(vt[hh * HEAD:(hh + 1) * HEAD]) for hh in range(2)]
        keys = keys_ref[kb]
        sel = (keys > thr) | ((keys == thr) & (key_pos(kb) <= jcut))
        return kblk, vts, sel

    qq = _dot((qc * qc).astype(BF16), _head_sum_matrix())
    qq_t = _dot_nt(eye, qq.astype(BF16))
    bound = [jnp.sqrt(qq_t[hd:hd + 1, :] * kk_max[:, (hd % 2) * HEAD:(hd % 2) * HEAD + 1]) * BOUND_SLACK
             for hd in range(C_HEADS)]
    acc_ref[...] = jnp.zeros_like(acc_ref)

    def fast_body(kb, carry):
        kblk, vts, sel = block_operands(kb)
        keep = jnp.where(sel, 1.0, 0.0).astype(BF16)
        probs = [jnp.exp2(_dot_nt(kblk, qm_ref[hd]) - bound[hd]).astype(BF16) * keep
                 for hd in range(C_HEADS)]
        for hd in range(C_HEADS):
            acc_ref[hd] = acc_ref[hd] + _dot(vts[hd % 2], probs[hd])
        return carry

    lax.fori_loop(0, nkb, fast_body, 0)
    denom_min = acc_ref[0][HEAD:HEAD + 1, :]
    for hd in range(1, C_HEADS):
        denom_min = jnp.minimum(denom_min, acc_ref[hd][HEAD:HEAD + 1, :])

    @pl.when(jnp.min(denom_min) < DENOM_FLOOR)
    def _():
        m_ref[...] = jnp.full_like(m_ref, NEG)
        acc_ref[...] = jnp.zeros_like(acc_ref)

        def exact_body(kb, carry):
            kblk, vts, sel = block_operands(kb)
            scores = [jnp.where(sel, _dot_nt(kblk, qm_ref[hd]), NEG) for hd in range(C_HEADS)]
            maxima = [_col_reduce(s, jnp.max) for s in scores]
            for hd in range(C_HEADS):
                _attn_update(scores[hd], maxima[hd], vts[hd % 2], m_ref, acc_ref, hd)
            return carry

        lax.fori_loop(0, nkb, exact_body, 0)

    _write_heads(o_ref, acc_ref, tq)


def dsa_attention(iq_arr, misc_arr, misc_col, qc_arr, kc, vc, ik, *, q_row0, nb, tq_total,
                  n_keys, n_keys_pad, q_off, tq, tk, out_rows, into, name):
    nq = tq_total // tq
    qb0 = q_row0 // tq
    ksel = min(TOPK_MAX, n_keys // 4)
    qspec = lambda w, c: pl.BlockSpec((tq, w), lambda b, i, c=c: (qb0 + b * nq + i, c))
    kspec = pl.BlockSpec((n_keys_pad, LANES), lambda b, i: (b, 0))
    o_shape, extra, extra_specs = _shared_rows(into, out_rows, 512)
    n_in = 6
    kern = functools.partial(_dsa_kernel, tq=tq, tk=tk, n_keys=n_keys, q_off=q_off, ksel=ksel)
    return pl.pallas_call(
        _ignore_last_input(kern, n_in) if extra else kern,
        grid=(nb, nq),
        in_specs=[qspec(512, 0), qspec(LANES, misc_col), qspec(512, 0), kspec, kspec, kspec]
        + extra_specs,
        out_specs=qspec(512, 0),
        out_shape=o_shape,
        input_output_aliases={n_in: 0} if extra else {},
        scratch_shapes=[pltpu.VMEM((n_keys_pad // tk, tk, tq), I32),
                        pltpu.VMEM((n_keys_pad // tk, tk, tq), I16),
                        pltpu.VMEM((8, tq, LANES), BF16),
                        pltpu.VMEM((8, 1, tq), F32),
                        pltpu.VMEM((8, HEAD + ONES_ROWS, tq), F32),
                        pltpu.VMEM((1, tq), I32), pltpu.VMEM((1, tq), I32),
                        pltpu.VMEM((1, tq), F32)],
        compiler_params=_cparams(("parallel", "arbitrary")),
        name=name,
    )(iq_arr, misc_arr, qc_arr, kc, vc, ik, *extra)


def _rope_tables(pos):
    half = HEAD // 2
    inv = jnp.power(ROPE_THETA, -jnp.arange(half, dtype=F32) / half)
    ang = pos.astype(F32)[:, None] * inv[None, :]
    cos, sin = jnp.cos(ang), jnp.sin(ang)
    z = jnp.zeros_like(sin)
    return (jnp.tile(cos, (1, 4)), jnp.tile(jnp.concatenate([-sin, z], 1), (1, 2)),
            jnp.tile(jnp.concatenate([z, sin], 1), (1, 2)))


def _head_mean_matrix():
    r = np.arange(512)
    return jnp.asarray((r[:, None] // HEAD == r[None, :] // HEAD).astype(np.float32), BF16)


def _pick_tile(n, prefs):
    for t in prefs:
        if n % t == 0:
            return t
    raise ValueError(f"no tile in {prefs} divides {n}")


def _pad_keys(x, n_pad):
    nb, n, w = x.shape
    if n_pad > n:
        x = jnp.concatenate([x, jnp.zeros((nb, n_pad - n, w), x.dtype)], axis=1)
    return x.reshape(nb * n_pad, w)


def kernel(x_prompt, x_sample, state_a, cache_b_k, cache_b_v, cache_b_logf, cache_c_k, cache_c_v,
           cache_c_idx, state_d, norm_mix, norm_ffn, even_w_in, even_b_f, even_q_norm, even_k_norm,
           even_w_out, odd_w_in, odd_w_gate_up, odd_b_gate, odd_q_norm, odd_k_norm, odd_o_norm,
           odd_w_out, ffn_w1, ffn_w2):
    bp, tp = x_prompt.shape[:2]
    nb, ts = x_sample.shape[:2]
    past = cache_b_k.shape[2]
    depth = norm_mix.shape[0]
    rp, rs = bp * tp, nb * ts
    rows = rp + rs
    assert bp == 1 and tp % 128 == 0 and ts % SUB == 0 and rp % ts == 0 and past % CHUNK == 0

    y = jnp.concatenate([x_prompt.reshape(rp, D_MODEL), x_sample.reshape(rs, D_MODEL)], axis=0)
    pos = jnp.concatenate([jnp.arange(tp, dtype=I32),
                           jnp.tile(past + jnp.arange(ts, dtype=I32), nb)])
    cos, slo, shi = _rope_tables(pos)
    gmat = _head_mean_matrix()

    tm = _pick_tile(rows, (512, 256, 128, 64, 32))
    tm_prep = _pick_tile(rows, (256, 128, 64, 32))
    ts_p = _pick_tile(tp, (128, 64, 32, 16))
    tq_fox = _pick_tile(tp, (512, 256, 128))
    tk_fox = _pick_tile(tp, (512, 256, 128))
    tq_dsa = _pick_tile(tp, (256, 128))
    tk_dsa = _pick_tile(tp, (512, 256, 128))
    nk_s = past + ts
    nk_s_pad = -(-nk_s // 384) * 384
    tb_p = _pick_tile(tp, (512, 256, 128))

    lg = jnp.log1p(-jnp.exp2(-5.0 - jnp.arange(A_HEADS, dtype=F32)))
    la_ret = jnp.repeat(lg, A_DK)[None, :]
    ones_gain = jnp.ones((1, 512), F32)
    zero_state = jnp.zeros((bp, 2, 256, LANES), F32)

    a_p, a_s, bk_p, bk_s, bv_p, bv_s, bf_p, bf_s = [], [], [], [], [], [], [], []
    ck_p, ck_s, cv_p, cv_s, ci_p, ci_s, d_p, d_s = [], [], [], [], [], [], [], []

    for l in range(depth):
        i = l // 2
        if l % 2 == 0:
            w_in = jnp.concatenate(
                [even_w_in[i], jnp.zeros((D_MODEL, EV_WIDTH - even_w_in.shape[2]), F32)], 1).astype(BF16)
            h = norm_matmul(y, norm_mix[l], w_in, tm=tm, tn=EV_WIDTH, name="even_in_proj")
            bf = jnp.concatenate([even_b_f[i], jnp.zeros((LANES - B_HEADS,), F32)])[None, :]
            qka, qbn, kbn, lf, qq, kk, qk = prep_even(
                h, cos, slo, shi, jnp.tile(even_q_norm[i], B_HEADS)[None, :],
                jnp.tile(even_k_norm[i], B_HEADS)[None, :], bf, gmat, tm=tm_prep)
            oa, sa_p = linear_scan(qka, 0, qka, 1, h, EV_VA // 512, h, EV_GA // 512,
                                   jnp.broadcast_to(la_ret, (ts_p, 256)), ones_gain, zero_state,
                                   row0=0, nb=bp, t=tp, ts=ts_p, la_shared=True, out_rows=rows,
                                   into=None, name="retention_prompt")
            oa, sa_s = linear_scan(qka, 0, qka, 1, h, EV_VA // 512, h, EV_GA // 512,
                                   jnp.broadcast_to(la_ret, (ts, 256)), ones_gain,
                                   _state_to_pairs(state_a[i]),
                                   row0=rp, nb=nb, t=ts, ts=ts, la_shared=True, out_rows=rows,
                                   into=oa, name="retention_decode")
            lf8 = lf[:, :B_HEADS]
            c_p = row_cumsum(lf, row0=0, nb=bp, n=tp, tb=tb_p)
            qk_bound = jnp.sqrt(qq[:rp] * jnp.max(kk[:rp], axis=0, keepdims=True)) * BOUND_SLACK
            tight = jnp.max((qk_bound - qk[:rp])[:, :B_HEADS]) < FOX_BOUND_RANGE
            fox_prompt = functools.partial(
                fox_attention, qbn, kbn, h, EV_VB // 512, c_p, q_row0=0, nb=bp, tq_total=tp,
                tk_total=tp, q_off=0, tq=tq_fox, tk=tk_fox, out_rows=rows, into=None)
            ob = lax.cond(tight,
                          lambda: fox_prompt(qk_bound - c_p, name="fox_prompt_bounded"),
                          lambda: fox_prompt(None, name="fox_prompt"))
            kb_new = kbn[rp:].reshape(nb, ts, 512)
            vb_new = h[rp:, EV_VB:EV_VB + 512].reshape(nb, ts, 512)
            ob = fox_decode(qbn, cache_b_k[i].reshape(nb * past, 512),
                            cache_b_v[i].reshape(nb * past, 512),
                            cache_b_logf[i].reshape(nb * past, B_HEADS), kbn, h, EV_VB // 512, lf,
                            row0=rp, nb=nb, ts=ts, past=past, out_rows=rows, into=ob,
                            name="fox_decode")
            w_out = even_w_out[i].astype(BF16)
            y = matmul_residual(y, [oa, ob], [w_out[:512], w_out[512:]], tm=tm, name="even_out_proj")
            a_p.append(_pairs_to_state(sa_p))
            a_s.append(_pairs_to_state(sa_s))
            bk_p.append(kbn[:rp].reshape(bp, tp, B_HEADS, B_DH))
            bk_s.append(kb_new.reshape(nb, ts, B_HEADS, B_DH))
            bv_p.append(h[:rp, EV_VB:EV_VB + 512].reshape(bp, tp, B_HEADS, B_DH))
            bv_s.append(vb_new.reshape(nb, ts, B_HEADS, B_DH))
            bf_p.append(lf8[:rp].reshape(bp, tp, B_HEADS))
            bf_s.append(lf8[rp:].reshape(nb, ts, B_HEADS))
        else:
            w = odd_w_in[i]
            offs = np.cumsum([0, 512, 128, 128, 512, 64, 8, 256, 256, 512, 512, 16])
            qc_w, kc_w, vc_w, iq_w, ik_w, iw_w, qd_w, kd_w, vd_w, gd_w, gr_w = [
                w[:, int(a):int(b)] for a, b in zip(offs[:-1], offs[1:])]
            qc_w = qc_w.reshape(D_MODEL, 2, 4, C_DH).transpose(0, 2, 1, 3).reshape(D_MODEL, 512)
            w_in = jnp.concatenate(
                [qc_w, iq_w, vd_w, gd_w, qd_w, kd_w, kc_w, vc_w, ik_w, iw_w, gr_w,
                 jnp.zeros((D_MODEL, OD_WIDTH - OD_MISC - 88), F32)], 1).astype(BF16)
            h = norm_matmul(y, norm_mix[l], w_in, tm=tm, tn=OD_WIDTH, name="odd_in_proj")
            wg = jnp.zeros((LANES, 256), F32).at[MISC_GR:MISC_GR + D_GATE_RANK].set(
                odd_w_gate_up[i]).astype(BF16)
            qcr, iqr, qkd, kcr, ikr, kcb, vcb, ikb, la = prep_odd(
                h, cos, slo, shi, jnp.tile(odd_q_norm[i], C_HEADS)[None, :],
                jnp.tile(odd_k_norm[i], C_KV_HEADS)[None, :], wg, odd_b_gate[i][None, :], gmat,
                tm=tm_prep)
            oc = dsa_attention(iqr, h, OD_MISC // LANES, qcr, kcb, vcb, ikb,
                               q_row0=0, nb=bp, tq_total=tp, n_keys=tp, n_keys_pad=tp, q_off=0,
                               tq=tq_dsa, tk=tk_dsa, out_rows=rows, into=None, name="dsa_prompt")

            def with_past(cache, new):
                return _pad_keys(jnp.concatenate(
                    [cache.reshape(nb, past, -1).astype(BF16), new[rp:].reshape(nb, ts, -1)], 1), nk_s_pad)

            ik_past = jnp.concatenate([cache_c_idx[i], cache_c_idx[i]], axis=-1)
            oc = dsa_attention(iqr, h, OD_MISC // LANES, qcr,
                               with_past(cache_c_k[i], kcb), with_past(cache_c_v[i], vcb),
                               with_past(ik_past, ikb),
                               q_row0=rp, nb=nb, tq_total=ts, n_keys=nk_s, n_keys_pad=nk_s_pad,
                               q_off=past, tq=ts, tk=384, out_rows=rows, into=oc, name="dsa_decode")
            gain = jnp.tile(odd_o_norm[i], D_HEADS)[None, :]
            od, sd_p = linear_scan(qkd, 0, qkd, 1, h, OD_VD // 512, h, OD_GD // 512, la, gain,
                                   zero_state, row0=0, nb=bp, t=tp, ts=ts_p, la_shared=False,
                                   out_rows=rows, into=None, name="gla_prompt")
            od, sd_s = linear_scan(qkd, 0, qkd, 1, h, OD_VD // 512, h, OD_GD // 512, la, gain,
                                   _state_to_pairs(state_d[i]), row0=rp, nb=nb, t=ts, ts=ts,
                                   la_shared=False, out_rows=rows, into=od, name="gla_decode")
            w_out = odd_w_out[i]
            w_oc = w_out[:512].reshape(2, 4, C_DH, D_MODEL).transpose(1, 0, 2, 3).reshape(512, D_MODEL)
            y = matmul_residual(y, [oc, od], [w_oc.astype(BF16), w_out[512:].astype(BF16)], tm=tm,
                                name="odd_out_proj")
            d_p.append(_pairs_to_state(sd_p))
            d_s.append(_pairs_to_state(sd_s))
            ck_p.append(kcr[:rp].reshape(bp, tp, C_KV_HEADS, C_DH))
            ck_s.append(kcr[rp:].reshape(nb, ts, C_KV_HEADS, C_DH))
            cv_p.append(h[:rp, OD_KVC + LANES:OD_KVC + 2 * LANES].reshape(bp, tp, C_KV_HEADS, C_DH))
            cv_s.append(h[rp:, OD_KVC + LANES:OD_KVC + 2 * LANES].reshape(nb, ts, C_KV_HEADS, C_DH))
            ci_p.append(ikr[:rp, :IDX_DIM].reshape(bp, tp, IDX_DIM))
            ci_s.append(ikr[rp:, :IDX_DIM].reshape(nb, ts, IDX_DIM))
        hid = norm_matmul(y, norm_ffn[l], ffn_w1[l].astype(BF16), tm=tm, tn=D_FF, relu2=True,
                          out_dtype=BF16, name="mlp_up")
        y = matmul_residual(y, [hid], [ffn_w2[l].astype(BF16)], tm=tm, name="mlp_down")

    return (y[:rp].reshape(bp, tp, D_MODEL), y[rp:].reshape(nb, ts, D_MODEL),
            jnp.stack(a_p), jnp.stack(a_s), jnp.stack(bk_p), jnp.stack(bk_s),
            jnp.stack(bv_p), jnp.stack(bv_s), jnp.stack(bf_p), jnp.stack(bf_s),
            jnp.stack(ck_p), jnp.stack(ck_s), jnp.stack(cv_p), jnp.stack(cv_s),
            jnp.stack(ci_p), jnp.stack(ci_s), jnp.stack(d_p), jnp.stack(d_s))
```

```python
import functools

import numpy as np
import jax
import jax.numpy as jnp
from jax import lax
from jax.experimental import pallas as pl
from jax.experimental.pallas import tpu as pltpu

F32 = jnp.float32
BF16 = jnp.bfloat16
I32 = jnp.int32
I16 = jnp.int16

D_MODEL = 1024
CHUNK = 64
ROPE_THETA = 10000.0
EPS = 1e-6
A_HEADS, A_DK, A_DV = 4, 64, 128
B_HEADS, B_DH = 8, 64
C_HEADS, C_KV_HEADS, C_DH = 8, 2, 64
IDX_HEADS, IDX_DIM = 8, 64
IDX_SCALE = (IDX_HEADS * IDX_DIM) ** -0.5
TOPK_MAX = 256
D_HEADS, D_DK, D_DV = 4, 64, 128
D_GATE_RANK = 16
D_GATE_NORM = 16.0
D_FF = 4 * D_MODEL

LANES = 128
SUBLANES = 8
VMEM_LIMIT_BYTES = 56 * 1024 * 1024

HEAD = 64
SUB = 16
FOLD = 64
COUNT_FOLD = 32
ONES_ROWS = 16
NEG = -1e30
LOG2E = 1.4426950408889634
BOUND_SLACK = 1.03
DENOM_FLOOR = 2.0 ** -100
FOX_BOUND_RANGE = 90.0
INT_MIN = -(2 ** 31)

EV_QK, EV_VA, EV_GA, EV_QB, EV_KB, EV_VB, EV_FB = 0, 512, 1024, 1536, 2048, 2560, 3072
EV_WIDTH = 3200
OD_QC, OD_IQ, OD_VD, OD_GD, OD_QKD, OD_KVC, OD_MISC = 0, 512, 1024, 1536, 2048, 2560, 2816
OD_WIDTH = 2944
MISC_IK, MISC_IW, MISC_GR = 0, 64, 72


def _cparams(sem):
    return pltpu.CompilerParams(dimension_semantics=sem, vmem_limit_bytes=VMEM_LIMIT_BYTES)


def _dot(a, b):
    return jnp.dot(a, b, preferred_element_type=F32)


def _dot_nt(a, b):
    return lax.dot_general(a, b, (((1,), (1,)), ((), ())), preferred_element_type=F32)


def _dot_tn(a, b):
    return lax.dot_general(a, b, (((0,), (0,)), ((), ())), preferred_element_type=F32)


def _split3(x):
    a1 = x.astype(BF16)
    r1 = x - a1.astype(F32)
    a2 = r1.astype(BF16)
    a3 = (r1 - a2.astype(F32)).astype(BF16)
    return a1, a2, a3


def _half_masks():
    lane = lax.broadcasted_iota(I32, (1, LANES), 1)
    lo = (lane < HEAD).astype(F32)
    return lo, 1.0 - lo


def _log_sigmoid(x):
    return jnp.minimum(x, 0.0) - jnp.log1p(jnp.exp(-jnp.abs(x)))


def _silu(x):
    return x / (1.0 + jnp.exp(-x))


def _norm_mm_kernel(x_ref, g_ref, w_ref, o_ref, *, relu2):
    x = x_ref[...]
    ms = jnp.mean(x * x, axis=-1, keepdims=True)
    xn = (x * lax.rsqrt(ms + EPS) * g_ref[...]).astype(BF16)
    y = _dot(xn, w_ref[...])
    if relu2:
        y = jnp.maximum(y, 0.0)
        y = y * y
    o_ref[...] = y.astype(o_ref.dtype)


def norm_matmul(x, g, w, *, tm, tn, relu2=False, out_dtype=F32, name):
    rows, k = x.shape
    n = w.shape[1]
    return pl.pallas_call(
        functools.partial(_norm_mm_kernel, relu2=relu2),
        grid=(rows // tm, n // tn),
        in_specs=[pl.BlockSpec((tm, k), lambda i, j: (i, 0)),
                  pl.BlockSpec((1, k), lambda i, j: (0, 0)),
                  pl.BlockSpec((k, tn), lambda i, j: (0, j))],
        out_specs=pl.BlockSpec((tm, tn), lambda i, j: (i, j)),
        out_shape=jax.ShapeDtypeStruct((rows, n), out_dtype),
        compiler_params=_cparams(("parallel", "parallel")),
        name=name,
    )(x, g.reshape(1, k), w)


def _mm_res_kernel(*refs, n_in):
    res_ref = refs[0]
    o_ref = refs[-1]
    acc = res_ref[...]
    for a_ref, w_ref in zip(refs[1:1 + n_in], refs[1 + n_in:1 + 2 * n_in]):
        acc = acc + _dot(a_ref[...].astype(BF16), w_ref[...])
    o_ref[...] = acc


def matmul_residual(res, a_list, w_list, *, tm, name):
    rows, n = res.shape
    n_in = len(a_list)
    in_specs = [pl.BlockSpec((tm, n), lambda i: (i, 0))]
    in_specs += [pl.BlockSpec((tm, a.shape[1]), lambda i: (i, 0)) for a in a_list]
    in_specs += [pl.BlockSpec(w.shape, lambda i: (0, 0)) for w in w_list]
    return pl.pallas_call(
        functools.partial(_mm_res_kernel, n_in=n_in),
        grid=(rows // tm,),
        in_specs=in_specs,
        out_specs=pl.BlockSpec((tm, n), lambda i: (i, 0)),
        out_shape=jax.ShapeDtypeStruct((rows, n), F32),
        compiler_params=_cparams(("parallel",)),
        name=name,
    )(res, *a_list, *w_list)


def _tile_lanes(t, width):
    n = width // LANES
    return t if n == 1 else jnp.concatenate([t] * n, axis=1)


def _rope(x, cos, sin_lo, sin_hi):
    w = x.shape[1]
    up = pltpu.roll(x, w - HEAD // 2, 1)
    dn = pltpu.roll(x, HEAD // 2, 1)
    return x * _tile_lanes(cos, w) + up * _tile_lanes(sin_lo, w) + dn * _tile_lanes(sin_hi, w)


def _head_rms(x, gmat):
    s = x * x
    hi = s.astype(BF16)
    lo = (s - hi.astype(F32)).astype(BF16)
    ms = (_dot(hi, gmat) + _dot(lo, gmat)) * (1.0 / HEAD)
    return x * lax.rsqrt(ms + EPS)


def _prep_even_kernel(qk_ref, qb_ref, kb_ref, fb_ref, cos_ref, slo_ref, shi_ref,
                      qg_ref, kg_ref, bf_ref, gmat_ref,
                      qka_ref, qbn_ref, kbn_ref, lf_ref, qq_ref, kk_ref, qk_ref_out):
    cos, slo, shi = cos_ref[...], slo_ref[...], shi_ref[...]
    lane = lax.broadcasted_iota(I32, (1, 4 * LANES), 1)
    kscale = jnp.where(lane < 2 * LANES, 1.0, A_DK ** -0.5)
    qka_ref[...] = _rope(qk_ref[...], cos, slo, shi) * kscale
    gmat = gmat_ref[...]
    qbn = _head_rms(qb_ref[...], gmat) * (qg_ref[...] * (B_DH ** -0.5 * LOG2E))
    kbn = _head_rms(kb_ref[...], gmat) * kg_ref[...]
    qbn_ref[...] = qbn
    kbn_ref[...] = kbn
    lf_ref[...] = _log_sigmoid(fb_ref[...] + bf_ref[...])
    hsum = _head_sum_matrix()
    qq_ref[...] = _dot((qbn * qbn).astype(BF16), hsum)
    kk_ref[...] = _dot((kbn * kbn).astype(BF16), hsum)
    qk_ref_out[...] = _dot((qbn * kbn).astype(BF16), hsum)


def prep_even(h, cos, slo, shi, qg, kg, bf, gmat, *, tm):
    rows = h.shape[0]
    blk = lambda w, c: pl.BlockSpec((tm, w), lambda i, c=c: (i, c))
    row = lambda w: pl.BlockSpec((1, w), lambda i: (0, 0))
    return pl.pallas_call(
        _prep_even_kernel,
        grid=(rows // tm,),
        in_specs=[blk(512, EV_QK // 512), blk(512, EV_QB // 512), blk(512, EV_KB // 512),
                  blk(LANES, EV_FB // LANES), blk(LANES, 0), blk(LANES, 0), blk(LANES, 0),
                  row(512), row(512), row(LANES),
                  pl.BlockSpec((512, 512), lambda i: (0, 0))],
        out_specs=[blk(512, 0), blk(512, 0), blk(512, 0)] + [blk(LANES, 0)] * 4,
        out_shape=[jax.ShapeDtypeStruct((rows, 512), F32)] * 3
        + [jax.ShapeDtypeStruct((rows, LANES), F32)] * 4,
        compiler_params=_cparams(("parallel",)),
        name="prep_even",
    )(h, h, h, h, cos, slo, shi, qg, kg, bf, gmat)


def _prep_odd_kernel(qc_ref, iq_ref, qkd_ref, kvc_ref, misc_ref, cos_ref, slo_ref, shi_ref,
                     qg_ref, kg_ref, wg_ref, bg_ref, gmat_ref,
                     qcr_ref, iqr_ref, qkdo_ref, kcr_ref, ikr_ref, kcb_ref, vcb_ref, ikb_ref,
                     la_ref):
    cos, slo, shi = cos_ref[...], slo_ref[...], shi_ref[...]
    gmat = gmat_ref[...]
    qc = _head_rms(qc_ref[...], gmat) * (qg_ref[...] * (C_DH ** -0.5 * LOG2E))
    qcr_ref[...] = _rope(qc, cos, slo, shi)
    iqr_ref[...] = _rope(iq_ref[...], cos, slo, shi)
    lane = lax.broadcasted_iota(I32, (1, 4 * LANES), 1)
    qkdo_ref[...] = qkd_ref[...] * jnp.where(lane < 2 * LANES, D_DK ** -0.5, 1.0)
    kvc = kvc_ref[...]
    kc = _head_rms(kvc[:, :LANES], gmat[:LANES, :LANES]) * kg_ref[...]
    kcr = _rope(kc, cos, slo, shi)
    kcr_ref[...] = kcr
    kcb_ref[...] = kcr.astype(BF16)
    vcb_ref[...] = kvc[:, LANES:].astype(BF16)
    misc = misc_ref[...]
    ikr = _rope(misc, cos, slo, shi)
    lane1 = lax.broadcasted_iota(I32, (1, LANES), 1)
    ik2 = jnp.where(lane1 < HEAD, ikr, pltpu.roll(ikr, HEAD, 1))
    ikr_ref[...] = ik2
    ikb_ref[...] = ik2.astype(BF16)
    z = _dot(misc.astype(BF16), wg_ref[...]) + bg_ref[...]
    la_ref[...] = _log_sigmoid(z) * (1.0 / D_GATE_NORM)


def prep_odd(h, cos, slo, shi, qg, kg, wg, bg, gmat, *, tm):
    rows = h.shape[0]
    blk = lambda w, c: pl.BlockSpec((tm, w), lambda i, c=c: (i, c))
    row = lambda w: pl.BlockSpec((1, w), lambda i: (0, 0))
    return pl.pallas_call(
        _prep_odd_kernel,
        grid=(rows // tm,),
        in_specs=[blk(512, OD_QC // 512), blk(512, OD_IQ // 512), blk(512, OD_QKD // 512),
                  blk(256, OD_KVC // 256), blk(LANES, OD_MISC // LANES),
                  blk(LANES, 0), blk(LANES, 0), blk(LANES, 0),
                  row(512), row(LANES),
                  pl.BlockSpec((LANES, 256), lambda i: (0, 0)), row(256),
                  pl.BlockSpec((512, 512), lambda i: (0, 0))],
        out_specs=[blk(512, 0), blk(512, 0), blk(512, 0), blk(LANES, 0), blk(LANES, 0),
                   blk(LANES, 0), blk(LANES, 0), blk(LANES, 0), blk(256, 0)],
        out_shape=[jax.ShapeDtypeStruct((rows, 512), F32)] * 3
        + [jax.ShapeDtypeStruct((rows, LANES), F32)] * 2
        + [jax.ShapeDtypeStruct((rows, LANES), BF16)] * 3
        + [jax.ShapeDtypeStruct((rows, 256), F32)],
        compiler_params=_cparams(("parallel",)),
        name="prep_odd",
    )(h, h, h, h, h, cos, slo, shi, qg, kg, wg, bg, gmat)


def _scan_kernel(q_ref, k_ref, v_ref, g_ref, la_ref, gain_ref, s0_ref, tri_ref, bmask_ref,
                 o_ref, sfin_ref, st_ref, *, ts, nsteps):
    step = pl.program_id(1)

    @pl.when(step == 0)
    def _():
        st_ref[...] = s0_ref[0]

    q, k, v = q_ref[...], k_ref[...], v_ref[...]
    tri = tri_ref[...]
    a1, a2, a3 = _split3(la_ref[...])
    cum = _dot(tri, a1) + _dot(tri, a2) + _dot(tri, a3)
    qt = q * jnp.exp(cum)
    kt = k * jnp.exp(-cum)
    hm = _half_masks()
    bmask = bmask_ref[...]
    nsub = ts // SUB
    vb = v.astype(BF16)
    elast, upd = [], []
    for u in range(nsub):
        r0 = u * SUB
        last = cum[r0 + SUB - 1:r0 + SUB, :]
        k2 = (k[r0:r0 + SUB, :] * jnp.exp(last - cum[r0:r0 + SUB, :])).astype(BF16)
        elast.append(jnp.exp(last))
        upd.append([_dot_tn(vb[r0:r0 + SUB, 2 * p * LANES:2 * (p + 1) * LANES],
                            k2[:, p * LANES:(p + 1) * LANES]) * bmask for p in range(2)])
    seen = []
    for p in range(2):
        st = st_ref[p]
        row = []
        for u in range(nsub):
            row.append(st.astype(BF16))
            st = st * elast[u][:, p * LANES:(p + 1) * LANES] + upd[u][p]
        st_ref[p] = st
        seen.append(row)
    qtb = qt.astype(BF16)
    inter = jnp.concatenate(
        [jnp.concatenate([_dot_nt(qtb[u * SUB:(u + 1) * SUB, p * LANES:(p + 1) * LANES], seen[p][u])
                          for p in range(2)], axis=1) for u in range(nsub)], axis=0)
    ktb = kt.astype(BF16)
    same_chunk_causal = tri > 0
    intra = []
    for h in range(4):
        l0 = (h // 2) * LANES
        att = _dot_nt((qt[:, l0:l0 + LANES] * hm[h % 2]).astype(BF16), ktb[:, l0:l0 + LANES])
        att = jnp.where(same_chunk_causal, att, 0.0)
        intra.append(_dot(att.astype(BF16), vb[:, h * LANES:(h + 1) * LANES]))
    o = inter + jnp.concatenate(intra, axis=1)
    g = g_ref[...]
    gain = gain_ref[...]
    outs = []
    for h in range(4):
        oh = o[:, h * LANES:(h + 1) * LANES]
        ms = jnp.mean(oh * oh, axis=-1, keepdims=True)
        outs.append(oh * lax.rsqrt(ms + EPS) * gain[:, h * LANES:(h + 1) * LANES]
                    * _silu(g[:, h * LANES:(h + 1) * LANES]))
    o_ref[...] = jnp.concatenate(outs, axis=1)

    @pl.when(step == nsteps - 1)
    def _():
        sfin_ref[0] = st_ref[...]


def _scan_consts(ts):
    r = np.arange(ts)
    tri = ((r[:, None] // SUB == r[None, :] // SUB) & (r[None, :] <= r[:, None])).astype(np.float32)
    row = np.arange(2 * LANES)[:, None] // LANES
    col = np.arange(LANES)[None, :] // HEAD
    bmask = (row == col).astype(np.float32)
    return jnp.asarray(tri, BF16), jnp.asarray(bmask, F32)


def _ignore_last_input(fn, n_in):
    def wrapped(*refs):
        return fn(*refs[:n_in], *refs[n_in + 1:])
    return wrapped


def _shared_rows(into, rows, width):
    assert into.shape == (rows, width)
    return jax.ShapeDtypeStruct((rows, width), F32), [into], [pl.BlockSpec(memory_space=pl.ANY)]


def linear_scan(q_arr, q_col, k_arr, k_col, v_arr, v_col, g_arr, g_col, la, gain, s0,
                *, row0, nb, t, ts, la_shared, out_rows, into, name):
    nsteps = t // ts
    blk0 = row0 // ts
    tri, bmask = _scan_consts(ts)
    src = lambda w, c: pl.BlockSpec((ts, w), lambda b, s, c=c: (blk0 + b * nsteps + s, c))
    la_spec = pl.BlockSpec((ts, 256), lambda b, s: (0, 0)) if la_shared else src(256, 0)
    o_shape, extra, extra_specs = _shared_rows(into, out_rows, 512)
    n_in = 9
    kern = functools.partial(_scan_kernel, ts=ts, nsteps=nsteps)
    return pl.pallas_call(
        _ignore_last_input(kern, n_in) if extra else kern,
        grid=(nb, nsteps),
        in_specs=[src(256, q_col), src(256, k_col), src(512, v_col), src(512, g_col), la_spec,
                  pl.BlockSpec((1, 512), lambda b, s: (0, 0)),
                  pl.BlockSpec((1, 2, 256, LANES), lambda b, s: (b, 0, 0, 0)),
                  pl.BlockSpec((ts, ts), lambda b, s: (0, 0)),
                  pl.BlockSpec((256, LANES), lambda b, s: (0, 0))] + extra_specs,
        out_specs=[src(512, 0), pl.BlockSpec((1, 2, 256, LANES), lambda b, s: (b, 0, 0, 0))],
        out_shape=[o_shape, jax.ShapeDtypeStruct((nb, 2, 256, LANES), F32)],
        scratch_shapes=[pltpu.VMEM((2, 256, LANES), F32)],
        input_output_aliases={n_in: 0} if extra else {},
        compiler_params=_cparams(("parallel", "arbitrary")),
        name=name,
    )(q_arr, k_arr, v_arr, g_arr, la, gain, s0, tri, bmask, *extra)


def _state_to_pairs(s):
    nb = s.shape[0]
    st = jnp.swapaxes(s, -1, -2).reshape(nb, 2, 2, LANES, HEAD)
    z = jnp.zeros_like(st[:, :, 0])
    top = jnp.concatenate([st[:, :, 0], z], axis=-1)
    bot = jnp.concatenate([z, st[:, :, 1]], axis=-1)
    return jnp.concatenate([top, bot], axis=-2)


def _pairs_to_state(sp):
    nb = sp.shape[0]
    h0 = sp[:, :, :LANES, :HEAD]
    h1 = sp[:, :, LANES:, HEAD:]
    st = jnp.stack([h0, h1], axis=2).reshape(nb, 4, LANES, HEAD)
    return jnp.swapaxes(st, -1, -2)


def _cumsum_kernel(x_ref, tri_ref, o_ref, carry_ref):
    @pl.when(pl.program_id(1) == 0)
    def _():
        carry_ref[...] = jnp.zeros_like(carry_ref)

    tri = tri_ref[...]
    a1, a2, a3 = _split3(x_ref[...])
    c = _dot(tri, a1) + _dot(tri, a2) + _dot(tri, a3) + carry_ref[0:1, :]
    o_ref[...] = c * LOG2E
    carry_ref[...] = jnp.broadcast_to(c[-1:, :], carry_ref.shape)


def row_cumsum(x, *, row0, nb, n, tb):
    r = np.arange(tb)
    tri = jnp.asarray((r[None, :] <= r[:, None]).astype(np.float32), BF16)
    nblk = n // tb
    blk0 = row0 // tb
    return pl.pallas_call(
        _cumsum_kernel,
        grid=(nb, nblk),
        in_specs=[pl.BlockSpec((tb, LANES), lambda b, j: (blk0 + b * nblk + j, 0)),
                  pl.BlockSpec((tb, tb), lambda b, j: (0, 0))],
        out_specs=pl.BlockSpec((tb, LANES), lambda b, j: (b * nblk + j, 0)),
        out_shape=jax.ShapeDtypeStruct((nb * n, LANES), F32),
        scratch_shapes=[pltpu.VMEM((SUBLANES, LANES), F32)],
        compiler_params=_cparams(("parallel", "arbitrary")),
        name="logf_cumsum",
    )(x, tri)


def _col_reduce(x, op):
    tk, tq = x.shape
    if tk > FOLD and tk % FOLD == 0:
        x = op(x.reshape(tk // FOLD, FOLD, tq), axis=0)
    return op(x, axis=0, keepdims=True)


def _values_with_ones(vt_head):
    ones = jnp.ones((ONES_ROWS, vt_head.shape[1]), F32)
    return jnp.concatenate([vt_head, ones], axis=0).astype(BF16)


def _attn_update(s, smax, vt_ext, m_ref, acc_ref, idx):
    m_prev = m_ref[idx]
    m_new = jnp.maximum(m_prev, smax)
    alpha = jnp.exp2(m_prev - m_new)
    p = jnp.exp2(s - m_new)
    acc_ref[idx] = alpha * acc_ref[idx] + _dot(vt_ext, p.astype(BF16))
    m_ref[idx] = m_new


def _eye(n):
    r = lax.broadcasted_iota(I32, (n, n), 0)
    c = lax.broadcasted_iota(I32, (n, n), 1)
    return jnp.where(r == c, 1.0, 0.0).astype(BF16)


def _transpose_exact(x, eye):
    a1, a2, a3 = _split3(x)
    return _dot_nt(eye, a1) + _dot_nt(eye, a2) + _dot_nt(eye, a3)


def _write_heads(o_ref, acc_ref, tq):
    eye = _eye(tq)
    for c in range(4):
        a0, a1 = acc_ref[2 * c], acc_ref[2 * c + 1]
        ot = jnp.concatenate([a0[:HEAD] / a0[HEAD:HEAD + 1], a1[:HEAD] / a1[HEAD:HEAD + 1]], axis=0)
        o_ref[:, c * LANES:(c + 1) * LANES] = _transpose_exact(ot, eye)


def _fox_kernel(qi_ref, kj_ref, q_ref, k_ref, v_ref, c_ref, o_ref, m_ref, acc_ref,
                *, tq, tk, q_off):
    s_id = pl.program_id(1)
    qi = qi_ref[s_id]
    kj = kj_ref[s_id]
    j_last = (q_off + (qi + 1) * tq - 1) // tk

    @pl.when(kj == 0)
    def _():
        m_ref[...] = jnp.full_like(m_ref, NEG)
        acc_ref[...] = jnp.zeros_like(acc_ref)

    def block(masked):
        visible = None
        if masked:
            kpos = kj * tk + lax.broadcasted_iota(I32, (tk, tq), 0)
            qpos = q_off + qi * tq + lax.broadcasted_iota(I32, (tk, tq), 1)
            visible = kpos <= qpos
        qs, ks = _pair_masked_operands(q_ref[...], k_ref[...])
        v = v_ref[...]
        vts = []
        for p in range(4):
            vt = v[:, p * LANES:(p + 1) * LANES].T
            vts += [vt[:HEAD], vt[HEAD:]]
        _fox_block(qs, ks, vts, c_ref[...], visible, m_ref, acc_ref)

    @pl.when(kj < j_last)
    def _():
        block(False)

    @pl.when(kj == j_last)
    def _():
        block(True)
        _write_heads(o_ref, acc_ref, tq)


def _fox_bounded_kernel(qi_ref, kj_ref, q_ref, k_ref, v_ref, c_ref, bnd_ref, o_ref, m_ref, acc_ref,
                        *, tq, tk, q_off):
    s_id = pl.program_id(1)
    qi = qi_ref[s_id]
    kj = kj_ref[s_id]
    j_last = (q_off + (qi + 1) * tq - 1) // tk

    @pl.when(kj == 0)
    def _():
        acc_ref[...] = jnp.zeros_like(acc_ref)
        bt = _transpose_exact(bnd_ref[...], _eye(LANES))
        for h in range(B_HEADS):
            m_ref[h] = bt[h:h + 1, :]

    def block(masked):
        qs, ks = _pair_masked_operands(q_ref[...], k_ref[...])
        c = c_ref[...]
        if masked:
            kpos = kj * tk + lax.broadcasted_iota(I32, (tk, tq), 0)
            qpos = q_off + qi * tq + lax.broadcasted_iota(I32, (tk, tq), 1)
            visible = kpos <= qpos
        probs = []
        for h in range(B_HEADS):
            e = _dot_nt(ks[h], qs[h]) - c[:, h:h + 1] - m_ref[h]
            if masked:
                e = jnp.where(visible, e, NEG)
            probs.append(jnp.exp2(e).astype(BF16))
        v = v_ref[...]
        for p in range(4):
            vt = v[:, p * LANES:(p + 1) * LANES].T
            for hh in range(2):
                h = 2 * p + hh
                acc_ref[h] = acc_ref[h] + _dot(_values_with_ones(vt[hh * HEAD:(hh + 1) * HEAD]), probs[h])

    @pl.when(kj < j_last)
    def _():
        block(False)

    @pl.when(kj == j_last)
    def _():
        block(True)
        _write_heads(o_ref, acc_ref, tq)


def _pair_masked_operands(q, k):
    hm = _half_masks()
    qs, ks = [], []
    for p in range(4):
        kp = k[:, p * LANES:(p + 1) * LANES].astype(BF16)
        for hh in range(2):
            qs.append((q[:, p * LANES:(p + 1) * LANES] * hm[hh]).astype(BF16))
            ks.append(kp)
    return qs, ks


def _fox_block(qs, ks, vts, c, visible, m_ref, acc_ref):
    scores, maxima = [], []
    for h in range(B_HEADS):
        s = _dot_nt(ks[h], qs[h]) - c[:, h:h + 1]
        if visible is not None:
            s = jnp.where(visible, s, NEG)
        scores.append(s)
        maxima.append(_col_reduce(s, jnp.max))
    for h in range(B_HEADS):
        _attn_update(scores[h], maxima[h], _values_with_ones(vts[h]), m_ref, acc_ref, h)


def _fox_decode_kernel(q_ref, kp_ref, vp_ref, lp_ref, kn_ref, vn_ref, ln_ref, tri_ref, o_ref,
                       m_ref, acc_ref, *, ts, past, tb):
    m_ref[...] = jnp.full_like(m_ref, NEG)
    acc_ref[...] = jnp.zeros_like(acc_ref)
    eye = _eye(LANES)
    tri = tri_ref[...]

    def cumsum(x, tri_blk, carry):
        a1, a2, a3 = _split3(x)
        return _dot(tri_blk, a1) + _dot(tri_blk, a2) + _dot(tri_blk, a3) + carry

    def transposed(v):
        return _dot_nt(eye[:v.shape[1], :v.shape[1]], v.astype(BF16))

    lp = lp_ref[...]
    carry = jnp.zeros((1, lp.shape[1]), F32)
    c_blocks = []
    for j in range(past // tb):
        cb = cumsum(lp[j * tb:(j + 1) * tb], tri, carry)
        carry = cb[-1:, :]
        c_blocks.append(cb)
    c_past = jnp.concatenate(c_blocks, axis=0) * LOG2E
    c_new = cumsum(ln_ref[:, :B_HEADS], tri[:ts, :ts], carry) * LOG2E
    q = q_ref[...]
    def attend(k, v, c, visible):
        qs, ks = _pair_masked_operands(q, k)
        vts = []
        for p in range(4):
            vt = transposed(v[:, p * LANES:(p + 1) * LANES])
            vts += [vt[:HEAD], vt[HEAD:]]
        _fox_block(qs, ks, vts, c, visible, m_ref, acc_ref)

    attend(kp_ref[...], vp_ref[...], c_past, None)
    kpos = lax.broadcasted_iota(I32, (ts, ts), 0)
    qpos = lax.broadcasted_iota(I32, (ts, ts), 1)
    attend(kn_ref[...], vn_ref[...], c_new, kpos <= qpos)
    _write_heads(o_ref, acc_ref, ts)


def fox_decode(q_arr, k_past, v_past, lf_past, k_new, v_new, v_col, lf_new, *, row0, nb, ts, past,
               out_rows, into, name):
    tb = _pick_tile(past, (256, 128, 64))
    r = np.arange(tb)
    tri = jnp.asarray((r[None, :] <= r[:, None]).astype(np.float32), BF16)
    blk0 = row0 // ts
    new = lambda w, c: pl.BlockSpec((ts, w), lambda b, c=c: (blk0 + b, c))
    old = pl.BlockSpec((past, 512), lambda b: (b, 0))
    o_shape, extra, extra_specs = _shared_rows(into, out_rows, 512)
    n_in = 8
    kern = functools.partial(_fox_decode_kernel, ts=ts, past=past, tb=tb)
    return pl.pallas_call(
        _ignore_last_input(kern, n_in) if extra else kern,
        grid=(nb,),
        in_specs=[new(512, 0), old, old, pl.BlockSpec((past, B_HEADS), lambda b: (b, 0)),
                  new(512, 0), new(512, v_col),
                  new(LANES, 0), pl.BlockSpec((tb, tb), lambda b: (0, 0))] + extra_specs,
        out_specs=new(512, 0),
        out_shape=o_shape,
        scratch_shapes=[pltpu.VMEM((8, 1, ts), F32), pltpu.VMEM((8, HEAD + ONES_ROWS, ts), F32)],
        input_output_aliases={n_in: 0} if extra else {},
        compiler_params=_cparams(("parallel",)),
        name=name,
    )(q_arr, k_past, v_past, lf_past, k_new, v_new, lf_new, tri, *extra)


def fox_attention(q_arr, k_arr, v_arr, v_col, c_arr, bound, *, q_row0, nb, tq_total, tk_total, q_off,
                  tq, tk, out_rows, into, name):
    nq = tq_total // tq
    steps = [(i, j) for i in range(nq) for j in range((q_off + (i + 1) * tq - 1) // tk + 1)]
    qi = jnp.asarray([s[0] for s in steps], I32)
    kj = jnp.asarray([s[1] for s in steps], I32)
    qb0 = q_row0 // tq
    nkb = tk_total // tk
    o_shape, extra, extra_specs = _shared_rows(into, out_rows, 512)
    qrows = lambda w: pl.BlockSpec((tq, w), lambda b, s, qi, kj: (qb0 + b * nq + qi[s], 0))
    operands = [q_arr, k_arr, v_arr, c_arr]
    in_specs = [qrows(512),
                pl.BlockSpec((tk, 512), lambda b, s, qi, kj: (b * nkb + kj[s], 0)),
                pl.BlockSpec((tk, 512), lambda b, s, qi, kj: (b * nkb + kj[s], v_col)),
                pl.BlockSpec((tk, LANES), lambda b, s, qi, kj: (b * nkb + kj[s], 0))]
    body = _fox_kernel
    if bound is not None:
        operands.append(bound)
        in_specs.append(qrows(LANES))
        body = _fox_bounded_kernel
    n_in = 2 + len(operands)
    grid_spec = pltpu.PrefetchScalarGridSpec(
        num_scalar_prefetch=2,
        grid=(nb, len(steps)),
        in_specs=in_specs + extra_specs,
        out_specs=qrows(512),
        scratch_shapes=[pltpu.VMEM((8, 1, tq), F32), pltpu.VMEM((8, HEAD + ONES_ROWS, tq), F32)],
    )
    kern = functools.partial(body, tq=tq, tk=tk, q_off=q_off)
    return pl.pallas_call(
        _ignore_last_input(kern, n_in) if extra else kern,
        grid_spec=grid_spec,
        out_shape=o_shape,
        input_output_aliases={n_in: 0} if extra else {},
        compiler_params=_cparams(("parallel", "arbitrary")),
        name=name,
    )(qi, kj, *operands, *extra)


def _head_sum_matrix():
    r = lax.broadcasted_iota(I32, (4 * LANES, LANES), 0)
    c = lax.broadcasted_iota(I32, (4 * LANES, LANES), 1)
    return jnp.where(r // HEAD == c, 1.0, 0.0).astype(BF16)


def _sortable_key(x):
    b = lax.bitcast_convert_type(x + 0.0, I32)
    return jnp.where(b < 0, b ^ jnp.int32(0x7FFFFFFF), b)


def _dsa_kernel(iq_ref, misc_ref, qc_ref, kc_ref, vc_ref, ik_ref, o_ref,
                keys_ref, half_ref, qm_ref, m_ref, acc_ref, thr_ref, jcut_ref,
                *, tq, tk, n_keys, q_off, ksel):
    i = pl.program_id(1)
    hm = _half_masks()
    qcol = lax.broadcasted_iota(I32, (1, tq), 1)
    lim = jnp.minimum(((q_off + i * tq + qcol) // CHUNK + 1) * CHUNK, n_keys)
    lim_max = jnp.minimum(((q_off + (i + 1) * tq - 1) // CHUNK + 1) * CHUNK, n_keys)
    nkb = (lim_max + tk - 1) // tk
    eye = _eye(LANES)

    def key_pos(kb):
        return kb * tk + lax.broadcasted_iota(I32, (tk, tq), 0)

    iq = iq_ref[...]
    for hd in range(IDX_HEADS):
        c0 = (hd // 2) * LANES
        qm_ref[hd] = (iq[:, c0:c0 + LANES] * hm[hd % 2]).astype(BF16)
    iwt = _transpose_exact(misc_ref[...], eye) * IDX_SCALE

    def score_body(kb, kk_max):
        k0 = pl.multiple_of(kb * tk, tk)
        ikb = ik_ref[pl.ds(k0, tk), :]
        sc = jnp.zeros((tk, tq), F32)
        for hd in range(IDX_HEADS):
            a = _dot_nt(ikb, qm_ref[hd])
            sc = sc + jnp.maximum(a, 0.0) * iwt[MISC_IW + hd:MISC_IW + hd + 1, :]
        key = jnp.where(key_pos(kb) < lim, _sortable_key(sc), INT_MIN)
        keys_ref[kb] = key
        half_ref[kb] = jnp.right_shift(key, 16).astype(I16)
        kf32 = kc_ref[pl.ds(k0, tk), :].astype(F32)
        kk = _dot((kf32 * kf32).astype(BF16), half_ones)
        return jnp.maximum(kk_max, _col_reduce(kk, jnp.max))

    lane_r = lax.broadcasted_iota(I32, (LANES, LANES), 0)
    lane_c = lax.broadcasted_iota(I32, (LANES, LANES), 1)
    half_ones = jnp.where(lane_r // HEAD == lane_c // HEAD, 1.0, 0.0).astype(BF16)
    kk_max = lax.fori_loop(0, nkb, score_body, jnp.zeros((1, LANES), F32))

    def count(pred):
        def body(kb, acc):
            hit = jnp.where(pred(keys_ref[kb], key_pos(kb)), 1.0, 0.0)
            return acc + jnp.sum(hit.reshape(tk // COUNT_FOLD, COUNT_FOLD, tq), axis=0)
        acc = lax.fori_loop(0, nkb, body, jnp.zeros((COUNT_FOLD, tq), F32))
        return jnp.sum(acc, axis=0, keepdims=True)

    def count_half(cand):
        def body(kb, acc):
            hit = jnp.where(half_ref[kb] >= cand, jnp.int16(1), jnp.int16(0))
            parts = [hit[r * COUNT_FOLD:(r + 1) * COUNT_FOLD] for r in range(tk // COUNT_FOLD)]
            while len(parts) > 1:
                parts = [a + b for a, b in zip(parts[0::2], parts[1::2])] + parts[len(parts) & ~1:]
            return acc + parts[0]
        acc = lax.fori_loop(0, nkb, body, jnp.zeros((COUNT_FOLD, tq), I16))
        return jnp.sum(acc.astype(F32), axis=0, keepdims=True)

    def any_row(flag):
        return jnp.max(jnp.where(flag, 1.0, 0.0)) > 0.0

    thr_ref[...] = jnp.full((1, tq), INT_MIN + 1, I32)
    jcut_ref[...] = jnp.full((1, tq), 2 ** 30, I32)
    kf = float(ksel)

    @pl.when(lim_max > ksel)
    def _():
        few = lim <= ksel

        def unsettled(cres):
            return jnp.logical_and(cres != kf, jnp.logical_not(few))

        def search(first, last, group, to_half, state):
            assert (last - first) % group == 0

            def cond(state):
                t, _, _, go = state
                return jnp.logical_and(t < last, go)

            def body(state):
                t, res, cres, _ = state
                for g in range(group):
                    cand = res + jnp.left_shift(jnp.int32(1), 30 - g - t)
                    cnt = count_half(to_half(cand))
                    take = cnt >= kf
                    res = jnp.where(take, cand, res)
                    cres = jnp.where(take, cnt, cres)
                return t + group, res, cres, any_row(unsettled(cres))

            return lax.while_loop(cond, body, (jnp.int32(first),) + state)[1:]

        def high_half(x):
            return jnp.right_shift(x, 16).astype(I16)

        def low_half(x):
            return ((x & 0xFFFF) - 32768).astype(I16)

        c0 = count_half(jnp.zeros((1, tq), I16))
        res = jnp.where(c0 >= kf, 0, INT_MIN).astype(I32)
        cres = jnp.where(c0 >= kf, c0, 2.0 * n_keys)
        res, cres, go = search(0, 15, 3, high_half, (res, cres, any_row(unsettled(cres))))

        @pl.when(go)
        def _():
            rh = high_half(res)

            def relabel(kb, carry):
                hi = half_ref[kb]
                lo = low_half(keys_ref[kb])
                half_ref[kb] = jnp.where(hi > rh, jnp.int16(32767),
                                         jnp.where(hi < rh, jnp.int16(-32768), lo))
                return carry

            lax.fori_loop(0, nkb, relabel, 0)

        res, cres, open_rows = search(15, 31, 4, low_half, (res, cres, go))
        thr = jnp.maximum(res, INT_MIN + 1)
        thr_ref[...] = thr

        @pl.when(open_rows)
        def _():
            need = kf - count(lambda blk, kpos: blk > thr)
            n_eq = count(lambda blk, kpos: blk == thr)
            split = n_eq > need

            @pl.when(any_row(split))
            def _():
                def idx_body(t, lo):
                    cand = lo + jnp.left_shift(jnp.int32(1), 14 - t)
                    cnt = count(lambda blk, kpos: (blk == thr) & (kpos < cand))
                    return jnp.where(cnt < need, cand, lo)

                lo = lax.fori_loop(0, 15, idx_body, jnp.zeros((1, tq), I32))
                jcut_ref[...] = jnp.where(split, lo, 2 ** 30)

    qc = qc_ref[...]
    for hd in range(C_HEADS):
        c0 = (hd // 2) * LANES
        qm_ref[hd] = (qc[:, c0:c0 + LANES] * hm[hd % 2]).astype(BF16)
    thr = thr_ref[...]
    jcut = jcut_ref[...]

    def block_operands(kb):
        k0 = pl.multiple_of(kb * tk, tk)
        kblk = kc_ref[pl.ds(k0, tk), :]
        vt = _dot_nt(eye, vc_ref[pl.ds(k0, tk), :])
        vts = [_values_with_ones(vt[hh * HEAD:(hh + 1) * HEAD]) for hh in range(2)]
        keys = keys_ref[kb]
        sel = (keys > thr) | ((keys == thr) & (key_pos(kb) <= jcut))
        return kblk, vts, sel

    qq = _dot((qc * qc).astype(BF16), _head_sum_matrix())
    qq_t = _dot_nt(eye, qq.astype(BF16))
    bound = [jnp.sqrt(qq_t[hd:hd + 1, :] * kk_max[:, (hd % 2) * HEAD:(hd % 2) * HEAD + 1]) * BOUND_SLACK
             for hd in range(C_HEADS)]
    acc_ref[...] = jnp.zeros_like(acc_ref)

    def fast_body(kb, carry):
        kblk, vts, sel = block_operands(kb)
        keep = jnp.where(sel, 1.0, 0.0).astype(BF16)
        probs = [jnp.exp2(_dot_nt(kblk, qm_ref[hd]) - bound[hd]).astype(BF16) * keep
                 for hd in range(C_HEADS)]
        for hd in range(C_HEADS):
            acc_ref[hd] = acc_ref[hd] + _dot(vts[hd % 2], probs[hd])
        return carry

    lax.fori_loop(0, nkb, fast_body, 0)
    denom_min = acc_ref[0][HEAD:HEAD + 1, :]
    for hd in range(1, C_HEADS):
        denom_min = jnp.minimum(denom_min, acc_ref[hd][HEAD:HEAD + 1, :])

    @pl.when(jnp.min(denom_min) < DENOM_FLOOR)
    def _():
        m_ref[...] = jnp.full_like(m_ref, NEG)
        acc_ref[...] = jnp.zeros_like(acc_ref)

        def exact_body(kb, carry):
            kblk, vts, sel = block_operands(kb)
            scores = [jnp.where(sel, _dot_nt(kblk, qm_ref[hd]), NEG) for hd in range(C_HEADS)]
            maxima = [_col_reduce(s, jnp.max) for s in scores]
            for hd in range(C_HEADS):
                _attn_update(scores[hd], maxima[hd], vts[hd % 2], m_ref, acc_ref, hd)
            return carry

        lax.fori_loop(0, nkb, exact_body, 0)

    _write_heads(o_ref, acc_ref, tq)


def dsa_attention(iq_arr, misc_arr, misc_col, qc_arr, kc, vc, ik, *, q_row0, nb, tq_total,
                  n_keys, n_keys_pad, q_off, tq, tk, out_rows, into, name):
    nq = tq_total // tq
    qb0 = q_row0 // tq
    ksel = min(TOPK_MAX, n_keys // 4)
    qspec = lambda w, c: pl.BlockSpec((tq, w), lambda b, i, c=c: (qb0 + b * nq + i, c))
    kspec = pl.BlockSpec((n_keys_pad, LANES), lambda b, i: (b, 0))
    o_shape, extra, extra_specs = _shared_rows(into, out_rows, 512)
    n_in = 6
    kern = functools.partial(_dsa_kernel, tq=tq, tk=tk, n_keys=n_keys, q_off=q_off, ksel=ksel)
    return pl.pallas_call(
        _ignore_last_input(kern, n_in) if extra else kern,
        grid=(nb, nq),
        in_specs=[qspec(512, 0), qspec(LANES, misc_col), qspec(512, 0), kspec, kspec, kspec]
        + extra_specs,
        out_specs=qspec(512, 0),
        out_shape=o_shape,
        input_output_aliases={n_in: 0} if extra else {},
        scratch_shapes=[pltpu.VMEM((n_keys_pad // tk, tk, tq), I32),
                        pltpu.VMEM((n_keys_pad // tk, tk, tq), I16),
                        pltpu.VMEM((8, tq, LANES), BF16),
                        pltpu.VMEM((8, 1, tq), F32),
                        pltpu.VMEM((8, HEAD + ONES_ROWS, tq), F32),
                        pltpu.VMEM((1, tq), I32), pltpu.VMEM((1, tq), I32)],
        compiler_params=_cparams(("parallel", "arbitrary")),
        name=name,
    )(iq_arr, misc_arr, qc_arr, kc, vc, ik, *extra)


def _rope_tables(pos):
    half = HEAD // 2
    inv = jnp.power(ROPE_THETA, -jnp.arange(half, dtype=F32) / half)
    ang = pos.astype(F32)[:, None] * inv[None, :]
    cos, sin = jnp.cos(ang), jnp.sin(ang)
    z = jnp.zeros_like(sin)
    return (jnp.tile(cos, (1, 4)), jnp.tile(jnp.concatenate([-sin, z], 1), (1, 2)),
            jnp.tile(jnp.concatenate([z, sin], 1), (1, 2)))


def _head_mean_matrix():
    r = np.arange(512)
    return jnp.asarray((r[:, None] // HEAD == r[None, :] // HEAD).astype(np.float32), BF16)


def _pick_tile(n, prefs):
    for t in prefs:
        if n % t == 0:
            return t
    raise ValueError(f"no tile in {prefs} divides {n}")


def _pad_keys(x, n_pad):
    nb, n, w = x.shape
    if n_pad > n:
        x = jnp.concatenate([x, jnp.zeros((nb, n_pad - n, w), x.dtype)], axis=1)
    return x.reshape(nb * n_pad, w)


def kernel(x_prompt, x_sample, state_a, cache_b_k, cache_b_v, cache_b_logf, cache_c_k, cache_c_v,
           cache_c_idx, state_d, norm_mix, norm_ffn, even_w_in, even_b_f, even_q_norm, even_k_norm,
           even_w_out, odd_w_in, odd_w_gate_up, odd_b_gate, odd_q_norm, odd_k_norm, odd_o_norm,
           odd_w_out, ffn_w1, ffn_w2):
    bp, tp = x_prompt.shape[:2]
    nb, ts = x_sample.shape[:2]
    past = cache_b_k.shape[2]
    depth = norm_mix.shape[0]
    rp, rs = bp * tp, nb * ts
    rows = rp + rs
    assert bp == 1 and tp % 128 == 0 and ts % SUB == 0 and rp % ts == 0 and past % CHUNK == 0

    y = jnp.concatenate([x_prompt.reshape(rp, D_MODEL), x_sample.reshape(rs, D_MODEL)], axis=0)
    pos = jnp.concatenate([jnp.arange(tp, dtype=I32),
                           jnp.tile(past + jnp.arange(ts, dtype=I32), nb)])
    cos, slo, shi = _rope_tables(pos)
    gmat = _head_mean_matrix()

    tm = _pick_tile(rows, (512, 256, 128, 64, 32))
    tm_prep = _pick_tile(rows, (256, 128, 64, 32))
    ts_p = _pick_tile(tp, (128, 64, 32, 16))
    tq_fox = _pick_tile(tp, (512, 256, 128))
    tk_fox = _pick_tile(tp, (512, 256, 128))
    tq_dsa = _pick_tile(tp, (256, 128))
    tk_dsa = _pick_tile(tp, (512, 256, 128))
    nk_s = past + ts
    nk_s_pad = -(-nk_s // 384) * 384
    tb_p = _pick_tile(tp, (512, 256, 128))

    lg = jnp.log1p(-jnp.exp2(-5.0 - jnp.arange(A_HEADS, dtype=F32)))
    la_ret = jnp.repeat(lg, A_DK)[None, :]
    ones_gain = jnp.ones((1, 512), F32)
    blank = lambda: jnp.zeros((rows, 512), F32)
    zero_state = jnp.zeros((bp, 2, 256, LANES), F32)

    a_p, a_s, bk_p, bk_s, bv_p, bv_s, bf_p, bf_s = [], [], [], [], [], [], [], []
    ck_p, ck_s, cv_p, cv_s, ci_p, ci_s, d_p, d_s = [], [], [], [], [], [], [], []

    for l in range(depth):
        i = l // 2
        if l % 2 == 0:
            w_in = jnp.concatenate(
                [even_w_in[i], jnp.zeros((D_MODEL, EV_WIDTH - even_w_in.shape[2]), F32)], 1).astype(BF16)
            h = norm_matmul(y, norm_mix[l], w_in, tm=tm, tn=EV_WIDTH, name="even_in_proj")
            bf = jnp.concatenate([even_b_f[i], jnp.zeros((LANES - B_HEADS,), F32)])[None, :]
            qka, qbn, kbn, lf, qq, kk, qk = prep_even(
                h, cos, slo, shi, jnp.tile(even_q_norm[i], B_HEADS)[None, :],
                jnp.tile(even_k_norm[i], B_HEADS)[None, :], bf, gmat, tm=tm_prep)
            oa, sa_p = linear_scan(qka, 0, qka, 1, h, EV_VA // 512, h, EV_GA // 512,
                                   jnp.broadcast_to(la_ret, (ts_p, 256)), ones_gain, zero_state,
                                   row0=0, nb=bp, t=tp, ts=ts_p, la_shared=True, out_rows=rows,
                                   into=blank(), name="retention_prompt")
            oa, sa_s = linear_scan(qka, 0, qka, 1, h, EV_VA // 512, h, EV_GA // 512,
                                   jnp.broadcast_to(la_ret, (ts, 256)), ones_gain,
                                   _state_to_pairs(state_a[i]),
                                   row0=rp, nb=nb, t=ts, ts=ts, la_shared=True, out_rows=rows,
                                   into=oa, name="retention_decode")
            lf8 = lf[:, :B_HEADS]
            c_p = row_cumsum(lf, row0=0, nb=bp, n=tp, tb=tb_p)
            qk_bound = jnp.sqrt(qq[:rp] * jnp.max(kk[:rp], axis=0, keepdims=True)) * BOUND_SLACK
            tight = jnp.max((qk_bound - qk[:rp])[:, :B_HEADS]) < FOX_BOUND_RANGE
            fox_prompt = functools.partial(
                fox_attention, qbn, kbn, h, EV_VB // 512, c_p, q_row0=0, nb=bp, tq_total=tp,
                tk_total=tp, q_off=0, tq=tq_fox, tk=tk_fox, out_rows=rows, into=blank())
            ob = lax.cond(tight,
                          lambda: fox_prompt(qk_bound - c_p, name="fox_prompt_bounded"),
                          lambda: fox_prompt(None, name="fox_prompt"))
            kb_new = kbn[rp:].reshape(nb, ts, 512)
            vb_new = h[rp:, EV_VB:EV_VB + 512].reshape(nb, ts, 512)
            ob = fox_decode(qbn, cache_b_k[i].reshape(nb * past, 512).astype(BF16),
                            cache_b_v[i].reshape(nb * past, 512).astype(BF16),
                            cache_b_logf[i].reshape(nb * past, B_HEADS), kbn, h, EV_VB // 512, lf,
                            row0=rp, nb=nb, ts=ts, past=past, out_rows=rows, into=ob,
                            name="fox_decode")
            w_out = even_w_out[i].astype(BF16)
            y = matmul_residual(y, [oa, ob], [w_out[:512], w_out[512:]], tm=tm, name="even_out_proj")
            a_p.append(_pairs_to_state(sa_p))
            a_s.append(_pairs_to_state(sa_s))
            bk_p.append(kbn[:rp].reshape(bp, tp, B_HEADS, B_DH))
            bk_s.append(kb_new.reshape(nb, ts, B_HEADS, B_DH))
            bv_p.append(h[:rp, EV_VB:EV_VB + 512].reshape(bp, tp, B_HEADS, B_DH))
            bv_s.append(vb_new.reshape(nb, ts, B_HEADS, B_DH))
            bf_p.append(lf8[:rp].reshape(bp, tp, B_HEADS))
            bf_s.append(lf8[rp:].reshape(nb, ts, B_HEADS))
        else:
            w = odd_w_in[i]
            offs = np.cumsum([0, 512, 128, 128, 512, 64, 8, 256, 256, 512, 512, 16])
            qc_w, kc_w, vc_w, iq_w, ik_w, iw_w, qd_w, kd_w, vd_w, gd_w, gr_w = [
                w[:, int(a):int(b)] for a, b in zip(offs[:-1], offs[1:])]
            qc_w = qc_w.reshape(D_MODEL, 2, 4, C_DH).transpose(0, 2, 1, 3).reshape(D_MODEL, 512)
            w_in = jnp.concatenate(
                [qc_w, iq_w, vd_w, gd_w, qd_w, kd_w, kc_w, vc_w, ik_w, iw_w, gr_w,
                 jnp.zeros((D_MODEL, OD_WIDTH - OD_MISC - 88), F32)], 1).astype(BF16)
            h = norm_matmul(y, norm_mix[l], w_in, tm=tm, tn=OD_WIDTH, name="odd_in_proj")
            wg = jnp.zeros((LANES, 256), F32).at[MISC_GR:MISC_GR + D_GATE_RANK].set(
                odd_w_gate_up[i]).astype(BF16)
            qcr, iqr, qkd, kcr, ikr, kcb, vcb, ikb, la = prep_odd(
                h, cos, slo, shi, jnp.tile(odd_q_norm[i], C_HEADS)[None, :],
                jnp.tile(odd_k_norm[i], C_KV_HEADS)[None, :], wg, odd_b_gate[i][None, :], gmat,
                tm=tm_prep)
            oc = dsa_attention(iqr, h, OD_MISC // LANES, qcr, kcb, vcb, ikb,
                               q_row0=0, nb=bp, tq_total=tp, n_keys=tp, n_keys_pad=tp, q_off=0,
                               tq=tq_dsa, tk=tk_dsa, out_rows=rows, into=blank(), name="dsa_prompt")

            def with_past(cache, new):
                return _pad_keys(jnp.concatenate(
                    [cache.reshape(nb, past, -1).astype(BF16), new[rp:].reshape(nb, ts, -1)], 1), nk_s_pad)

            ik_past = jnp.concatenate([cache_c_idx[i], cache_c_idx[i]], axis=-1)
            oc = dsa_attention(iqr, h, OD_MISC // LANES, qcr,
                               with_past(cache_c_k[i], kcb), with_past(cache_c_v[i], vcb),
                               with_past(ik_past, ikb),
                               q_row0=rp, nb=nb, tq_total=ts, n_keys=nk_s, n_keys_pad=nk_s_pad,
                               q_off=past, tq=ts, tk=384, out_rows=rows, into=oc, name="dsa_decode")
            gain = jnp.tile(odd_o_norm[i], D_HEADS)[None, :]
            od, sd_p = linear_scan(qkd, 0, qkd, 1, h, OD_VD // 512, h, OD_GD // 512, la, gain,
                                   zero_state, row0=0, nb=bp, t=tp, ts=ts_p, la_shared=False,
                                   out_rows=rows, into=blank(), name="gla_prompt")
            od, sd_s = linear_scan(qkd, 0, qkd, 1, h, OD_VD // 512, h, OD_GD // 512, la, gain,
                                   _state_to_pairs(state_d[i]), row0=rp, nb=nb, t=ts, ts=ts,
                                   la_shared=False, out_rows=rows, into=od, name="gla_decode")
            w_out = odd_w_out[i]
            w_oc = w_out[:512].reshape(2, 4, C_DH, D_MODEL).transpose(1, 0, 2, 3).reshape(512, D_MODEL)
            y = matmul_residual(y, [oc, od], [w_oc.astype(BF16), w_out[512:].astype(BF16)], tm=tm,
                                name="odd_out_proj")
            d_p.append(_pairs_to_state(sd_p))
            d_s.append(_pairs_to_state(sd_s))
            ck_p.append(kcr[:rp].reshape(bp, tp, C_KV_HEADS, C_DH))
            ck_s.append(kcr[rp:].reshape(nb, ts, C_KV_HEADS, C_DH))
            cv_p.append(h[:rp, OD_KVC + LANES:OD_KVC + 2 * LANES].reshape(bp, tp, C_KV_HEADS, C_DH))
            cv_s.append(h[rp:, OD_KVC + LANES:OD_KVC + 2 * LANES].reshape(nb, ts, C_KV_HEADS, C_DH))
            ci_p.append(ikr[:rp, :IDX_DIM].reshape(bp, tp, IDX_DIM))
            ci_s.append(ikr[rp:, :IDX_DIM].reshape(nb, ts, IDX_DIM))
        hid = norm_matmul(y, norm_ffn[l], ffn_w1[l].astype(BF16), tm=tm, tn=D_FF, relu2=True,
                          out_dtype=BF16, name="mlp_up")
        y = matmul_residual(y, [hid], [ffn_w2[l].astype(BF16)], tm=tm, name="mlp_down")

    return (y[:rp].reshape(bp, tp, D_MODEL), y[rp:].reshape(nb, ts, D_MODEL),
            jnp.stack(a_p), jnp.stack(a_s), jnp.stack(bk_p), jnp.stack(bk_s),
            jnp.stack(bv_p), jnp.stack(bv_s), jnp.stack(bf_p), jnp.stack(bf_s),
            jnp.stack(ck_p), jnp.stack(ck_s), jnp.stack(cv_p), jnp.stack(cv_s),
            jnp.stack(ci_p), jnp.stack(ci_s), jnp.stack(d_p), jnp.stack(d_s))
```

```python
import functools

import numpy as np
import jax
import jax.numpy as jnp
from jax import lax
from jax.experimental import pallas as pl
from jax.experimental.pallas import tpu as pltpu

F32 = jnp.float32
BF16 = jnp.bfloat16
I32 = jnp.int32
I16 = jnp.int16

D_MODEL = 1024
CHUNK = 64
ROPE_THETA = 10000.0
EPS = 1e-6
A_HEADS, A_DK, A_DV = 4, 64, 128
B_HEADS, B_DH = 8, 64
C_HEADS, C_KV_HEADS, C_DH = 8, 2, 64
IDX_HEADS, IDX_DIM = 8, 64
IDX_SCALE = (IDX_HEADS * IDX_DIM) ** -0.5
TOPK_MAX = 256
D_HEADS, D_DK, D_DV = 4, 64, 128
D_GATE_RANK = 16
D_GATE_NORM = 16.0
D_FF = 4 * D_MODEL

LANES = 128
SUBLANES = 8
VMEM_LIMIT_BYTES = 56 * 1024 * 1024

HEAD = 64
SUB = 16
FOLD = 64
COUNT_FOLD = 32
ONES_ROWS = 16
NEG = -1e30
LOG2E = 1.4426950408889634
BOUND_SLACK = 1.03
DENOM_FLOOR = 2.0 ** -100
FOX_BOUND_RANGE = 90.0
INT_MIN = -(2 ** 31)

EV_QK, EV_VA, EV_GA, EV_QB, EV_KB, EV_VB, EV_FB = 0, 512, 1024, 1536, 2048, 2560, 3072
EV_WIDTH = 3200
OD_QC, OD_IQ, OD_VD, OD_GD, OD_QKD, OD_KVC, OD_MISC = 0, 512, 1024, 1536, 2048, 2560, 2816
OD_WIDTH = 2944
MISC_IK, MISC_IW, MISC_GR = 0, 64, 72


def _cparams(sem):
    return pltpu.CompilerParams(dimension_semantics=sem, vmem_limit_bytes=VMEM_LIMIT_BYTES)


def _dot(a, b):
    return jnp.dot(a, b, preferred_element_type=F32)


def _dot_nt(a, b):
    return lax.dot_general(a, b, (((1,), (1,)), ((), ())), preferred_element_type=F32)


def _dot_tn(a, b):
    return lax.dot_general(a, b, (((0,), (0,)), ((), ())), preferred_element_type=F32)


def _split3(x):
    a1 = x.astype(BF16)
    r1 = x - a1.astype(F32)
    a2 = r1.astype(BF16)
    a3 = (r1 - a2.astype(F32)).astype(BF16)
    return a1, a2, a3


def _half_masks():
    lane = lax.broadcasted_iota(I32, (1, LANES), 1)
    lo = (lane < HEAD).astype(F32)
    return lo, 1.0 - lo


def _log_sigmoid(x):
    return jnp.minimum(x, 0.0) - jnp.log1p(jnp.exp(-jnp.abs(x)))


def _silu(x):
    return x / (1.0 + jnp.exp(-x))


def _norm_mm_kernel(x_ref, g_ref, w_ref, o_ref, *, relu2):
    x = x_ref[...]
    ms = jnp.mean(x * x, axis=-1, keepdims=True)
    xn = (x * lax.rsqrt(ms + EPS) * g_ref[...]).astype(BF16)
    y = _dot(xn, w_ref[...])
    if relu2:
        y = jnp.maximum(y, 0.0)
        y = y * y
    o_ref[...] = y.astype(o_ref.dtype)


def norm_matmul(x, g, w, *, tm, tn, relu2=False, out_dtype=F32, name):
    rows, k = x.shape
    n = w.shape[1]
    return pl.pallas_call(
        functools.partial(_norm_mm_kernel, relu2=relu2),
        grid=(rows // tm, n // tn),
        in_specs=[pl.BlockSpec((tm, k), lambda i, j: (i, 0)),
                  pl.BlockSpec((1, k), lambda i, j: (0, 0)),
                  pl.BlockSpec((k, tn), lambda i, j: (0, j))],
        out_specs=pl.BlockSpec((tm, tn), lambda i, j: (i, j)),
        out_shape=jax.ShapeDtypeStruct((rows, n), out_dtype),
        compiler_params=_cparams(("parallel", "parallel")),
        name=name,
    )(x, g.reshape(1, k), w)


def _mm_res_kernel(*refs, n_in):
    res_ref = refs[0]
    o_ref = refs[-1]
    acc = res_ref[...]
    for a_ref, w_ref in zip(refs[1:1 + n_in], refs[1 + n_in:1 + 2 * n_in]):
        acc = acc + _dot(a_ref[...].astype(BF16), w_ref[...])
    o_ref[...] = acc


def matmul_residual(res, a_list, w_list, *, tm, name):
    rows, n = res.shape
    n_in = len(a_list)
    in_specs = [pl.BlockSpec((tm, n), lambda i: (i, 0))]
    in_specs += [pl.BlockSpec((tm, a.shape[1]), lambda i: (i, 0)) for a in a_list]
    in_specs += [pl.BlockSpec(w.shape, lambda i: (0, 0)) for w in w_list]
    return pl.pallas_call(
        functools.partial(_mm_res_kernel, n_in=n_in),
        grid=(rows // tm,),
        in_specs=in_specs,
        out_specs=pl.BlockSpec((tm, n), lambda i: (i, 0)),
        out_shape=jax.ShapeDtypeStruct((rows, n), F32),
        compiler_params=_cparams(("parallel",)),
        name=name,
    )(res, *a_list, *w_list)


def _tile_lanes(t, width):
    n = width // LANES
    return t if n == 1 else jnp.concatenate([t] * n, axis=1)


def _rope(x, cos, sin_lo, sin_hi):
    w = x.shape[1]
    up = pltpu.roll(x, w - HEAD // 2, 1)
    dn = pltpu.roll(x, HEAD // 2, 1)
    return x * _tile_lanes(cos, w) + up * _tile_lanes(sin_lo, w) + dn * _tile_lanes(sin_hi, w)


def _head_rms(x, gmat):
    s = x * x
    hi = s.astype(BF16)
    lo = (s - hi.astype(F32)).astype(BF16)
    ms = (_dot(hi, gmat) + _dot(lo, gmat)) * (1.0 / HEAD)
    return x * lax.rsqrt(ms + EPS)


def _prep_even_kernel(qk_ref, qb_ref, kb_ref, fb_ref, cos_ref, slo_ref, shi_ref,
                      qg_ref, kg_ref, bf_ref, gmat_ref,
                      qka_ref, qbn_ref, kbn_ref, lf_ref, qq_ref, kk_ref, qk_ref_out):
    cos, slo, shi = cos_ref[...], slo_ref[...], shi_ref[...]
    lane = lax.broadcasted_iota(I32, (1, 4 * LANES), 1)
    kscale = jnp.where(lane < 2 * LANES, 1.0, A_DK ** -0.5)
    qka_ref[...] = _rope(qk_ref[...], cos, slo, shi) * kscale
    gmat = gmat_ref[...]
    qbn = _head_rms(qb_ref[...], gmat) * (qg_ref[...] * (B_DH ** -0.5 * LOG2E))
    kbn = _head_rms(kb_ref[...], gmat) * kg_ref[...]
    qbn_ref[...] = qbn
    kbn_ref[...] = kbn
    lf_ref[...] = _log_sigmoid(fb_ref[...] + bf_ref[...])
    hsum = _head_sum_matrix()
    qq_ref[...] = _dot((qbn * qbn).astype(BF16), hsum)
    kk_ref[...] = _dot((kbn * kbn).astype(BF16), hsum)
    qk_ref_out[...] = _dot((qbn * kbn).astype(BF16), hsum)


def prep_even(h, cos, slo, shi, qg, kg, bf, gmat, *, tm):
    rows = h.shape[0]
    blk = lambda w, c: pl.BlockSpec((tm, w), lambda i, c=c: (i, c))
    row = lambda w: pl.BlockSpec((1, w), lambda i: (0, 0))
    return pl.pallas_call(
        _prep_even_kernel,
        grid=(rows // tm,),
        in_specs=[blk(512, EV_QK // 512), blk(512, EV_QB // 512), blk(512, EV_KB // 512),
                  blk(LANES, EV_FB // LANES), blk(LANES, 0), blk(LANES, 0), blk(LANES, 0),
                  row(512), row(512), row(LANES),
                  pl.BlockSpec((512, 512), lambda i: (0, 0))],
        out_specs=[blk(512, 0), blk(512, 0), blk(512, 0)] + [blk(LANES, 0)] * 4,
        out_shape=[jax.ShapeDtypeStruct((rows, 512), F32)] * 3
        + [jax.ShapeDtypeStruct((rows, LANES), F32)] * 4,
        compiler_params=_cparams(("parallel",)),
        name="prep_even",
    )(h, h, h, h, cos, slo, shi, qg, kg, bf, gmat)


def _prep_odd_kernel(qc_ref, iq_ref, qkd_ref, kvc_ref, misc_ref, cos_ref, slo_ref, shi_ref,
                     qg_ref, kg_ref, wg_ref, bg_ref, gmat_ref,
                     qcr_ref, iqr_ref, qkdo_ref, kcr_ref, ikr_ref, kcb_ref, vcb_ref, ikb_ref,
                     la_ref):
    cos, slo, shi = cos_ref[...], slo_ref[...], shi_ref[...]
    gmat = gmat_ref[...]
    qc = _head_rms(qc_ref[...], gmat) * (qg_ref[...] * (C_DH ** -0.5 * LOG2E))
    qcr_ref[...] = _rope(qc, cos, slo, shi)
    iqr_ref[...] = _rope(iq_ref[...], cos, slo, shi)
    lane = lax.broadcasted_iota(I32, (1, 4 * LANES), 1)
    qkdo_ref[...] = qkd_ref[...] * jnp.where(lane < 2 * LANES, D_DK ** -0.5, 1.0)
    kvc = kvc_ref[...]
    kc = _head_rms(kvc[:, :LANES], gmat[:LANES, :LANES]) * kg_ref[...]
    kcr = _rope(kc, cos, slo, shi)
    kcr_ref[...] = kcr
    kcb_ref[...] = kcr.astype(BF16)
    vcb_ref[...] = kvc[:, LANES:].astype(BF16)
    misc = misc_ref[...]
    ikr = _rope(misc, cos, slo, shi)
    lane1 = lax.broadcasted_iota(I32, (1, LANES), 1)
    ik2 = jnp.where(lane1 < HEAD, ikr, pltpu.roll(ikr, HEAD, 1))
    ikr_ref[...] = ik2
    ikb_ref[...] = ik2.astype(BF16)
    z = _dot(misc.astype(BF16), wg_ref[...]) + bg_ref[...]
    la_ref[...] = _log_sigmoid(z) * (1.0 / D_GATE_NORM)


def prep_odd(h, cos, slo, shi, qg, kg, wg, bg, gmat, *, tm):
    rows = h.shape[0]
    blk = lambda w, c: pl.BlockSpec((tm, w), lambda i, c=c: (i, c))
    row = lambda w: pl.BlockSpec((1, w), lambda i: (0, 0))
    return pl.pallas_call(
        _prep_odd_kernel,
        grid=(rows // tm,),
        in_specs=[blk(512, OD_QC // 512), blk(512, OD_IQ // 512), blk(512, OD_QKD // 512),
                  blk(256, OD_KVC // 256), blk(LANES, OD_MISC // LANES),
                  blk(LANES, 0), blk(LANES, 0), blk(LANES, 0),
                  row(512), row(LANES),
                  pl.BlockSpec((LANES, 256), lambda i: (0, 0)), row(256),
                  pl.BlockSpec((512, 512), lambda i: (0, 0))],
        out_specs=[blk(512, 0), blk(512, 0), blk(512, 0), blk(LANES, 0), blk(LANES, 0),
                   blk(LANES, 0), blk(LANES, 0), blk(LANES, 0), blk(256, 0)],
        out_shape=[jax.ShapeDtypeStruct((rows, 512), F32)] * 3
        + [jax.ShapeDtypeStruct((rows, LANES), F32)] * 2
        + [jax.ShapeDtypeStruct((rows, LANES), BF16)] * 3
        + [jax.ShapeDtypeStruct((rows, 256), F32)],
        compiler_params=_cparams(("parallel",)),
        name="prep_odd",
    )(h, h, h, h, h, cos, slo, shi, qg, kg, wg, bg, gmat)


def _scan_kernel(q_ref, k_ref, v_ref, g_ref, la_ref, gain_ref, s0_ref, tri_ref, bmask_ref,
                 o_ref, sfin_ref, st_ref, *, ts, nsteps):
    step = pl.program_id(1)

    @pl.when(step == 0)
    def _():
        st_ref[...] = s0_ref[0]

    q, k, v = q_ref[...], k_ref[...], v_ref[...]
    tri = tri_ref[...]
    a1, a2, a3 = _split3(la_ref[...])
    cum = _dot(tri, a1) + _dot(tri, a2) + _dot(tri, a3)
    qt = q * jnp.exp(cum)
    kt = k * jnp.exp(-cum)
    hm = _half_masks()
    bmask = bmask_ref[...]
    nsub = ts // SUB
    vb = v.astype(BF16)
    elast, upd = [], []
    for u in range(nsub):
        r0 = u * SUB
        last = cum[r0 + SUB - 1:r0 + SUB, :]
        k2 = (k[r0:r0 + SUB, :] * jnp.exp(last - cum[r0:r0 + SUB, :])).astype(BF16)
        elast.append(jnp.exp(last))
        upd.append([_dot_tn(vb[r0:r0 + SUB, 2 * p * LANES:2 * (p + 1) * LANES],
                            k2[:, p * LANES:(p + 1) * LANES]) * bmask for p in range(2)])
    seen = []
    for p in range(2):
        st = st_ref[p]
        row = []
        for u in range(nsub):
            row.append(st.astype(BF16))
            st = st * elast[u][:, p * LANES:(p + 1) * LANES] + upd[u][p]
        st_ref[p] = st
        seen.append(row)
    qtb = qt.astype(BF16)
    inter = jnp.concatenate(
        [jnp.concatenate([_dot_nt(qtb[u * SUB:(u + 1) * SUB, p * LANES:(p + 1) * LANES], seen[p][u])
                          for p in range(2)], axis=1) for u in range(nsub)], axis=0)
    ktb = kt.astype(BF16)
    same_chunk_causal = tri > 0
    intra = []
    for h in range(4):
        l0 = (h // 2) * LANES
        att = _dot_nt((qt[:, l0:l0 + LANES] * hm[h % 2]).astype(BF16), ktb[:, l0:l0 + LANES])
        att = jnp.where(same_chunk_causal, att, 0.0)
        intra.append(_dot(att.astype(BF16), vb[:, h * LANES:(h + 1) * LANES]))
    o = inter + jnp.concatenate(intra, axis=1)
    g = g_ref[...]
    gain = gain_ref[...]
    outs = []
    for h in range(4):
        oh = o[:, h * LANES:(h + 1) * LANES]
        ms = jnp.mean(oh * oh, axis=-1, keepdims=True)
        outs.append(oh * lax.rsqrt(ms + EPS) * gain[:, h * LANES:(h + 1) * LANES]
                    * _silu(g[:, h * LANES:(h + 1) * LANES]))
    o_ref[...] = jnp.concatenate(outs, axis=1)

    @pl.when(step == nsteps - 1)
    def _():
        sfin_ref[0] = st_ref[...]


def _scan_consts(ts):
    r = np.arange(ts)
    tri = ((r[:, None] // SUB == r[None, :] // SUB) & (r[None, :] <= r[:, None])).astype(np.float32)
    row = np.arange(2 * LANES)[:, None] // LANES
    col = np.arange(LANES)[None, :] // HEAD
    bmask = (row == col).astype(np.float32)
    return jnp.asarray(tri, BF16), jnp.asarray(bmask, F32)


def _ignore_last_input(fn, n_in):
    def wrapped(*refs):
        return fn(*refs[:n_in], *refs[n_in + 1:])
    return wrapped


def _shared_rows(into, rows, width):
    assert into.shape == (rows, width)
    return jax.ShapeDtypeStruct((rows, width), F32), [into], [pl.BlockSpec(memory_space=pl.ANY)]


def linear_scan(q_arr, q_col, k_arr, k_col, v_arr, v_col, g_arr, g_col, la, gain, s0,
                *, row0, nb, t, ts, la_shared, out_rows, into, name):
    nsteps = t // ts
    blk0 = row0 // ts
    tri, bmask = _scan_consts(ts)
    src = lambda w, c: pl.BlockSpec((ts, w), lambda b, s, c=c: (blk0 + b * nsteps + s, c))
    la_spec = pl.BlockSpec((ts, 256), lambda b, s: (0, 0)) if la_shared else src(256, 0)
    o_shape, extra, extra_specs = _shared_rows(into, out_rows, 512)
    n_in = 9
    kern = functools.partial(_scan_kernel, ts=ts, nsteps=nsteps)
    return pl.pallas_call(
        _ignore_last_input(kern, n_in) if extra else kern,
        grid=(nb, nsteps),
        in_specs=[src(256, q_col), src(256, k_col), src(512, v_col), src(512, g_col), la_spec,
                  pl.BlockSpec((1, 512), lambda b, s: (0, 0)),
                  pl.BlockSpec((1, 2, 256, LANES), lambda b, s: (b, 0, 0, 0)),
                  pl.BlockSpec((ts, ts), lambda b, s: (0, 0)),
                  pl.BlockSpec((256, LANES), lambda b, s: (0, 0))] + extra_specs,
        out_specs=[src(512, 0), pl.BlockSpec((1, 2, 256, LANES), lambda b, s: (b, 0, 0, 0))],
        out_shape=[o_shape, jax.ShapeDtypeStruct((nb, 2, 256, LANES), F32)],
        scratch_shapes=[pltpu.VMEM((2, 256, LANES), F32)],
        input_output_aliases={n_in: 0} if extra else {},
        compiler_params=_cparams(("parallel", "arbitrary")),
        name=name,
    )(q_arr, k_arr, v_arr, g_arr, la, gain, s0, tri, bmask, *extra)


def _state_to_pairs(s):
    nb = s.shape[0]
    st = jnp.swapaxes(s, -1, -2).reshape(nb, 2, 2, LANES, HEAD)
    z = jnp.zeros_like(st[:, :, 0])
    top = jnp.concatenate([st[:, :, 0], z], axis=-1)
    bot = jnp.concatenate([z, st[:, :, 1]], axis=-1)
    return jnp.concatenate([top, bot], axis=-2)


def _pairs_to_state(sp):
    nb = sp.shape[0]
    h0 = sp[:, :, :LANES, :HEAD]
    h1 = sp[:, :, LANES:, HEAD:]
    st = jnp.stack([h0, h1], axis=2).reshape(nb, 4, LANES, HEAD)
    return jnp.swapaxes(st, -1, -2)


def _cumsum_kernel(x_ref, tri_ref, o_ref, carry_ref):
    @pl.when(pl.program_id(1) == 0)
    def _():
        carry_ref[...] = jnp.zeros_like(carry_ref)

    tri = tri_ref[...]
    a1, a2, a3 = _split3(x_ref[...])
    c = _dot(tri, a1) + _dot(tri, a2) + _dot(tri, a3) + carry_ref[0:1, :]
    o_ref[...] = c * LOG2E
    carry_ref[...] = jnp.broadcast_to(c[-1:, :], carry_ref.shape)


def row_cumsum(x, *, row0, nb, n, tb):
    r = np.arange(tb)
    tri = jnp.asarray((r[None, :] <= r[:, None]).astype(np.float32), BF16)
    nblk = n // tb
    blk0 = row0 // tb
    return pl.pallas_call(
        _cumsum_kernel,
        grid=(nb, nblk),
        in_specs=[pl.BlockSpec((tb, LANES), lambda b, j: (blk0 + b * nblk + j, 0)),
                  pl.BlockSpec((tb, tb), lambda b, j: (0, 0))],
        out_specs=pl.BlockSpec((tb, LANES), lambda b, j: (b * nblk + j, 0)),
        out_shape=jax.ShapeDtypeStruct((nb * n, LANES), F32),
        scratch_shapes=[pltpu.VMEM((SUBLANES, LANES), F32)],
        compiler_params=_cparams(("parallel", "arbitrary")),
        name="logf_cumsum",
    )(x, tri)


def _col_reduce(x, op):
    tk, tq = x.shape
    if tk > FOLD and tk % FOLD == 0:
        x = op(x.reshape(tk // FOLD, FOLD, tq), axis=0)
    return op(x, axis=0, keepdims=True)


def _values_with_ones(vt_head):
    ones = jnp.ones((ONES_ROWS, vt_head.shape[1]), F32)
    return jnp.concatenate([vt_head, ones], axis=0).astype(BF16)


def _attn_update(s, smax, vt_ext, m_ref, acc_ref, idx):
    m_prev = m_ref[idx]
    m_new = jnp.maximum(m_prev, smax)
    alpha = jnp.exp2(m_prev - m_new)
    p = jnp.exp2(s - m_new)
    acc_ref[idx] = alpha * acc_ref[idx] + _dot(vt_ext, p.astype(BF16))
    m_ref[idx] = m_new


def _eye(n):
    r = lax.broadcasted_iota(I32, (n, n), 0)
    c = lax.broadcasted_iota(I32, (n, n), 1)
    return jnp.where(r == c, 1.0, 0.0).astype(BF16)


def _transpose_exact(x, eye):
    a1, a2, a3 = _split3(x)
    return _dot_nt(eye, a1) + _dot_nt(eye, a2) + _dot_nt(eye, a3)


def _write_heads(o_ref, acc_ref, tq):
    eye = _eye(tq)
    for c in range(4):
        a0, a1 = acc_ref[2 * c], acc_ref[2 * c + 1]
        ot = jnp.concatenate([a0[:HEAD] / a0[HEAD:HEAD + 1], a1[:HEAD] / a1[HEAD:HEAD + 1]], axis=0)
        o_ref[:, c * LANES:(c + 1) * LANES] = _transpose_exact(ot, eye)


def _fox_kernel(qi_ref, kj_ref, q_ref, k_ref, v_ref, c_ref, o_ref, m_ref, acc_ref,
                *, tq, tk, q_off):
    s_id = pl.program_id(1)
    qi = qi_ref[s_id]
    kj = kj_ref[s_id]
    j_last = (q_off + (qi + 1) * tq - 1) // tk

    @pl.when(kj == 0)
    def _():
        m_ref[...] = jnp.full_like(m_ref, NEG)
        acc_ref[...] = jnp.zeros_like(acc_ref)

    def block(masked):
        visible = None
        if masked:
            kpos = kj * tk + lax.broadcasted_iota(I32, (tk, tq), 0)
            qpos = q_off + qi * tq + lax.broadcasted_iota(I32, (tk, tq), 1)
            visible = kpos <= qpos
        qs, ks = _pair_masked_operands(q_ref[...], k_ref[...])
        v = v_ref[...]
        vts = []
        for p in range(4):
            vt = v[:, p * LANES:(p + 1) * LANES].T
            vts += [vt[:HEAD], vt[HEAD:]]
        _fox_block(qs, ks, vts, c_ref[...], visible, m_ref, acc_ref)

    @pl.when(kj < j_last)
    def _():
        block(False)

    @pl.when(kj == j_last)
    def _():
        block(True)
        _write_heads(o_ref, acc_ref, tq)


def _fox_bounded_kernel(qi_ref, kj_ref, q_ref, k_ref, v_ref, c_ref, bnd_ref, o_ref, m_ref, acc_ref,
                        *, tq, tk, q_off):
    s_id = pl.program_id(1)
    qi = qi_ref[s_id]
    kj = kj_ref[s_id]
    j_last = (q_off + (qi + 1) * tq - 1) // tk

    @pl.when(kj == 0)
    def _():
        acc_ref[...] = jnp.zeros_like(acc_ref)
        bt = _transpose_exact(bnd_ref[...], _eye(LANES))
        for h in range(B_HEADS):
            m_ref[h] = bt[h:h + 1, :]

    def block(masked):
        qs, ks = _pair_masked_operands(q_ref[...], k_ref[...])
        c = c_ref[...]
        if masked:
            kpos = kj * tk + lax.broadcasted_iota(I32, (tk, tq), 0)
            qpos = q_off + qi * tq + lax.broadcasted_iota(I32, (tk, tq), 1)
            visible = kpos <= qpos
        probs = []
        for h in range(B_HEADS):
            e = _dot_nt(ks[h], qs[h]) - c[:, h:h + 1] - m_ref[h]
            if masked:
                e = jnp.where(visible, e, NEG)
            probs.append(jnp.exp2(e).astype(BF16))
        v = v_ref[...]
        for p in range(4):
            vt = v[:, p * LANES:(p + 1) * LANES].T
            for hh in range(2):
                h = 2 * p + hh
                acc_ref[h] = acc_ref[h] + _dot(_values_with_ones(vt[hh * HEAD:(hh + 1) * HEAD]), probs[h])

    @pl.when(kj < j_last)
    def _():
        block(False)

    @pl.when(kj == j_last)
    def _():
        block(True)
        _write_heads(o_ref, acc_ref, tq)


def _pair_masked_operands(q, k):
    hm = _half_masks()
    qs, ks = [], []
    for p in range(4):
        kp = k[:, p * LANES:(p + 1) * LANES].astype(BF16)
        for hh in range(2):
            qs.append((q[:, p * LANES:(p + 1) * LANES] * hm[hh]).astype(BF16))
            ks.append(kp)
    return qs, ks


def _fox_block(qs, ks, vts, c, visible, m_ref, acc_ref):
    scores, maxima = [], []
    for h in range(B_HEADS):
        s = _dot_nt(ks[h], qs[h]) - c[:, h:h + 1]
        if visible is not None:
            s = jnp.where(visible, s, NEG)
        scores.append(s)
        maxima.append(_col_reduce(s, jnp.max))
    for h in range(B_HEADS):
        _attn_update(scores[h], maxima[h], _values_with_ones(vts[h]), m_ref, acc_ref, h)


def _fox_decode_kernel(q_ref, kp_ref, vp_ref, lp_ref, kn_ref, vn_ref, ln_ref, tri_ref, o_ref,
                       m_ref, acc_ref, *, ts, past, tb):
    m_ref[...] = jnp.full_like(m_ref, NEG)
    acc_ref[...] = jnp.zeros_like(acc_ref)
    eye = _eye(LANES)
    tri = tri_ref[...]

    def cumsum(x, tri_blk, carry):
        a1, a2, a3 = _split3(x)
        return _dot(tri_blk, a1) + _dot(tri_blk, a2) + _dot(tri_blk, a3) + carry

    def transposed(v):
        return _dot_nt(eye[:v.shape[1], :v.shape[1]], v.astype(BF16))

    lp = lp_ref[...]
    carry = jnp.zeros((1, lp.shape[1]), F32)
    c_blocks = []
    for j in range(past // tb):
        cb = cumsum(lp[j * tb:(j + 1) * tb], tri, carry)
        carry = cb[-1:, :]
        c_blocks.append(cb)
    c_past = jnp.concatenate(c_blocks, axis=0) * LOG2E
    c_new = cumsum(ln_ref[:, :B_HEADS], tri[:ts, :ts], carry) * LOG2E
    q = q_ref[...]
    def attend(k, v, c, visible):
        qs, ks = _pair_masked_operands(q, k)
        vts = []
        for p in range(4):
            vt = transposed(v[:, p * LANES:(p + 1) * LANES])
            vts += [vt[:HEAD], vt[HEAD:]]
        _fox_block(qs, ks, vts, c, visible, m_ref, acc_ref)

    attend(kp_ref[...], vp_ref[...], c_past, None)
    kpos = lax.broadcasted_iota(I32, (ts, ts), 0)
    qpos = lax.broadcasted_iota(I32, (ts, ts), 1)
    attend(kn_ref[...], vn_ref[...], c_new, kpos <= qpos)
    _write_heads(o_ref, acc_ref, ts)


def fox_decode(q_arr, k_past, v_past, lf_past, k_new, v_new, v_col, lf_new, *, row0, nb, ts, past,
               out_rows, into, name):
    tb = _pick_tile(past, (256, 128, 64))
    r = np.arange(tb)
    tri = jnp.asarray((r[None, :] <= r[:, None]).astype(np.float32), BF16)
    blk0 = row0 // ts
    new = lambda w, c: pl.BlockSpec((ts, w), lambda b, c=c: (blk0 + b, c))
    old = pl.BlockSpec((past, 512), lambda b: (b, 0))
    o_shape, extra, extra_specs = _shared_rows(into, out_rows, 512)
    n_in = 8
    kern = functools.partial(_fox_decode_kernel, ts=ts, past=past, tb=tb)
    return pl.pallas_call(
        _ignore_last_input(kern, n_in) if extra else kern,
        grid=(nb,),
        in_specs=[new(512, 0), old, old, pl.BlockSpec((past, B_HEADS), lambda b: (b, 0)),
                  new(512, 0), new(512, v_col),
                  new(LANES, 0), pl.BlockSpec((tb, tb), lambda b: (0, 0))] + extra_specs,
        out_specs=new(512, 0),
        out_shape=o_shape,
        scratch_shapes=[pltpu.VMEM((8, 1, ts), F32), pltpu.VMEM((8, HEAD + ONES_ROWS, ts), F32)],
        input_output_aliases={n_in: 0} if extra else {},
        compiler_params=_cparams(("parallel",)),
        name=name,
    )(q_arr, k_past, v_past, lf_past, k_new, v_new, lf_new, tri, *extra)


def fox_attention(q_arr, k_arr, v_arr, v_col, c_arr, bound, *, q_row0, nb, tq_total, tk_total, q_off,
                  tq, tk, out_rows, into, name):
    nq = tq_total // tq
    steps = [(i, j) for i in range(nq) for j in range((q_off + (i + 1) * tq - 1) // tk + 1)]
    qi = jnp.asarray([s[0] for s in steps], I32)
    kj = jnp.asarray([s[1] for s in steps], I32)
    qb0 = q_row0 // tq
    nkb = tk_total // tk
    o_shape, extra, extra_specs = _shared_rows(into, out_rows, 512)
    qrows = lambda w: pl.BlockSpec((tq, w), lambda b, s, qi, kj: (qb0 + b * nq + qi[s], 0))
    operands = [q_arr, k_arr, v_arr, c_arr]
    in_specs = [qrows(512),
                pl.BlockSpec((tk, 512), lambda b, s, qi, kj: (b * nkb + kj[s], 0)),
                pl.BlockSpec((tk, 512), lambda b, s, qi, kj: (b * nkb + kj[s], v_col)),
                pl.BlockSpec((tk, LANES), lambda b, s, qi, kj: (b * nkb + kj[s], 0))]
    body = _fox_kernel
    if bound is not None:
        operands.append(bound)
        in_specs.append(qrows(LANES))
        body = _fox_bounded_kernel
    n_in = 2 + len(operands)
    grid_spec = pltpu.PrefetchScalarGridSpec(
        num_scalar_prefetch=2,
        grid=(nb, len(steps)),
        in_specs=in_specs + extra_specs,
        out_specs=qrows(512),
        scratch_shapes=[pltpu.VMEM((8, 1, tq), F32), pltpu.VMEM((8, HEAD + ONES_ROWS, tq), F32)],
    )
    kern = functools.partial(body, tq=tq, tk=tk, q_off=q_off)
    return pl.pallas_call(
        _ignore_last_input(kern, n_in) if extra else kern,
        grid_spec=grid_spec,
        out_shape=o_shape,
        input_output_aliases={n_in: 0} if extra else {},
        compiler_params=_cparams(("parallel", "arbitrary")),
        name=name,
    )(qi, kj, *operands, *extra)


def _head_sum_matrix():
    r = lax.broadcasted_iota(I32, (4 * LANES, LANES), 0)
    c = lax.broadcasted_iota(I32, (4 * LANES, LANES), 1)
    return jnp.where(r // HEAD == c, 1.0, 0.0).astype(BF16)


def _sortable_key(x):
    b = lax.bitcast_convert_type(x + 0.0, I32)
    return jnp.where(b < 0, b ^ jnp.int32(0x7FFFFFFF), b)


def _dsa_kernel(iq_ref, misc_ref, qc_ref, kc_ref, vc_ref, ik_ref, o_ref,
                keys_ref, half_ref, qm_ref, m_ref, acc_ref, thr_ref, jcut_ref,
                *, tq, tk, n_keys, q_off, ksel):
    i = pl.program_id(1)
    hm = _half_masks()
    qcol = lax.broadcasted_iota(I32, (1, tq), 1)
    lim = jnp.minimum(((q_off + i * tq + qcol) // CHUNK + 1) * CHUNK, n_keys)
    lim_max = jnp.minimum(((q_off + (i + 1) * tq - 1) // CHUNK + 1) * CHUNK, n_keys)
    nkb = (lim_max + tk - 1) // tk
    eye = _eye(LANES)

    def key_pos(kb):
        return kb * tk + lax.broadcasted_iota(I32, (tk, tq), 0)

    iq = iq_ref[...]
    for hd in range(IDX_HEADS):
        c0 = (hd // 2) * LANES
        qm_ref[hd] = (iq[:, c0:c0 + LANES] * hm[hd % 2]).astype(BF16)
    iwt = _transpose_exact(misc_ref[...], eye) * IDX_SCALE

    def score_body(kb, kk_max):
        k0 = pl.multiple_of(kb * tk, tk)
        ikb = ik_ref[pl.ds(k0, tk), :]
        sc = jnp.zeros((tk, tq), F32)
        for hd in range(IDX_HEADS):
            a = _dot_nt(ikb, qm_ref[hd])
            sc = sc + jnp.maximum(a, 0.0) * iwt[MISC_IW + hd:MISC_IW + hd + 1, :]
        key = jnp.where(key_pos(kb) < lim, _sortable_key(sc), INT_MIN)
        keys_ref[kb] = key
        half_ref[kb] = jnp.right_shift(key, 16).astype(I16)
        kf32 = kc_ref[pl.ds(k0, tk), :].astype(F32)
        kk = _dot((kf32 * kf32).astype(BF16), half_ones)
        return jnp.maximum(kk_max, _col_reduce(kk, jnp.max))

    lane_r = lax.broadcasted_iota(I32, (LANES, LANES), 0)
    lane_c = lax.broadcasted_iota(I32, (LANES, LANES), 1)
    half_ones = jnp.where(lane_r // HEAD == lane_c // HEAD, 1.0, 0.0).astype(BF16)
    kk_max = lax.fori_loop(0, nkb, score_body, jnp.zeros((1, LANES), F32))

    def count(pred):
        def body(kb, acc):
            hit = jnp.where(pred(keys_ref[kb], key_pos(kb)), 1.0, 0.0)
            return acc + jnp.sum(hit.reshape(tk // COUNT_FOLD, COUNT_FOLD, tq), axis=0)
        acc = lax.fori_loop(0, nkb, body, jnp.zeros((COUNT_FOLD, tq), F32))
        return jnp.sum(acc, axis=0, keepdims=True)

    def count_half(cand):
        def body(kb, acc):
            hit = jnp.where(half_ref[kb] >= cand, jnp.int16(1), jnp.int16(0))
            parts = [hit[r * COUNT_FOLD:(r + 1) * COUNT_FOLD] for r in range(tk // COUNT_FOLD)]
            while len(parts) > 1:
                parts = [a + b for a, b in zip(parts[0::2], parts[1::2])] + parts[len(parts) & ~1:]
            return acc + parts[0]
        acc = lax.fori_loop(0, nkb, body, jnp.zeros((COUNT_FOLD, tq), I16))
        return jnp.sum(acc.astype(F32), axis=0, keepdims=True)

    def any_row(flag):
        return jnp.max(jnp.where(flag, 1.0, 0.0)) > 0.0

    thr_ref[...] = jnp.full((1, tq), INT_MIN + 1, I32)
    jcut_ref[...] = jnp.full((1, tq), 2 ** 30, I32)
    kf = float(ksel)

    @pl.when(lim_max > ksel)
    def _():
        few = lim <= ksel

        def unsettled(cres):
            return jnp.logical_and(cres != kf, jnp.logical_not(few))

        def search(first, last, group, to_half, state):
            assert (last - first) % group == 0

            def cond(state):
                t, _, _, go = state
                return jnp.logical_and(t < last, go)

            def body(state):
                t, res, cres, _ = state
                for g in range(group):
                    cand = res + jnp.left_shift(jnp.int32(1), 30 - g - t)
                    cnt = count_half(to_half(cand))
                    take = cnt >= kf
                    res = jnp.where(take, cand, res)
                    cres = jnp.where(take, cnt, cres)
                return t + group, res, cres, any_row(unsettled(cres))

            return lax.while_loop(cond, body, (jnp.int32(first),) + state)[1:]

        def high_half(x):
            return jnp.right_shift(x, 16).astype(I16)

        def low_half(x):
            return ((x & 0xFFFF) - 32768).astype(I16)

        c0 = count_half(jnp.zeros((1, tq), I16))
        res = jnp.where(c0 >= kf, 0, INT_MIN).astype(I32)
        cres = jnp.where(c0 >= kf, c0, 2.0 * n_keys)
        res, cres, go = search(0, 15, 3, high_half, (res, cres, any_row(unsettled(cres))))

        @pl.when(go)
        def _():
            rh = high_half(res)

            def relabel(kb, carry):
                hi = half_ref[kb]
                lo = low_half(keys_ref[kb])
                half_ref[kb] = jnp.where(hi > rh, jnp.int16(32767),
                                         jnp.where(hi < rh, jnp.int16(-32768), lo))
                return carry

            lax.fori_loop(0, nkb, relabel, 0)

        res, cres, open_rows = search(15, 31, 4, low_half, (res, cres, go))
        thr = jnp.maximum(res, INT_MIN + 1)
        thr_ref[...] = thr

        @pl.when(open_rows)
        def _():
            need = kf - count(lambda blk, kpos: blk > thr)
            n_eq = count(lambda blk, kpos: blk == thr)
            split = n_eq > need

            @pl.when(any_row(split))
            def _():
                def idx_body(t, lo):
                    cand = lo + jnp.left_shift(jnp.int32(1), 14 - t)
                    cnt = count(lambda blk, kpos: (blk == thr) & (kpos < cand))
                    return jnp.where(cnt < need, cand, lo)

                lo = lax.fori_loop(0, 15, idx_body, jnp.zeros((1, tq), I32))
                jcut_ref[...] = jnp.where(split, lo, 2 ** 30)

    qc = qc_ref[...]
    for hd in range(C_HEADS):
        c0 = (hd // 2) * LANES
        qm_ref[hd] = (qc[:, c0:c0 + LANES] * hm[hd % 2]).astype(BF16)
    thr = thr_ref[...]
    jcut = jcut_ref[...]

    def block_operands(kb):
        k0 = pl.multiple_of(kb * tk, tk)
        kblk = kc_ref[pl.ds(k0, tk), :]
        vt = _dot_nt(eye, vc_ref[pl.ds(k0, tk), :])
        vts = [_values_with_ones(vt[hh * HEAD:(hh + 1) * HEAD]) for hh in range(2)]
        keys = keys_ref[kb]
        sel = (keys > thr) | ((keys == thr) & (key_pos(kb) <= jcut))
        return kblk, vts, sel

    qq = _dot((qc * qc).astype(BF16), _head_sum_matrix())
    qq_t = _dot_nt(eye, qq.astype(BF16))
    bound = [jnp.sqrt(qq_t[hd:hd + 1, :] * kk_max[:, (hd % 2) * HEAD:(hd % 2) * HEAD + 1]) * BOUND_SLACK
             for hd in range(C_HEADS)]
    acc_ref[...] = jnp.zeros_like(acc_ref)

    def fast_body(kb, carry):
        kblk, vts, sel = block_operands(kb)
        keep = jnp.where(sel, 1.0, 0.0).astype(BF16)
        probs = [jnp.exp2(_dot_nt(kblk, qm_ref[hd]) - bound[hd]).astype(BF16) * keep
                 for hd in range(C_HEADS)]
        for hd in range(C_HEADS):
            acc_ref[hd] = acc_ref[hd] + _dot(vts[hd % 2], probs[hd])
        return carry

    lax.fori_loop(0, nkb, fast_body, 0)
    denom_min = acc_ref[0][HEAD:HEAD + 1, :]
    for hd in range(1, C_HEADS):
        denom_min = jnp.minimum(denom_min, acc_ref[hd][HEAD:HEAD + 1, :])

    @pl.when(jnp.min(denom_min) < DENOM_FLOOR)
    def _():
        m_ref[...] = jnp.full_like(m_ref, NEG)
        acc_ref[...] = jnp.zeros_like(acc_ref)

        def exact_body(kb, carry):
            kblk, vts, sel = block_operands(kb)
            scores = [jnp.where(sel, _dot_nt(kblk, qm_ref[hd]), NEG) for hd in range(C_HEADS)]
            maxima = [_col_reduce(s, jnp.max) for s in scores]
            for hd in range(C_HEADS):
                _attn_update(scores[hd], maxima[hd], vts[hd % 2], m_ref, acc_ref, hd)
            return carry

        lax.fori_loop(0, nkb, exact_body, 0)

    _write_heads(o_ref, acc_ref, tq)


def dsa_attention(iq_arr, misc_arr, misc_col, qc_arr, kc, vc, ik, *, q_row0, nb, tq_total,
                  n_keys, n_keys_pad, q_off, tq, tk, out_rows, into, name):
    nq = tq_total // tq
    qb0 = q_row0 // tq
    ksel = min(TOPK_MAX, n_keys // 4)
    qspec = lambda w, c: pl.BlockSpec((tq, w), lambda b, i, c=c: (qb0 + b * nq + i, c))
    kspec = pl.BlockSpec((n_keys_pad, LANES), lambda b, i: (b, 0))
    o_shape, extra, extra_specs = _shared_rows(into, out_rows, 512)
    n_in = 6
    kern = functools.partial(_dsa_kernel, tq=tq, tk=tk, n_keys=n_keys, q_off=q_off, ksel=ksel)
    return pl.pallas_call(
        _ignore_last_input(kern, n_in) if extra else kern,
        grid=(nb, nq),
        in_specs=[qspec(512, 0), qspec(LANES, misc_col), qspec(512, 0), kspec, kspec, kspec]
        + extra_specs,
        out_specs=qspec(512, 0),
        out_shape=o_shape,
        input_output_aliases={n_in: 0} if extra else {},
        scratch_shapes=[pltpu.VMEM((n_keys_pad // tk, tk, tq), I32),
                        pltpu.VMEM((n_keys_pad // tk, tk, tq), I16),
                        pltpu.VMEM((8, tq, LANES), BF16),
                        pltpu.VMEM((8, 1, tq), F32),
                        pltpu.VMEM((8, HEAD + ONES_ROWS, tq), F32),
                        pltpu.VMEM((1, tq), I32), pltpu.VMEM((1, tq), I32)],
        compiler_params=_cparams(("parallel", "arbitrary")),
        name=name,
    )(iq_arr, misc_arr, qc_arr, kc, vc, ik, *extra)


def _rope_tables(pos):
    half = HEAD // 2
    inv = jnp.power(ROPE_THETA, -jnp.arange(half, dtype=F32) / half)
    ang = pos.astype(F32)[:, None] * inv[None, :]
    cos, sin = jnp.cos(ang), jnp.sin(ang)
    z = jnp.zeros_like(sin)
    return (jnp.tile(cos, (1, 4)), jnp.tile(jnp.concatenate([-sin, z], 1), (1, 2)),
            jnp.tile(jnp.concatenate([z, sin], 1), (1, 2)))


def _head_mean_matrix():
    r = np.arange(512)
    return jnp.asarray((r[:, None] // HEAD == r[None, :] // HEAD).astype(np.float32), BF16)


def _pick_tile(n, prefs):
    for t in prefs:
        if n % t == 0:
            return t
    raise ValueError(f"no tile in {prefs} divides {n}")


def _pad_keys(x, n_pad):
    nb, n, w = x.shape
    if n_pad > n:
        x = jnp.concatenate([x, jnp.zeros((nb, n_pad - n, w), x.dtype)], axis=1)
    return x.reshape(nb * n_pad, w)


def kernel(x_prompt, x_sample, state_a, cache_b_k, cache_b_v, cache_b_logf, cache_c_k, cache_c_v,
           cache_c_idx, state_d, norm_mix, norm_ffn, even_w_in, even_b_f, even_q_norm, even_k_norm,
           even_w_out, odd_w_in, odd_w_gate_up, odd_b_gate, odd_q_norm, odd_k_norm, odd_o_norm,
           odd_w_out, ffn_w1, ffn_w2):
    bp, tp = x_prompt.shape[:2]
    nb, ts = x_sample.shape[:2]
    past = cache_b_k.shape[2]
    depth = norm_mix.shape[0]
    rp, rs = bp * tp, nb * ts
    rows = rp + rs
    assert bp == 1 and tp % 128 == 0 and ts % SUB == 0 and rp % ts == 0 and past % CHUNK == 0

    y = jnp.concatenate([x_prompt.reshape(rp, D_MODEL), x_sample.reshape(rs, D_MODEL)], axis=0)
    pos = jnp.concatenate([jnp.arange(tp, dtype=I32),
                           jnp.tile(past + jnp.arange(ts, dtype=I32), nb)])
    cos, slo, shi = _rope_tables(pos)
    gmat = _head_mean_matrix()

    tm = _pick_tile(rows, (512, 256, 128, 64, 32))
    tm_prep = _pick_tile(rows, (256, 128, 64, 32))
    ts_p = _pick_tile(tp, (256, 128, 64, 32, 16))
    tq_fox = _pick_tile(tp, (512, 256, 128))
    tk_fox = _pick_tile(tp, (512, 256, 128))
    tq_dsa = _pick_tile(tp, (256, 128))
    tk_dsa = _pick_tile(tp, (512, 256, 128))
    nk_s = past + ts
    nk_s_pad = -(-nk_s // 384) * 384
    tb_p = _pick_tile(tp, (512, 256, 128))

    lg = jnp.log1p(-jnp.exp2(-5.0 - jnp.arange(A_HEADS, dtype=F32)))
    la_ret = jnp.repeat(lg, A_DK)[None, :]
    ones_gain = jnp.ones((1, 512), F32)
    blank = lambda: jnp.zeros((rows, 512), F32)
    zero_state = jnp.zeros((bp, 2, 256, LANES), F32)

    a_p, a_s, bk_p, bk_s, bv_p, bv_s, bf_p, bf_s = [], [], [], [], [], [], [], []
    ck_p, ck_s, cv_p, cv_s, ci_p, ci_s, d_p, d_s = [], [], [], [], [], [], [], []

    for l in range(depth):
        i = l // 2
        if l % 2 == 0:
            w_in = jnp.concatenate(
                [even_w_in[i], jnp.zeros((D_MODEL, EV_WIDTH - even_w_in.shape[2]), F32)], 1).astype(BF16)
            h = norm_matmul(y, norm_mix[l], w_in, tm=tm, tn=EV_WIDTH, name="even_in_proj")
            bf = jnp.concatenate([even_b_f[i], jnp.zeros((LANES - B_HEADS,), F32)])[None, :]
            qka, qbn, kbn, lf, qq, kk, qk = prep_even(
                h, cos, slo, shi, jnp.tile(even_q_norm[i], B_HEADS)[None, :],
                jnp.tile(even_k_norm[i], B_HEADS)[None, :], bf, gmat, tm=tm_prep)
            oa, sa_p = linear_scan(qka, 0, qka, 1, h, EV_VA // 512, h, EV_GA // 512,
                                   jnp.broadcast_to(la_ret, (ts_p, 256)), ones_gain, zero_state,
                                   row0=0, nb=bp, t=tp, ts=ts_p, la_shared=True, out_rows=rows,
                                   into=blank(), name="retention_prompt")
            oa, sa_s = linear_scan(qka, 0, qka, 1, h, EV_VA // 512, h, EV_GA // 512,
                                   jnp.broadcast_to(la_ret, (ts, 256)), ones_gain,
                                   _state_to_pairs(state_a[i]),
                                   row0=rp, nb=nb, t=ts, ts=ts, la_shared=True, out_rows=rows,
                                   into=oa, name="retention_decode")
            lf8 = lf[:, :B_HEADS]
            c_p = row_cumsum(lf, row0=0, nb=bp, n=tp, tb=tb_p)
            qk_bound = jnp.sqrt(qq[:rp] * jnp.max(kk[:rp], axis=0, keepdims=True)) * BOUND_SLACK
            tight = jnp.max((qk_bound - qk[:rp])[:, :B_HEADS]) < FOX_BOUND_RANGE
            fox_prompt = functools.partial(
                fox_attention, qbn, kbn, h, EV_VB // 512, c_p, q_row0=0, nb=bp, tq_total=tp,
                tk_total=tp, q_off=0, tq=tq_fox, tk=tk_fox, out_rows=rows, into=blank())
            ob = lax.cond(tight,
                          lambda: fox_prompt(qk_bound - c_p, name="fox_prompt_bounded"),
                          lambda: fox_prompt(None, name="fox_prompt"))
            kb_new = kbn[rp:].reshape(nb, ts, 512)
            vb_new = h[rp:, EV_VB:EV_VB + 512].reshape(nb, ts, 512)
            ob = fox_decode(qbn, cache_b_k[i].reshape(nb * past, 512).astype(BF16),
                            cache_b_v[i].reshape(nb * past, 512).astype(BF16),
                            cache_b_logf[i].reshape(nb * past, B_HEADS), kbn, h, EV_VB // 512, lf,
                            row0=rp, nb=nb, ts=ts, past=past, out_rows=rows, into=ob,
                            name="fox_decode")
            w_out = even_w_out[i].astype(BF16)
            y = matmul_residual(y, [oa, ob], [w_out[:512], w_out[512:]], tm=tm, name="even_out_proj")
            a_p.append(_pairs_to_state(sa_p))
            a_s.append(_pairs_to_state(sa_s))
            bk_p.append(kbn[:rp].reshape(bp, tp, B_HEADS, B_DH))
            bk_s.append(kb_new.reshape(nb, ts, B_HEADS, B_DH))
            bv_p.append(h[:rp, EV_VB:EV_VB + 512].reshape(bp, tp, B_HEADS, B_DH))
            bv_s.append(vb_new.reshape(nb, ts, B_HEADS, B_DH))
            bf_p.append(lf8[:rp].reshape(bp, tp, B_HEADS))
            bf_s.append(lf8[rp:].reshape(nb, ts, B_HEADS))
        else:
            w = odd_w_in[i]
            offs = np.cumsum([0, 512, 128, 128, 512, 64, 8, 256, 256, 512, 512, 16])
            qc_w, kc_w, vc_w, iq_w, ik_w, iw_w, qd_w, kd_w, vd_w, gd_w, gr_w = [
                w[:, int(a):int(b)] for a, b in zip(offs[:-1], offs[1:])]
            qc_w = qc_w.reshape(D_MODEL, 2, 4, C_DH).transpose(0, 2, 1, 3).reshape(D_MODEL, 512)
            w_in = jnp.concatenate(
                [qc_w, iq_w, vd_w, gd_w, qd_w, kd_w, kc_w, vc_w, ik_w, iw_w, gr_w,
                 jnp.zeros((D_MODEL, OD_WIDTH - OD_MISC - 88), F32)], 1).astype(BF16)
            h = norm_matmul(y, norm_mix[l], w_in, tm=tm, tn=OD_WIDTH, name="odd_in_proj")
            wg = jnp.zeros((LANES, 256), F32).at[MISC_GR:MISC_GR + D_GATE_RANK].set(
                odd_w_gate_up[i]).astype(BF16)
            qcr, iqr, qkd, kcr, ikr, kcb, vcb, ikb, la = prep_odd(
                h, cos, slo, shi, jnp.tile(odd_q_norm[i], C_HEADS)[None, :],
                jnp.tile(odd_k_norm[i], C_KV_HEADS)[None, :], wg, odd_b_gate[i][None, :], gmat,
                tm=tm_prep)
            oc = dsa_attention(iqr, h, OD_MISC // LANES, qcr, kcb, vcb, ikb,
                               q_row0=0, nb=bp, tq_total=tp, n_keys=tp, n_keys_pad=tp, q_off=0,
                               tq=tq_dsa, tk=tk_dsa, out_rows=rows, into=blank(), name="dsa_prompt")

            def with_past(cache, new):
                return _pad_keys(jnp.concatenate(
                    [cache.reshape(nb, past, -1).astype(BF16), new[rp:].reshape(nb, ts, -1)], 1), nk_s_pad)

            ik_past = jnp.concatenate([cache_c_idx[i], cache_c_idx[i]], axis=-1)
            oc = dsa_attention(iqr, h, OD_MISC // LANES, qcr,
                               with_past(cache_c_k[i], kcb), with_past(cache_c_v[i], vcb),
                               with_past(ik_past, ikb),
                               q_row0=rp, nb=nb, tq_total=ts, n_keys=nk_s, n_keys_pad=nk_s_pad,
                               q_off=past, tq=ts, tk=384, out_rows=rows, into=oc, name="dsa_decode")
            gain = jnp.tile(odd_o_norm[i], D_HEADS)[None, :]
            od, sd_p = linear_scan(qkd, 0, qkd, 1, h, OD_VD // 512, h, OD_GD // 512, la, gain,
                                   zero_state, row0=0, nb=bp, t=tp, ts=ts_p, la_shared=False,
                                   out_rows=rows, into=blank(), name="gla_prompt")
            od, sd_s = linear_scan(qkd, 0, qkd, 1, h, OD_VD // 512, h, OD_GD // 512, la, gain,
                                   _state_to_pairs(state_d[i]), row0=rp, nb=nb, t=ts, ts=ts,
                                   la_shared=False, out_rows=rows, into=od, name="gla_decode")
            w_out = odd_w_out[i]
            w_oc = w_out[:512].reshape(2, 4, C_DH, D_MODEL).transpose(1, 0, 2, 3).reshape(512, D_MODEL)
            y = matmul_residual(y, [oc, od], [w_oc.astype(BF16), w_out[512:].astype(BF16)], tm=tm,
                                name="odd_out_proj")
            d_p.append(_pairs_to_state(sd_p))
            d_s.append(_pairs_to_state(sd_s))
            ck_p.append(kcr[:rp].reshape(bp, tp, C_KV_HEADS, C_DH))
            ck_s.append(kcr[rp:].reshape(nb, ts, C_KV_HEADS, C_DH))
            cv_p.append(h[:rp, OD_KVC + LANES:OD_KVC + 2 * LANES].reshape(bp, tp, C_KV_HEADS, C_DH))
            cv_s.append(h[rp:, OD_KVC + LANES:OD_KVC + 2 * LANES].reshape(nb, ts, C_KV_HEADS, C_DH))
            ci_p.append(ikr[:rp, :IDX_DIM].reshape(bp, tp, IDX_DIM))
            ci_s.append(ikr[rp:, :IDX_DIM].reshape(nb, ts, IDX_DIM))
        hid = norm_matmul(y, norm_ffn[l], ffn_w1[l].astype(BF16), tm=tm, tn=D_FF, relu2=True,
                          out_dtype=BF16, name="mlp_up")
        y = matmul_residual(y, [hid], [ffn_w2[l].astype(BF16)], tm=tm, name="mlp_down")

    return (y[:rp].reshape(bp, tp, D_MODEL), y[rp:].reshape(nb, ts, D_MODEL),
            jnp.stack(a_p), jnp.stack(a_s), jnp.stack(bk_p), jnp.stack(bk_s),
            jnp.stack(bv_p), jnp.stack(bv_s), jnp.stack(bf_p), jnp.stack(bf_s),
            jnp.stack(ck_p), jnp.stack(ck_s), jnp.stack(cv_p), jnp.stack(cv_s),
            jnp.stack(ci_p), jnp.stack(ci_s), jnp.stack(d_p), jnp.stack(d_s))
```

```python
import functools

import numpy as np
import jax
import jax.numpy as jnp
from jax import lax
from jax.experimental import pallas as pl
from jax.experimental.pallas import tpu as pltpu

F32 = jnp.float32
BF16 = jnp.bfloat16
I32 = jnp.int32
I16 = jnp.int16

D_MODEL = 1024
CHUNK = 64
ROPE_THETA = 10000.0
EPS = 1e-6
A_HEADS, A_DK, A_DV = 4, 64, 128
B_HEADS, B_DH = 8, 64
C_HEADS, C_KV_HEADS, C_DH = 8, 2, 64
IDX_HEADS, IDX_DIM = 8, 64
IDX_SCALE = (IDX_HEADS * IDX_DIM) ** -0.5
TOPK_MAX = 256
D_HEADS, D_DK, D_DV = 4, 64, 128
D_GATE_RANK = 16
D_GATE_NORM = 16.0
D_FF = 4 * D_MODEL

LANES = 128
SUBLANES = 8
VMEM_LIMIT_BYTES = 56 * 1024 * 1024

HEAD = 64
SUB = 16
FOLD = 64
COUNT_FOLD = 32
ONES_ROWS = 16
NEG = -1e30
LOG2E = 1.4426950408889634
BOUND_SLACK = 1.03
DENOM_FLOOR = 2.0 ** -100
FOX_BOUND_RANGE = 90.0
INT_MIN = -(2 ** 31)

EV_QK, EV_VA, EV_GA, EV_QB, EV_KB, EV_VB, EV_FB = 0, 512, 1024, 1536, 2048, 2560, 3072
EV_WIDTH = 3200
OD_QC, OD_IQ, OD_VD, OD_GD, OD_QKD, OD_KVC, OD_MISC = 0, 512, 1024, 1536, 2048, 2560, 2816
OD_WIDTH = 2944
MISC_IK, MISC_IW, MISC_GR = 0, 64, 72


def _cparams(sem):
    return pltpu.CompilerParams(dimension_semantics=sem, vmem_limit_bytes=VMEM_LIMIT_BYTES)


def _dot(a, b):
    return jnp.dot(a, b, preferred_element_type=F32)


def _dot_nt(a, b):
    return lax.dot_general(a, b, (((1,), (1,)), ((), ())), preferred_element_type=F32)


def _dot_tn(a, b):
    return lax.dot_general(a, b, (((0,), (0,)), ((), ())), preferred_element_type=F32)


def _split3(x):
    a1 = x.astype(BF16)
    r1 = x - a1.astype(F32)
    a2 = r1.astype(BF16)
    a3 = (r1 - a2.astype(F32)).astype(BF16)
    return a1, a2, a3


def _half_masks():
    lane = lax.broadcasted_iota(I32, (1, LANES), 1)
    lo = (lane < HEAD).astype(F32)
    return lo, 1.0 - lo


def _log_sigmoid(x):
    return jnp.minimum(x, 0.0) - jnp.log1p(jnp.exp(-jnp.abs(x)))


def _silu(x):
    return x / (1.0 + jnp.exp(-x))


def _norm_mm_kernel(x_ref, g_ref, w_ref, o_ref, *, relu2):
    x = x_ref[...]
    ms = jnp.mean(x * x, axis=-1, keepdims=True)
    xn = (x * lax.rsqrt(ms + EPS) * g_ref[...]).astype(BF16)
    y = _dot(xn, w_ref[...])
    if relu2:
        y = jnp.maximum(y, 0.0)
        y = y * y
    o_ref[...] = y.astype(o_ref.dtype)


def norm_matmul(x, g, w, *, tm, tn, relu2=False, out_dtype=F32, name):
    rows, k = x.shape
    n = w.shape[1]
    return pl.pallas_call(
        functools.partial(_norm_mm_kernel, relu2=relu2),
        grid=(rows // tm, n // tn),
        in_specs=[pl.BlockSpec((tm, k), lambda i, j: (i, 0)),
                  pl.BlockSpec((1, k), lambda i, j: (0, 0)),
                  pl.BlockSpec((k, tn), lambda i, j: (0, j))],
        out_specs=pl.BlockSpec((tm, tn), lambda i, j: (i, j)),
        out_shape=jax.ShapeDtypeStruct((rows, n), out_dtype),
        compiler_params=_cparams(("parallel", "parallel")),
        name=name,
    )(x, g.reshape(1, k), w)


def _mm_res_kernel(*refs, n_in):
    res_ref = refs[0]
    o_ref = refs[-1]
    acc = res_ref[...]
    for a_ref, w_ref in zip(refs[1:1 + n_in], refs[1 + n_in:1 + 2 * n_in]):
        acc = acc + _dot(a_ref[...].astype(BF16), w_ref[...])
    o_ref[...] = acc


def matmul_residual(res, a_list, w_list, *, tm, name):
    rows, n = res.shape
    n_in = len(a_list)
    in_specs = [pl.BlockSpec((tm, n), lambda i: (i, 0))]
    in_specs += [pl.BlockSpec((tm, a.shape[1]), lambda i: (i, 0)) for a in a_list]
    in_specs += [pl.BlockSpec(w.shape, lambda i: (0, 0)) for w in w_list]
    return pl.pallas_call(
        functools.partial(_mm_res_kernel, n_in=n_in),
        grid=(rows // tm,),
        in_specs=in_specs,
        out_specs=pl.BlockSpec((tm, n), lambda i: (i, 0)),
        out_shape=jax.ShapeDtypeStruct((rows, n), F32),
        compiler_params=_cparams(("parallel",)),
        name=name,
    )(res, *a_list, *w_list)


def _tile_lanes(t, width):
    n = width // LANES
    return t if n == 1 else jnp.concatenate([t] * n, axis=1)


def _rope(x, cos, sin_lo, sin_hi):
    w = x.shape[1]
    up = pltpu.roll(x, w - HEAD // 2, 1)
    dn = pltpu.roll(x, HEAD // 2, 1)
    return x * _tile_lanes(cos, w) + up * _tile_lanes(sin_lo, w) + dn * _tile_lanes(sin_hi, w)


def _head_rms(x, gmat):
    s = x * x
    hi = s.astype(BF16)
    lo = (s - hi.astype(F32)).astype(BF16)
    ms = (_dot(hi, gmat) + _dot(lo, gmat)) * (1.0 / HEAD)
    return x * lax.rsqrt(ms + EPS)


def _prep_even_kernel(qk_ref, qb_ref, kb_ref, fb_ref, cos_ref, slo_ref, shi_ref,
                      qg_ref, kg_ref, bf_ref, gmat_ref,
                      qka_ref, qbn_ref, kbn_ref, lf_ref, qq_ref, kk_ref, qk_ref_out):
    cos, slo, shi = cos_ref[...], slo_ref[...], shi_ref[...]
    lane = lax.broadcasted_iota(I32, (1, 4 * LANES), 1)
    kscale = jnp.where(lane < 2 * LANES, 1.0, A_DK ** -0.5)
    qka_ref[...] = _rope(qk_ref[...], cos, slo, shi) * kscale
    gmat = gmat_ref[...]
    qbn = _head_rms(qb_ref[...], gmat) * (qg_ref[...] * (B_DH ** -0.5 * LOG2E))
    kbn = _head_rms(kb_ref[...], gmat) * kg_ref[...]
    qbn_ref[...] = qbn
    kbn_ref[...] = kbn
    lf_ref[...] = _log_sigmoid(fb_ref[...] + bf_ref[...])
    hsum = _head_sum_matrix()
    qq_ref[...] = _dot((qbn * qbn).astype(BF16), hsum)
    kk_ref[...] = _dot((kbn * kbn).astype(BF16), hsum)
    qk_ref_out[...] = _dot((qbn * kbn).astype(BF16), hsum)


def prep_even(h, cos, slo, shi, qg, kg, bf, gmat, *, tm):
    rows = h.shape[0]
    blk = lambda w, c: pl.BlockSpec((tm, w), lambda i, c=c: (i, c))
    row = lambda w: pl.BlockSpec((1, w), lambda i: (0, 0))
    return pl.pallas_call(
        _prep_even_kernel,
        grid=(rows // tm,),
        in_specs=[blk(512, EV_QK // 512), blk(512, EV_QB // 512), blk(512, EV_KB // 512),
                  blk(LANES, EV_FB // LANES), blk(LANES, 0), blk(LANES, 0), blk(LANES, 0),
                  row(512), row(512), row(LANES),
                  pl.BlockSpec((512, 512), lambda i: (0, 0))],
        out_specs=[blk(512, 0), blk(512, 0), blk(512, 0)] + [blk(LANES, 0)] * 4,
        out_shape=[jax.ShapeDtypeStruct((rows, 512), F32)] * 3
        + [jax.ShapeDtypeStruct((rows, LANES), F32)] * 4,
        compiler_params=_cparams(("parallel",)),
        name="prep_even",
    )(h, h, h, h, cos, slo, shi, qg, kg, bf, gmat)


def _prep_odd_kernel(qc_ref, iq_ref, qkd_ref, kvc_ref, misc_ref, cos_ref, slo_ref, shi_ref,
                     qg_ref, kg_ref, wg_ref, bg_ref, gmat_ref,
                     qcr_ref, iqr_ref, qkdo_ref, kcr_ref, ikr_ref, kcb_ref, vcb_ref, ikb_ref,
                     la_ref):
    cos, slo, shi = cos_ref[...], slo_ref[...], shi_ref[...]
    gmat = gmat_ref[...]
    qc = _head_rms(qc_ref[...], gmat) * (qg_ref[...] * (C_DH ** -0.5 * LOG2E))
    qcr_ref[...] = _rope(qc, cos, slo, shi)
    iqr_ref[...] = _rope(iq_ref[...], cos, slo, shi)
    lane = lax.broadcasted_iota(I32, (1, 4 * LANES), 1)
    qkdo_ref[...] = qkd_ref[...] * jnp.where(lane < 2 * LANES, D_DK ** -0.5, 1.0)
    kvc = kvc_ref[...]
    kc = _head_rms(kvc[:, :LANES], gmat[:LANES, :LANES]) * kg_ref[...]
    kcr = _rope(kc, cos, slo, shi)
    kcr_ref[...] = kcr
    kcb_ref[...] = kcr.astype(BF16)
    vcb_ref[...] = kvc[:, LANES:].astype(BF16)
    misc = misc_ref[...]
    ikr = _rope(misc, cos, slo, shi)
    lane1 = lax.broadcasted_iota(I32, (1, LANES), 1)
    ik2 = jnp.where(lane1 < HEAD, ikr, pltpu.roll(ikr, HEAD, 1))
    ikr_ref[...] = ik2
    ikb_ref[...] = ik2.astype(BF16)
    z = _dot(misc.astype(BF16), wg_ref[...]) + bg_ref[...]
    la_ref[...] = _log_sigmoid(z) * (1.0 / D_GATE_NORM)


def prep_odd(h, cos, slo, shi, qg, kg, wg, bg, gmat, *, tm):
    rows = h.shape[0]
    blk = lambda w, c: pl.BlockSpec((tm, w), lambda i, c=c: (i, c))
    row = lambda w: pl.BlockSpec((1, w), lambda i: (0, 0))
    return pl.pallas_call(
        _prep_odd_kernel,
        grid=(rows // tm,),
        in_specs=[blk(512, OD_QC // 512), blk(512, OD_IQ // 512), blk(512, OD_QKD // 512),
                  blk(256, OD_KVC // 256), blk(LANES, OD_MISC // LANES),
                  blk(LANES, 0), blk(LANES, 0), blk(LANES, 0),
                  row(512), row(LANES),
                  pl.BlockSpec((LANES, 256), lambda i: (0, 0)), row(256),
                  pl.BlockSpec((512, 512), lambda i: (0, 0))],
        out_specs=[blk(512, 0), blk(512, 0), blk(512, 0), blk(LANES, 0), blk(LANES, 0),
                   blk(LANES, 0), blk(LANES, 0), blk(LANES, 0), blk(256, 0)],
        out_shape=[jax.ShapeDtypeStruct((rows, 512), F32)] * 3
        + [jax.ShapeDtypeStruct((rows, LANES), F32)] * 2
        + [jax.ShapeDtypeStruct((rows, LANES), BF16)] * 3
        + [jax.ShapeDtypeStruct((rows, 256), F32)],
        compiler_params=_cparams(("parallel",)),
        name="prep_odd",
    )(h, h, h, h, h, cos, slo, shi, qg, kg, wg, bg, gmat)


def _scan_kernel(q_ref, k_ref, v_ref, g_ref, la_ref, gain_ref, s0_ref, tri_ref, bmask_ref,
                 o_ref, sfin_ref, st_ref, *, ts, nsteps):
    step = pl.program_id(1)

    @pl.when(step == 0)
    def _():
        st_ref[...] = s0_ref[0]

    q, k, v = q_ref[...], k_ref[...], v_ref[...]
    tri = tri_ref[...]
    a1, a2, a3 = _split3(la_ref[...])
    cum = _dot(tri, a1) + _dot(tri, a2) + _dot(tri, a3)
    qt = q * jnp.exp(cum)
    kt = k * jnp.exp(-cum)
    hm = _half_masks()
    bmask = bmask_ref[...]
    nsub = ts // SUB
    vb = v.astype(BF16)
    elast, upd = [], []
    for u in range(nsub):
        r0 = u * SUB
        last = cum[r0 + SUB - 1:r0 + SUB, :]
        k2 = (k[r0:r0 + SUB, :] * jnp.exp(last - cum[r0:r0 + SUB, :])).astype(BF16)
        elast.append(jnp.exp(last))
        upd.append([_dot_tn(vb[r0:r0 + SUB, 2 * p * LANES:2 * (p + 1) * LANES],
                            k2[:, p * LANES:(p + 1) * LANES]) * bmask for p in range(2)])
    seen = []
    for p in range(2):
        st = st_ref[p]
        row = []
        for u in range(nsub):
            row.append(st.astype(BF16))
            st = st * elast[u][:, p * LANES:(p + 1) * LANES] + upd[u][p]
        st_ref[p] = st
        seen.append(row)
    qtb = qt.astype(BF16)
    inter = jnp.concatenate(
        [jnp.concatenate([_dot_nt(qtb[u * SUB:(u + 1) * SUB, p * LANES:(p + 1) * LANES], seen[p][u])
                          for p in range(2)], axis=1) for u in range(nsub)], axis=0)
    ktb = kt.astype(BF16)
    same_chunk_causal = tri > 0
    intra = []
    for h in range(4):
        l0 = (h // 2) * LANES
        att = _dot_nt((qt[:, l0:l0 + LANES] * hm[h % 2]).astype(BF16), ktb[:, l0:l0 + LANES])
        att = jnp.where(same_chunk_causal, att, 0.0)
        intra.append(_dot(att.astype(BF16), vb[:, h * LANES:(h + 1) * LANES]))
    o = inter + jnp.concatenate(intra, axis=1)
    g = g_ref[...]
    gain = gain_ref[...]
    outs = []
    for h in range(4):
        oh = o[:, h * LANES:(h + 1) * LANES]
        ms = jnp.mean(oh * oh, axis=-1, keepdims=True)
        outs.append(oh * lax.rsqrt(ms + EPS) * gain[:, h * LANES:(h + 1) * LANES]
                    * _silu(g[:, h * LANES:(h + 1) * LANES]))
    o_ref[...] = jnp.concatenate(outs, axis=1)

    @pl.when(step == nsteps - 1)
    def _():
        sfin_ref[0] = st_ref[...]


def _scan_consts(ts):
    r = np.arange(ts)
    tri = ((r[:, None] // SUB == r[None, :] // SUB) & (r[None, :] <= r[:, None])).astype(np.float32)
    row = np.arange(2 * LANES)[:, None] // LANES
    col = np.arange(LANES)[None, :] // HEAD
    bmask = (row == col).astype(np.float32)
    return jnp.asarray(tri, BF16), jnp.asarray(bmask, F32)


def _ignore_last_input(fn, n_in):
    def wrapped(*refs):
        return fn(*refs[:n_in], *refs[n_in + 1:])
    return wrapped


def _shared_rows(into, rows, width):
    assert into.shape == (rows, width)
    return jax.ShapeDtypeStruct((rows, width), F32), [into], [pl.BlockSpec(memory_space=pl.ANY)]


def linear_scan(q_arr, q_col, k_arr, k_col, v_arr, v_col, g_arr, g_col, la, gain, s0,
                *, row0, nb, t, ts, la_shared, out_rows, into, name):
    nsteps = t // ts
    blk0 = row0 // ts
    tri, bmask = _scan_consts(ts)
    src = lambda w, c: pl.BlockSpec((ts, w), lambda b, s, c=c: (blk0 + b * nsteps + s, c))
    la_spec = pl.BlockSpec((ts, 256), lambda b, s: (0, 0)) if la_shared else src(256, 0)
    o_shape, extra, extra_specs = _shared_rows(into, out_rows, 512)
    n_in = 9
    kern = functools.partial(_scan_kernel, ts=ts, nsteps=nsteps)
    return pl.pallas_call(
        _ignore_last_input(kern, n_in) if extra else kern,
        grid=(nb, nsteps),
        in_specs=[src(256, q_col), src(256, k_col), src(512, v_col), src(512, g_col), la_spec,
                  pl.BlockSpec((1, 512), lambda b, s: (0, 0)),
                  pl.BlockSpec((1, 2, 256, LANES), lambda b, s: (b, 0, 0, 0)),
                  pl.BlockSpec((ts, ts), lambda b, s: (0, 0)),
                  pl.BlockSpec((256, LANES), lambda b, s: (0, 0))] + extra_specs,
        out_specs=[src(512, 0), pl.BlockSpec((1, 2, 256, LANES), lambda b, s: (b, 0, 0, 0))],
        out_shape=[o_shape, jax.ShapeDtypeStruct((nb, 2, 256, LANES), F32)],
        scratch_shapes=[pltpu.VMEM((2, 256, LANES), F32)],
        input_output_aliases={n_in: 0} if extra else {},
        compiler_params=_cparams(("parallel", "arbitrary")),
        name=name,
    )(q_arr, k_arr, v_arr, g_arr, la, gain, s0, tri, bmask, *extra)


def _state_to_pairs(s):
    nb = s.shape[0]
    st = jnp.swapaxes(s, -1, -2).reshape(nb, 2, 2, LANES, HEAD)
    z = jnp.zeros_like(st[:, :, 0])
    top = jnp.concatenate([st[:, :, 0], z], axis=-1)
    bot = jnp.concatenate([z, st[:, :, 1]], axis=-1)
    return jnp.concatenate([top, bot], axis=-2)


def _pairs_to_state(sp):
    nb = sp.shape[0]
    h0 = sp[:, :, :LANES, :HEAD]
    h1 = sp[:, :, LANES:, HEAD:]
    st = jnp.stack([h0, h1], axis=2).reshape(nb, 4, LANES, HEAD)
    return jnp.swapaxes(st, -1, -2)


def _cumsum_kernel(x_ref, tri_ref, o_ref, carry_ref):
    @pl.when(pl.program_id(1) == 0)
    def _():
        carry_ref[...] = jnp.zeros_like(carry_ref)

    tri = tri_ref[...]
    a1, a2, a3 = _split3(x_ref[...])
    c = _dot(tri, a1) + _dot(tri, a2) + _dot(tri, a3) + carry_ref[0:1, :]
    o_ref[...] = c * LOG2E
    carry_ref[...] = jnp.broadcast_to(c[-1:, :], carry_ref.shape)


def row_cumsum(x, *, row0, nb, n, tb):
    r = np.arange(tb)
    tri = jnp.asarray((r[None, :] <= r[:, None]).astype(np.float32), BF16)
    nblk = n // tb
    blk0 = row0 // tb
    return pl.pallas_call(
        _cumsum_kernel,
        grid=(nb, nblk),
        in_specs=[pl.BlockSpec((tb, LANES), lambda b, j: (blk0 + b * nblk + j, 0)),
                  pl.BlockSpec((tb, tb), lambda b, j: (0, 0))],
        out_specs=pl.BlockSpec((tb, LANES), lambda b, j: (b * nblk + j, 0)),
        out_shape=jax.ShapeDtypeStruct((nb * n, LANES), F32),
        scratch_shapes=[pltpu.VMEM((SUBLANES, LANES), F32)],
        compiler_params=_cparams(("parallel", "arbitrary")),
        name="logf_cumsum",
    )(x, tri)


def _col_reduce(x, op):
    tk, tq = x.shape
    if tk > FOLD and tk % FOLD == 0:
        x = op(x.reshape(tk // FOLD, FOLD, tq), axis=0)
    return op(x, axis=0, keepdims=True)


def _values_with_ones(vt_head):
    ones = jnp.ones((ONES_ROWS, vt_head.shape[1]), F32)
    return jnp.concatenate([vt_head, ones], axis=0).astype(BF16)


def _attn_update(s, smax, vt_ext, m_ref, acc_ref, idx):
    m_prev = m_ref[idx]
    m_new = jnp.maximum(m_prev, smax)
    alpha = jnp.exp2(m_prev - m_new)
    p = jnp.exp2(s - m_new)
    acc_ref[idx] = alpha * acc_ref[idx] + _dot(vt_ext, p.astype(BF16))
    m_ref[idx] = m_new


def _eye(n):
    r = lax.broadcasted_iota(I32, (n, n), 0)
    c = lax.broadcasted_iota(I32, (n, n), 1)
    return jnp.where(r == c, 1.0, 0.0).astype(BF16)


def _transpose_exact(x, eye):
    a1, a2, a3 = _split3(x)
    return _dot_nt(eye, a1) + _dot_nt(eye, a2) + _dot_nt(eye, a3)


def _write_heads(o_ref, acc_ref, tq):
    eye = _eye(tq)
    for c in range(4):
        a0, a1 = acc_ref[2 * c], acc_ref[2 * c + 1]
        ot = jnp.concatenate([a0[:HEAD] / a0[HEAD:HEAD + 1], a1[:HEAD] / a1[HEAD:HEAD + 1]], axis=0)
        o_ref[:, c * LANES:(c + 1) * LANES] = _transpose_exact(ot, eye)


def _fox_kernel(qi_ref, kj_ref, q_ref, k_ref, v_ref, c_ref, o_ref, m_ref, acc_ref,
                *, tq, tk, q_off):
    s_id = pl.program_id(1)
    qi = qi_ref[s_id]
    kj = kj_ref[s_id]
    j_last = (q_off + (qi + 1) * tq - 1) // tk

    @pl.when(kj == 0)
    def _():
        m_ref[...] = jnp.full_like(m_ref, NEG)
        acc_ref[...] = jnp.zeros_like(acc_ref)

    def block(masked):
        visible = None
        if masked:
            kpos = kj * tk + lax.broadcasted_iota(I32, (tk, tq), 0)
            qpos = q_off + qi * tq + lax.broadcasted_iota(I32, (tk, tq), 1)
            visible = kpos <= qpos
        qs, ks = _pair_masked_operands(q_ref[...], k_ref[...])
        v = v_ref[...]
        vts = []
        for p in range(4):
            vt = v[:, p * LANES:(p + 1) * LANES].T
            vts += [vt[:HEAD], vt[HEAD:]]
        _fox_block(qs, ks, vts, c_ref[...], visible, m_ref, acc_ref)

    @pl.when(kj < j_last)
    def _():
        block(False)

    @pl.when(kj == j_last)
    def _():
        block(True)
        _write_heads(o_ref, acc_ref, tq)


def _fox_bounded_kernel(qi_ref, kj_ref, q_ref, k_ref, v_ref, c_ref, bnd_ref, o_ref, m_ref, acc_ref,
                        *, tq, tk, q_off):
    s_id = pl.program_id(1)
    qi = qi_ref[s_id]
    kj = kj_ref[s_id]
    j_last = (q_off + (qi + 1) * tq - 1) // tk

    @pl.when(kj == 0)
    def _():
        acc_ref[...] = jnp.zeros_like(acc_ref)
        bt = _transpose_exact(bnd_ref[...], _eye(LANES))
        for h in range(B_HEADS):
            m_ref[h] = bt[h:h + 1, :]

    def block(masked):
        qs, ks = _pair_masked_operands(q_ref[...], k_ref[...])
        c = c_ref[...]
        if masked:
            kpos = kj * tk + lax.broadcasted_iota(I32, (tk, tq), 0)
            qpos = q_off + qi * tq + lax.broadcasted_iota(I32, (tk, tq), 1)
            visible = kpos <= qpos
        probs = []
        for h in range(B_HEADS):
            e = _dot_nt(ks[h], qs[h]) - c[:, h:h + 1] - m_ref[h]
            if masked:
                e = jnp.where(visible, e, NEG)
            probs.append(jnp.exp2(e).astype(BF16))
        v = v_ref[...]
        for p in range(4):
            vt = v[:, p * LANES:(p + 1) * LANES].T
            for hh in range(2):
                h = 2 * p + hh
                acc_ref[h] = acc_ref[h] + _dot(_values_with_ones(vt[hh * HEAD:(hh + 1) * HEAD]), probs[h])

    @pl.when(kj < j_last)
    def _():
        block(False)

    @pl.when(kj == j_last)
    def _():
        block(True)
        _write_heads(o_ref, acc_ref, tq)


def _pair_masked_operands(q, k):
    hm = _half_masks()
    qs, ks = [], []
    for p in range(4):
        kp = k[:, p * LANES:(p + 1) * LANES].astype(BF16)
        for hh in range(2):
            qs.append((q[:, p * LANES:(p + 1) * LANES] * hm[hh]).astype(BF16))
            ks.append(kp)
    return qs, ks


def _fox_block(qs, ks, vts, c, visible, m_ref, acc_ref):
    scores, maxima = [], []
    for h in range(B_HEADS):
        s = _dot_nt(ks[h], qs[h]) - c[:, h:h + 1]
        if visible is not None:
            s = jnp.where(visible, s, NEG)
        scores.append(s)
        maxima.append(_col_reduce(s, jnp.max))
    for h in range(B_HEADS):
        _attn_update(scores[h], maxima[h], _values_with_ones(vts[h]), m_ref, acc_ref, h)


def _fox_decode_kernel(q_ref, kp_ref, vp_ref, lp_ref, kn_ref, vn_ref, ln_ref, tri_ref, o_ref,
                       m_ref, acc_ref, *, ts, past, tb):
    m_ref[...] = jnp.full_like(m_ref, NEG)
    acc_ref[...] = jnp.zeros_like(acc_ref)
    eye = _eye(LANES)
    tri = tri_ref[...]

    def cumsum(x, tri_blk, carry):
        a1, a2, a3 = _split3(x)
        return _dot(tri_blk, a1) + _dot(tri_blk, a2) + _dot(tri_blk, a3) + carry

    def transposed(v):
        return _dot_nt(eye[:v.shape[1], :v.shape[1]], v.astype(BF16))

    lp = lp_ref[...]
    carry = jnp.zeros((1, lp.shape[1]), F32)
    c_blocks = []
    for j in range(past // tb):
        cb = cumsum(lp[j * tb:(j + 1) * tb], tri, carry)
        carry = cb[-1:, :]
        c_blocks.append(cb)
    c_past = jnp.concatenate(c_blocks, axis=0) * LOG2E
    c_new = cumsum(ln_ref[:, :B_HEADS], tri[:ts, :ts], carry) * LOG2E
    q = q_ref[...]
    def attend(k, v, c, visible):
        qs, ks = _pair_masked_operands(q, k)
        vts = []
        for p in range(4):
            vt = transposed(v[:, p * LANES:(p + 1) * LANES])
            vts += [vt[:HEAD], vt[HEAD:]]
        _fox_block(qs, ks, vts, c, visible, m_ref, acc_ref)

    attend(kp_ref[...], vp_ref[...], c_past, None)
    kpos = lax.broadcasted_iota(I32, (ts, ts), 0)
    qpos = lax.broadcasted_iota(I32, (ts, ts), 1)
    attend(kn_ref[...], vn_ref[...], c_new, kpos <= qpos)
    _write_heads(o_ref, acc_ref, ts)


def fox_decode(q_arr, k_past, v_past, lf_past, k_new, v_new, v_col, lf_new, *, row0, nb, ts, past,
               out_rows, into, name):
    tb = _pick_tile(past, (256, 128, 64))
    r = np.arange(tb)
    tri = jnp.asarray((r[None, :] <= r[:, None]).astype(np.float32), BF16)
    blk0 = row0 // ts
    new = lambda w, c: pl.BlockSpec((ts, w), lambda b, c=c: (blk0 + b, c))
    old = pl.BlockSpec((past, 512), lambda b: (b, 0))
    o_shape, extra, extra_specs = _shared_rows(into, out_rows, 512)
    n_in = 8
    kern = functools.partial(_fox_decode_kernel, ts=ts, past=past, tb=tb)
    return pl.pallas_call(
        _ignore_last_input(kern, n_in) if extra else kern,
        grid=(nb,),
        in_specs=[new(512, 0), old, old, pl.BlockSpec((past, B_HEADS), lambda b: (b, 0)),
                  new(512, 0), new(512, v_col),
                  new(LANES, 0), pl.BlockSpec((tb, tb), lambda b: (0, 0))] + extra_specs,
        out_specs=new(512, 0),
        out_shape=o_shape,
        scratch_shapes=[pltpu.VMEM((8, 1, ts), F32), pltpu.VMEM((8, HEAD + ONES_ROWS, ts), F32)],
        input_output_aliases={n_in: 0} if extra else {},
        compiler_params=_cparams(("parallel",)),
        name=name,
    )(q_arr, k_past, v_past, lf_past, k_new, v_new, lf_new, tri, *extra)


def fox_attention(q_arr, k_arr, v_arr, v_col, c_arr, bound, *, q_row0, nb, tq_total, tk_total, q_off,
                  tq, tk, out_rows, into, name):
    nq = tq_total // tq
    steps = [(i, j) for i in range(nq) for j in range((q_off + (i + 1) * tq - 1) // tk + 1)]
    qi = jnp.asarray([s[0] for s in steps], I32)
    kj = jnp.asarray([s[1] for s in steps], I32)
    qb0 = q_row0 // tq
    nkb = tk_total // tk
    o_shape, extra, extra_specs = _shared_rows(into, out_rows, 512)
    qrows = lambda w: pl.BlockSpec((tq, w), lambda b, s, qi, kj: (qb0 + b * nq + qi[s], 0))
    operands = [q_arr, k_arr, v_arr, c_arr]
    in_specs = [qrows(512),
                pl.BlockSpec((tk, 512), lambda b, s, qi, kj: (b * nkb + kj[s], 0)),
                pl.BlockSpec((tk, 512), lambda b, s, qi, kj: (b * nkb + kj[s], v_col)),
                pl.BlockSpec((tk, LANES), lambda b, s, qi, kj: (b * nkb + kj[s], 0))]
    body = _fox_kernel
    if bound is not None:
        operands.append(bound)
        in_specs.append(qrows(LANES))
        body = _fox_bounded_kernel
    n_in = 2 + len(operands)
    grid_spec = pltpu.PrefetchScalarGridSpec(
        num_scalar_prefetch=2,
        grid=(nb, len(steps)),
        in_specs=in_specs + extra_specs,
        out_specs=qrows(512),
        scratch_shapes=[pltpu.VMEM((8, 1, tq), F32), pltpu.VMEM((8, HEAD + ONES_ROWS, tq), F32)],
    )
    kern = functools.partial(body, tq=tq, tk=tk, q_off=q_off)
    return pl.pallas_call(
        _ignore_last_input(kern, n_in) if extra else kern,
        grid_spec=grid_spec,
        out_shape=o_shape,
        input_output_aliases={n_in: 0} if extra else {},
        compiler_params=_cparams(("parallel", "arbitrary")),
        name=name,
    )(qi, kj, *operands, *extra)


def _head_sum_matrix():
    r = lax.broadcasted_iota(I32, (4 * LANES, LANES), 0)
    c = lax.broadcasted_iota(I32, (4 * LANES, LANES), 1)
    return jnp.where(r // HEAD == c, 1.0, 0.0).astype(BF16)


def _sortable_key(x):
    b = lax.bitcast_convert_type(x + 0.0, I32)
    return jnp.where(b < 0, b ^ jnp.int32(0x7FFFFFFF), b)


def _dsa_kernel(iq_ref, misc_ref, qc_ref, kc_ref, vc_ref, ik_ref, o_ref,
                keys_ref, half_ref, qm_ref, m_ref, acc_ref, thr_ref, jcut_ref,
                *, tq, tk, n_keys, q_off, ksel):
    i = pl.program_id(1)
    hm = _half_masks()
    qcol = lax.broadcasted_iota(I32, (1, tq), 1)
    lim = jnp.minimum(((q_off + i * tq + qcol) // CHUNK + 1) * CHUNK, n_keys)
    lim_max = jnp.minimum(((q_off + (i + 1) * tq - 1) // CHUNK + 1) * CHUNK, n_keys)
    nkb = (lim_max + tk - 1) // tk
    eye = _eye(LANES)

    def key_pos(kb):
        return kb * tk + lax.broadcasted_iota(I32, (tk, tq), 0)

    iq = iq_ref[...]
    for hd in range(IDX_HEADS):
        c0 = (hd // 2) * LANES
        qm_ref[hd] = (iq[:, c0:c0 + LANES] * hm[hd % 2]).astype(BF16)
    iwt = _transpose_exact(misc_ref[...], eye) * IDX_SCALE

    def score_body(kb, kk_max):
        k0 = pl.multiple_of(kb * tk, tk)
        ikb = ik_ref[pl.ds(k0, tk), :]
        sc = jnp.zeros((tk, tq), F32)
        for hd in range(IDX_HEADS):
            a = _dot_nt(ikb, qm_ref[hd])
            sc = sc + jnp.maximum(a, 0.0) * iwt[MISC_IW + hd:MISC_IW + hd + 1, :]
        key = jnp.where(key_pos(kb) < lim, _sortable_key(sc), INT_MIN)
        keys_ref[kb] = key
        half_ref[kb] = jnp.right_shift(key, 16).astype(I16)
        kf32 = kc_ref[pl.ds(k0, tk), :].astype(F32)
        kk = _dot((kf32 * kf32).astype(BF16), half_ones)
        return jnp.maximum(kk_max, _col_reduce(kk, jnp.max))

    lane_r = lax.broadcasted_iota(I32, (LANES, LANES), 0)
    lane_c = lax.broadcasted_iota(I32, (LANES, LANES), 1)
    half_ones = jnp.where(lane_r // HEAD == lane_c // HEAD, 1.0, 0.0).astype(BF16)
    kk_max = lax.fori_loop(0, nkb, score_body, jnp.zeros((1, LANES), F32))

    def count(pred):
        def body(kb, acc):
            hit = jnp.where(pred(keys_ref[kb], key_pos(kb)), 1.0, 0.0)
            return acc + jnp.sum(hit.reshape(tk // COUNT_FOLD, COUNT_FOLD, tq), axis=0)
        acc = lax.fori_loop(0, nkb, body, jnp.zeros((COUNT_FOLD, tq), F32))
        return jnp.sum(acc, axis=0, keepdims=True)

    def count_half(cand):
        def block_hits(kb):
            hit = jnp.where(half_ref[kb] >= cand, jnp.int16(1), jnp.int16(0))
            parts = [hit[r * COUNT_FOLD:(r + 1) * COUNT_FOLD] for r in range(tk // COUNT_FOLD)]
            while len(parts) > 1:
                parts = [a + b for a, b in zip(parts[0::2], parts[1::2])] + parts[len(parts) & ~1:]
            return parts[0]

        pairs = nkb // 2
        acc = lax.fori_loop(0, pairs, lambda j, a: a + block_hits(2 * j) + block_hits(2 * j + 1),
                            jnp.zeros((COUNT_FOLD, tq), I16))
        acc = lax.fori_loop(2 * pairs, nkb, lambda kb, a: a + block_hits(kb), acc)
        return jnp.sum(acc.astype(F32), axis=0, keepdims=True)

    def any_row(flag):
        return jnp.max(jnp.where(flag, 1.0, 0.0)) > 0.0

    thr_ref[...] = jnp.full((1, tq), INT_MIN + 1, I32)
    jcut_ref[...] = jnp.full((1, tq), 2 ** 30, I32)
    kf = float(ksel)

    @pl.when(lim_max > ksel)
    def _():
        few = lim <= ksel

        def unsettled(cres):
            return jnp.logical_and(cres != kf, jnp.logical_not(few))

        def search(first, last, group, to_half, state):
            assert (last - first) % group == 0

            def cond(state):
                t, _, _, go = state
                return jnp.logical_and(t < last, go)

            def body(state):
                t, res, cres, _ = state
                for g in range(group):
                    cand = res + jnp.left_shift(jnp.int32(1), 30 - g - t)
                    cnt = count_half(to_half(cand))
                    take = cnt >= kf
                    res = jnp.where(take, cand, res)
                    cres = jnp.where(take, cnt, cres)
                return t + group, res, cres, any_row(unsettled(cres))

            return lax.while_loop(cond, body, (jnp.int32(first),) + state)[1:]

        def high_half(x):
            return jnp.right_shift(x, 16).astype(I16)

        def low_half(x):
            return ((x & 0xFFFF) - 32768).astype(I16)

        c0 = count_half(jnp.zeros((1, tq), I16))
        res = jnp.where(c0 >= kf, 0, INT_MIN).astype(I32)
        cres = jnp.where(c0 >= kf, c0, 2.0 * n_keys)
        res, cres, go = search(0, 15, 3, high_half, (res, cres, any_row(unsettled(cres))))

        @pl.when(go)
        def _():
            rh = high_half(res)

            def relabel(kb, carry):
                hi = half_ref[kb]
                lo = low_half(keys_ref[kb])
                half_ref[kb] = jnp.where(hi > rh, jnp.int16(32767),
                                         jnp.where(hi < rh, jnp.int16(-32768), lo))
                return carry

            lax.fori_loop(0, nkb, relabel, 0)

        res, cres, open_rows = search(15, 31, 4, low_half, (res, cres, go))
        thr = jnp.maximum(res, INT_MIN + 1)
        thr_ref[...] = thr

        @pl.when(open_rows)
        def _():
            need = kf - count(lambda blk, kpos: blk > thr)
            n_eq = count(lambda blk, kpos: blk == thr)
            split = n_eq > need

            @pl.when(any_row(split))
            def _():
                def idx_body(t, lo):
                    cand = lo + jnp.left_shift(jnp.int32(1), 14 - t)
                    cnt = count(lambda blk, kpos: (blk == thr) & (kpos < cand))
                    return jnp.where(cnt < need, cand, lo)

                lo = lax.fori_loop(0, 15, idx_body, jnp.zeros((1, tq), I32))
                jcut_ref[...] = jnp.where(split, lo, 2 ** 30)

    qc = qc_ref[...]
    for hd in range(C_HEADS):
        c0 = (hd // 2) * LANES
        qm_ref[hd] = (qc[:, c0:c0 + LANES] * hm[hd % 2]).astype(BF16)
    thr = thr_ref[...]
    jcut = jcut_ref[...]

    def block_operands(kb):
        k0 = pl.multiple_of(kb * tk, tk)
        kblk = kc_ref[pl.ds(k0, tk), :]
        vt = _dot_nt(eye, vc_ref[pl.ds(k0, tk), :])
        vts = [_values_with_ones(vt[hh * HEAD:(hh + 1) * HEAD]) for hh in range(2)]
        keys = keys_ref[kb]
        sel = (keys > thr) | ((keys == thr) & (key_pos(kb) <= jcut))
        return kblk, vts, sel

    qq = _dot((qc * qc).astype(BF16), _head_sum_matrix())
    qq_t = _dot_nt(eye, qq.astype(BF16))
    bound = [jnp.sqrt(qq_t[hd:hd + 1, :] * kk_max[:, (hd % 2) * HEAD:(hd % 2) * HEAD + 1]) * BOUND_SLACK
             for hd in range(C_HEADS)]
    acc_ref[...] = jnp.zeros_like(acc_ref)

    def fast_body(kb, carry):
        kblk, vts, sel = block_operands(kb)
        keep = jnp.where(sel, 1.0, 0.0).astype(BF16)
        probs = [jnp.exp2(_dot_nt(kblk, qm_ref[hd]) - bound[hd]).astype(BF16) * keep
                 for hd in range(C_HEADS)]
        for hd in range(C_HEADS):
            acc_ref[hd] = acc_ref[hd] + _dot(vts[hd % 2], probs[hd])
        return carry

    lax.fori_loop(0, nkb, fast_body, 0)
    denom_min = acc_ref[0][HEAD:HEAD + 1, :]
    for hd in range(1, C_HEADS):
        denom_min = jnp.minimum(denom_min, acc_ref[hd][HEAD:HEAD + 1, :])

    @pl.when(jnp.min(denom_min) < DENOM_FLOOR)
    def _():
        m_ref[...] = jnp.full_like(m_ref, NEG)
        acc_ref[...] = jnp.zeros_like(acc_ref)

        def exact_body(kb, carry):
            kblk, vts, sel = block_operands(kb)
            scores = [jnp.where(sel, _dot_nt(kblk, qm_ref[hd]), NEG) for hd in range(C_HEADS)]
            maxima = [_col_reduce(s, jnp.max) for s in scores]
            for hd in range(C_HEADS):
                _attn_update(scores[hd], maxima[hd], vts[hd % 2], m_ref, acc_ref, hd)
            return carry

        lax.fori_loop(0, nkb, exact_body, 0)

    _write_heads(o_ref, acc_ref, tq)


def dsa_attention(iq_arr, misc_arr, misc_col, qc_arr, kc, vc, ik, *, q_row0, nb, tq_total,
                  n_keys, n_keys_pad, q_off, tq, tk, out_rows, into, name):
    nq = tq_total // tq
    qb0 = q_row0 // tq
    ksel = min(TOPK_MAX, n_keys // 4)
    qspec = lambda w, c: pl.BlockSpec((tq, w), lambda b, i, c=c: (qb0 + b * nq + i, c))
    kspec = pl.BlockSpec((n_keys_pad, LANES), lambda b, i: (b, 0))
    o_shape, extra, extra_specs = _shared_rows(into, out_rows, 512)
    n_in = 6
    kern = functools.partial(_dsa_kernel, tq=tq, tk=tk, n_keys=n_keys, q_off=q_off, ksel=ksel)
    return pl.pallas_call(
        _ignore_last_input(kern, n_in) if extra else kern,
        grid=(nb, nq),
        in_specs=[qspec(512, 0), qspec(LANES, misc_col), qspec(512, 0), kspec, kspec, kspec]
        + extra_specs,
        out_specs=qspec(512, 0),
        out_shape=o_shape,
        input_output_aliases={n_in: 0} if extra else {},
        scratch_shapes=[pltpu.VMEM((n_keys_pad // tk, tk, tq), I32),
                        pltpu.VMEM((n_keys_pad // tk, tk, tq), I16),
                        pltpu.VMEM((8, tq, LANES), BF16),
                        pltpu.VMEM((8, 1, tq), F32),
                        pltpu.VMEM((8, HEAD + ONES_ROWS, tq), F32),
                        pltpu.VMEM((1, tq), I32), pltpu.VMEM((1, tq), I32)],
        compiler_params=_cparams(("parallel", "arbitrary")),
        name=name,
    )(iq_arr, misc_arr, qc_arr, kc, vc, ik, *extra)


def _rope_tables(pos):
    half = HEAD // 2
    inv = jnp.power(ROPE_THETA, -jnp.arange(half, dtype=F32) / half)
    ang = pos.astype(F32)[:, None] * inv[None, :]
    cos, sin = jnp.cos(ang), jnp.sin(ang)
    z = jnp.zeros_like(sin)
    return (jnp.tile(cos, (1, 4)), jnp.tile(jnp.concatenate([-sin, z], 1), (1, 2)),
            jnp.tile(jnp.concatenate([z, sin], 1), (1, 2)))


def _head_mean_matrix():
    r = np.arange(512)
    return jnp.asarray((r[:, None] // HEAD == r[None, :] // HEAD).astype(np.float32), BF16)


def _pick_tile(n, prefs):
    for t in prefs:
        if n % t == 0:
            return t
    raise ValueError(f"no tile in {prefs} divides {n}")


def _pad_keys(x, n_pad):
    nb, n, w = x.shape
    if n_pad > n:
        x = jnp.concatenate([x, jnp.zeros((nb, n_pad - n, w), x.dtype)], axis=1)
    return x.reshape(nb * n_pad, w)


def kernel(x_prompt, x_sample, state_a, cache_b_k, cache_b_v, cache_b_logf, cache_c_k, cache_c_v,
           cache_c_idx, state_d, norm_mix, norm_ffn, even_w_in, even_b_f, even_q_norm, even_k_norm,
           even_w_out, odd_w_in, odd_w_gate_up, odd_b_gate, odd_q_norm, odd_k_norm, odd_o_norm,
           odd_w_out, ffn_w1, ffn_w2):
    bp, tp = x_prompt.shape[:2]
    nb, ts = x_sample.shape[:2]
    past = cache_b_k.shape[2]
    depth = norm_mix.shape[0]
    rp, rs = bp * tp, nb * ts
    rows = rp + rs
    assert bp == 1 and tp % 128 == 0 and ts % SUB == 0 and rp % ts == 0 and past % CHUNK == 0

    y = jnp.concatenate([x_prompt.reshape(rp, D_MODEL), x_sample.reshape(rs, D_MODEL)], axis=0)
    pos = jnp.concatenate([jnp.arange(tp, dtype=I32),
                           jnp.tile(past + jnp.arange(ts, dtype=I32), nb)])
    cos, slo, shi = _rope_tables(pos)
    gmat = _head_mean_matrix()

    tm = _pick_tile(rows, (512, 256, 128, 64, 32))
    tm_prep = _pick_tile(rows, (256, 128, 64, 32))
    ts_p = _pick_tile(tp, (256, 128, 64, 32, 16))
    tq_fox = _pick_tile(tp, (512, 256, 128))
    tk_fox = _pick_tile(tp, (512, 256, 128))
    tq_dsa = _pick_tile(tp, (256, 128))
    tk_dsa = _pick_tile(tp, (512, 256, 128))
    nk_s = past + ts
    nk_s_pad = -(-nk_s // 384) * 384
    tb_p = _pick_tile(tp, (512, 256, 128))

    lg = jnp.log1p(-jnp.exp2(-5.0 - jnp.arange(A_HEADS, dtype=F32)))
    la_ret = jnp.repeat(lg, A_DK)[None, :]
    ones_gain = jnp.ones((1, 512), F32)
    blank = lambda: jnp.zeros((rows, 512), F32)
    zero_state = jnp.zeros((bp, 2, 256, LANES), F32)

    a_p, a_s, bk_p, bk_s, bv_p, bv_s, bf_p, bf_s = [], [], [], [], [], [], [], []
    ck_p, ck_s, cv_p, cv_s, ci_p, ci_s, d_p, d_s = [], [], [], [], [], [], [], []

    for l in range(depth):
        i = l // 2
        if l % 2 == 0:
            w_in = jnp.concatenate(
                [even_w_in[i], jnp.zeros((D_MODEL, EV_WIDTH - even_w_in.shape[2]), F32)], 1).astype(BF16)
            h = norm_matmul(y, norm_mix[l], w_in, tm=tm, tn=EV_WIDTH, name="even_in_proj")
            bf = jnp.concatenate([even_b_f[i], jnp.zeros((LANES - B_HEADS,), F32)])[None, :]
            qka, qbn, kbn, lf, qq, kk, qk = prep_even(
                h, cos, slo, shi, jnp.tile(even_q_norm[i], B_HEADS)[None, :],
                jnp.tile(even_k_norm[i], B_HEADS)[None, :], bf, gmat, tm=tm_prep)
            oa, sa_p = linear_scan(qka, 0, qka, 1, h, EV_VA // 512, h, EV_GA // 512,
                                   jnp.broadcast_to(la_ret, (ts_p, 256)), ones_gain, zero_state,
                                   row0=0, nb=bp, t=tp, ts=ts_p, la_shared=True, out_rows=rows,
                                   into=blank(), name="retention_prompt")
            oa, sa_s = linear_scan(qka, 0, qka, 1, h, EV_VA // 512, h, EV_GA // 512,
                                   jnp.broadcast_to(la_ret, (ts, 256)), ones_gain,
                                   _state_to_pairs(state_a[i]),
                                   row0=rp, nb=nb, t=ts, ts=ts, la_shared=True, out_rows=rows,
                                   into=oa, name="retention_decode")
            lf8 = lf[:, :B_HEADS]
            c_p = row_cumsum(lf, row0=0, nb=bp, n=tp, tb=tb_p)
            qk_bound = jnp.sqrt(qq[:rp] * jnp.max(kk[:rp], axis=0, keepdims=True)) * BOUND_SLACK
            tight = jnp.max((qk_bound - qk[:rp])[:, :B_HEADS]) < FOX_BOUND_RANGE
            fox_prompt = functools.partial(
                fox_attention, qbn, kbn, h, EV_VB // 512, c_p, q_row0=0, nb=bp, tq_total=tp,
                tk_total=tp, q_off=0, tq=tq_fox, tk=tk_fox, out_rows=rows, into=blank())
            ob = lax.cond(tight,
                          lambda: fox_prompt(qk_bound - c_p, name="fox_prompt_bounded"),
                          lambda: fox_prompt(None, name="fox_prompt"))
            kb_new = kbn[rp:].reshape(nb, ts, 512)
            vb_new = h[rp:, EV_VB:EV_VB + 512].reshape(nb, ts, 512)
            ob = fox_decode(qbn, cache_b_k[i].reshape(nb * past, 512).astype(BF16),
                            cache_b_v[i].reshape(nb * past, 512).astype(BF16),
                            cache_b_logf[i].reshape(nb * past, B_HEADS), kbn, h, EV_VB // 512, lf,
                            row0=rp, nb=nb, ts=ts, past=past, out_rows=rows, into=ob,
                            name="fox_decode")
            w_out = even_w_out[i].astype(BF16)
            y = matmul_residual(y, [oa, ob], [w_out[:512], w_out[512:]], tm=tm, name="even_out_proj")
            a_p.append(_pairs_to_state(sa_p))
            a_s.append(_pairs_to_state(sa_s))
            bk_p.append(kbn[:rp].reshape(bp, tp, B_HEADS, B_DH))
            bk_s.append(kb_new.reshape(nb, ts, B_HEADS, B_DH))
            bv_p.append(h[:rp, EV_VB:EV_VB + 512].reshape(bp, tp, B_HEADS, B_DH))
            bv_s.append(vb_new.reshape(nb, ts, B_HEADS, B_DH))
            bf_p.append(lf8[:rp].reshape(bp, tp, B_HEADS))
            bf_s.append(lf8[rp:].reshape(nb, ts, B_HEADS))
        else:
            w = odd_w_in[i]
            offs = np.cumsum([0, 512, 128, 128, 512, 64, 8, 256, 256, 512, 512, 16])
            qc_w, kc_w, vc_w, iq_w, ik_w, iw_w, qd_w, kd_w, vd_w, gd_w, gr_w = [
                w[:, int(a):int(b)] for a, b in zip(offs[:-1], offs[1:])]
            qc_w = qc_w.reshape(D_MODEL, 2, 4, C_DH).transpose(0, 2, 1, 3).reshape(D_MODEL, 512)
            w_in = jnp.concatenate(
                [qc_w, iq_w, vd_w, gd_w, qd_w, kd_w, kc_w, vc_w, ik_w, iw_w, gr_w,
                 jnp.zeros((D_MODEL, OD_WIDTH - OD_MISC - 88), F32)], 1).astype(BF16)
            h = norm_matmul(y, norm_mix[l], w_in, tm=tm, tn=OD_WIDTH, name="odd_in_proj")
            wg = jnp.zeros((LANES, 256), F32).at[MISC_GR:MISC_GR + D_GATE_RANK].set(
                odd_w_gate_up[i]).astype(BF16)
            qcr, iqr, qkd, kcr, ikr, kcb, vcb, ikb, la = prep_odd(
                h, cos, slo, shi, jnp.tile(odd_q_norm[i], C_HEADS)[None, :],
                jnp.tile(odd_k_norm[i], C_KV_HEADS)[None, :], wg, odd_b_gate[i][None, :], gmat,
                tm=tm_prep)
            oc = dsa_attention(iqr, h, OD_MISC // LANES, qcr, kcb, vcb, ikb,
                               q_row0=0, nb=bp, tq_total=tp, n_keys=tp, n_keys_pad=tp, q_off=0,
                               tq=tq_dsa, tk=tk_dsa, out_rows=rows, into=blank(), name="dsa_prompt")

            def with_past(cache, new):
                return _pad_keys(jnp.concatenate(
                    [cache.reshape(nb, past, -1).astype(BF16), new[rp:].reshape(nb, ts, -1)], 1), nk_s_pad)

            ik_past = jnp.concatenate([cache_c_idx[i], cache_c_idx[i]], axis=-1)
            oc = dsa_attention(iqr, h, OD_MISC // LANES, qcr,
                               with_past(cache_c_k[i], kcb), with_past(cache_c_v[i], vcb),
                               with_past(ik_past, ikb),
                               q_row0=rp, nb=nb, tq_total=ts, n_keys=nk_s, n_keys_pad=nk_s_pad,
                               q_off=past, tq=ts, tk=384, out_rows=rows, into=oc, name="dsa_decode")
            gain = jnp.tile(odd_o_norm[i], D_HEADS)[None, :]
            od, sd_p = linear_scan(qkd, 0, qkd, 1, h, OD_VD // 512, h, OD_GD // 512, la, gain,
                                   zero_state, row0=0, nb=bp, t=tp, ts=ts_p, la_shared=False,
                                   out_rows=rows, into=blank(), name="gla_prompt")
            od, sd_s = linear_scan(qkd, 0, qkd, 1, h, OD_VD // 512, h, OD_GD // 512, la, gain,
                                   _state_to_pairs(state_d[i]), row0=rp, nb=nb, t=ts, ts=ts,
                                   la_shared=False, out_rows=rows, into=od, name="gla_decode")
            w_out = odd_w_out[i]
            w_oc = w_out[:512].reshape(2, 4, C_DH, D_MODEL).transpose(1, 0, 2, 3).reshape(512, D_MODEL)
            y = matmul_residual(y, [oc, od], [w_oc.astype(BF16), w_out[512:].astype(BF16)], tm=tm,
                                name="odd_out_proj")
            d_p.append(_pairs_to_state(sd_p))
            d_s.append(_pairs_to_state(sd_s))
            ck_p.append(kcr[:rp].reshape(bp, tp, C_KV_HEADS, C_DH))
            ck_s.append(kcr[rp:].reshape(nb, ts, C_KV_HEADS, C_DH))
            cv_p.append(h[:rp, OD_KVC + LANES:OD_KVC + 2 * LANES].reshape(bp, tp, C_KV_HEADS, C_DH))
            cv_s.append(h[rp:, OD_KVC + LANES:OD_KVC + 2 * LANES].reshape(nb, ts, C_KV_HEADS, C_DH))
            ci_p.append(ikr[:rp, :IDX_DIM].reshape(bp, tp, IDX_DIM))
            ci_s.append(ikr[rp:, :IDX_DIM].reshape(nb, ts, IDX_DIM))
        hid = norm_matmul(y, norm_ffn[l], ffn_w1[l].astype(BF16), tm=tm, tn=D_FF, relu2=True,
                          out_dtype=BF16, name="mlp_up")
        y = matmul_residual(y, [hid], [ffn_w2[l].astype(BF16)], tm=tm, name="mlp_down")

    return (y[:rp].reshape(bp, tp, D_MODEL), y[rp:].reshape(nb, ts, D_MODEL),
            jnp.stack(a_p), jnp.stack(a_s), jnp.stack(bk_p), jnp.stack(bk_s),
            jnp.stack(bv_p), jnp.stack(bv_s), jnp.stack(bf_p), jnp.stack(bf_s),
            jnp.stack(ck_p), jnp.stack(ck_s), jnp.stack(cv_p), jnp.stack(cv_s),
            jnp.stack(ci_p), jnp.stack(ci_s), jnp.stack(d_p), jnp.stack(d_s))
```
